```python
import math
import jax
import jax.numpy as jnp
from jax import lax
import numpy as np

D_MODEL = 1024
BATCH = 16
SEQ = 2048
DEPTH = 4

GRID_W = 64
CTX_LEN = 256
N_MIXERS = 4
ALPHA = (2 * DEPTH) ** 0.25
BETA = (8 * DEPTH) ** -0.25
LN_EPS = 1e-5
ROPE_BASE = 10000.0

GDN_HEADS = 8
GDN_DK = D_MODEL // GDN_HEADS
GDN_DV = D_MODEL // GDN_HEADS
GDN_CONV = 3
GDN_CHUNK = 64
GDN_IN = 2 * GDN_HEADS * GDN_DK + 2 * GDN_HEADS * GDN_DV + 4 * GDN_HEADS

MLSTM_HEADS = 4
MLSTM_DQK = D_MODEL // (2 * MLSTM_HEADS)
MLSTM_DV = D_MODEL // MLSTM_HEADS
MLSTM_CHUNK = 64
MLSTM_IN = 2 * MLSTM_HEADS * MLSTM_DQK + 2 * MLSTM_HEADS * MLSTM_DV + 4 * MLSTM_HEADS

RWKV_HEAD = 64
RWKV_HEADS = D_MODEL // RWKV_HEAD
RWKV_DECAY_LORA = 64
RWKV_AAA_LORA = 64
RWKV_GATE_LORA = 128
RWKV_GN_EPS = 64e-5

NA_HEADS = 16
NA_DH = D_MODEL // NA_HEADS
NA_WIN_ROWS = 8
NA_WIN_COLS = 16
NA_COL_BLOCK = 16
NA_BAND = NA_COL_BLOCK + NA_WIN_COLS

N_EXPERTS = 16
N_GROUPS = 4
TOP_K = 2
D_EXPERT = 512

kernel_name = 'hybrid_flow_backbone_gdn_mlstm_rwkv7_natten_moe'

F32 = jnp.float32


def _heads(t, n_heads):
    b, t_len, hd = t.shape
    return t.reshape(b, t_len, n_heads, hd // n_heads).transpose(0, 2, 1, 3)


def _merge(t):
    b, h, t_len, d = t.shape
    return t.transpose(0, 2, 1, 3).reshape(b, t_len, h * d)


def layer_norm(x, g, b):
    xf = x.astype(F32)
    mu = jnp.mean(xf, axis=-1, keepdims=True)
    var = jnp.mean(jnp.square(xf - mu), axis=-1, keepdims=True)
    return ((xf - mu) * lax.rsqrt(var + LN_EPS) * g.astype(F32) + b.astype(F32)).astype(x.dtype)


def head_norm(t, eps):
    mu = jnp.mean(t, axis=-1, keepdims=True)
    var = jnp.mean(jnp.square(t - mu), axis=-1, keepdims=True)
    return (t - mu) * lax.rsqrt(var + eps)


def head_rms(t, eps=1e-6):
    return t * lax.rsqrt(jnp.mean(t * t, axis=-1, keepdims=True) + eps)


def l2norm(t, eps=1e-6):
    return t * lax.rsqrt(jnp.sum(t * t, axis=-1, keepdims=True) + eps)


def axial_rope(t):
    t_len, dh = t.shape[-2], t.shape[-1]
    half = dh // 2
    quarter = half // 2
    pos = jnp.arange(t_len)
    row = (pos // GRID_W).astype(F32)
    col = (pos % GRID_W).astype(F32)
    inv_freq = ROPE_BASE ** (-jnp.arange(quarter, dtype=F32) / quarter)

    def rot(xa, p):
        ang = p[:, None] * inv_freq[None, :]
        cos, sin = jnp.cos(ang), jnp.sin(ang)
        x1, x2 = xa[..., :quarter], xa[..., quarter:]
        return jnp.concatenate([x1 * cos - x2 * sin, x1 * sin + x2 * cos], axis=-1)

    return jnp.concatenate([rot(t[..., :half], row), rot(t[..., half:], col)], axis=-1)


def centred_dwconv(x, w):
    k = w.shape[0]
    return lax.conv_general_dilated(
        x, w[:, None, :].astype(x.dtype), window_strides=(1,), padding=[(k // 2, k // 2)],
        dimension_numbers=('NWC', 'WIO', 'NWC'), feature_group_count=x.shape[-1])


def centred_token_shift(x):
    xp = jnp.pad(x, ((0, 0), (1, 1), (0, 0)))
    return 0.5 * (xp[:, :-2] + xp[:, 2:])


def _flip_t(t, axis):
    return jnp.flip(t, axis=axis)


def gdn_chunked(q, k, v, log_alpha, beta, s0):
    b, h, t_len, dk = q.shape
    dv = v.shape[-1]
    c = GDN_CHUNK
    n = t_len // c
    ch = lambda t: t.reshape(b, h, n, c, *t.shape[3:])
    q, k, v, log_alpha, beta = map(ch, (q, k, v, log_alpha, beta))
    incl = jnp.tril(jnp.ones((c, c), dtype=bool))
    strict = jnp.tril(jnp.ones((c, c), dtype=bool), -1)
    g = jnp.cumsum(log_alpha, axis=-1)
    gamma = jnp.exp(jnp.where(incl, g[..., :, None] - g[..., None, :], -jnp.inf))
    a_mat = jnp.where(strict, beta[..., :, None] * jnp.einsum('bhnid,bhnjd->bhnij', k, k) * gamma, 0.0)
    rhs = jnp.concatenate([beta[..., None] * v, (beta * jnp.exp(g))[..., None] * k], axis=-1)
    sol = lax.linalg.triangular_solve(a_mat, rhs, left_side=True, lower=True, unit_diagonal=True)
    u, w = sol[..., :dv], sol[..., dv:]
    p_mat = jnp.where(incl, jnp.einsum('bhnid,bhnjd->bhnij', q, k) * gamma, 0.0)
    q_dec = q * jnp.exp(g)[..., None]
    k_dec = k * jnp.exp(g[..., -1:] - g)[..., None]
    g_end = jnp.exp(g[..., -1])

    def step(s, xs):
        u_c, w_c, p_c, qd_c, kd_c, ge_c = xs
        delta = u_c - jnp.einsum('bhck,bhkv->bhcv', w_c, s)
        o = jnp.einsum('bhck,bhkv->bhcv', qd_c, s) + jnp.einsum('bhij,bhjv->bhiv', p_c, delta)
        s = ge_c[..., None, None] * s + jnp.einsum('bhck,bhcv->bhkv', kd_c, delta)
        return s, o

    xs = tuple(jnp.moveaxis(t, 2, 0) for t in (u, w, p_mat, q_dec, k_dec, g_end))
    s, o = lax.scan(step, s0, xs)
    return jnp.moveaxis(o, 0, 2).reshape(b, h, t_len, dv), s


def gdn_mixer(h_lat, h_ctx, w_in, conv_w, a_log, dt_bias, norm_g, w_out):
    wq = GDN_HEADS * GDN_DK
    wv = GDN_HEADS * GDN_DV

    def prep(h, rope):
        b, t_len, _ = h.shape
        z = h @ w_in
        qkv = jax.nn.silu(centred_dwconv(z[..., :2 * wq + wv], conv_w)).astype(F32)
        q = l2norm(_heads(qkv[..., :wq], GDN_HEADS))
        k = l2norm(_heads(qkv[..., wq:2 * wq], GDN_HEADS))
        v = _heads(qkv[..., 2 * wq:], GDN_HEADS)
        if rope:
            q, k = axial_rope(q), axial_rope(k)
        gate = z[..., 2 * wq + wv:2 * wq + 2 * wv]
        ab = z[..., 2 * wq + 2 * wv:].astype(F32).reshape(b, t_len, 2, 2, GDN_HEADS).transpose(2, 3, 0, 4, 1)
        log_alpha = -jnp.exp(a_log.astype(F32))[:, None, :, None] * jax.nn.softplus(
            ab[:, 0] + dt_bias.astype(F32)[:, None, :, None])
        beta = jax.nn.sigmoid(ab[:, 1])
        return q * GDN_DK ** -0.5, k, v, gate, log_alpha, beta

    qc, kc, vc, gc, lac, bc = prep(h_ctx, False)
    ql, kl, vl, gl, lal, bl = prep(h_lat, True)
    s0 = jnp.zeros((h_lat.shape[0], GDN_HEADS, GDN_DK, GDN_DV), F32)
    fl = lambda t: _flip_t(t, 2)
    oc_f, sc_f = gdn_chunked(qc, kc, vc, lac[0], bc[0], s0)
    ol_f, _ = gdn_chunked(ql, kl, vl, lal[0], bl[0], sc_f)
    oc_b, sc_b = gdn_chunked(fl(qc), fl(kc), fl(vc), fl(lac[1]), fl(bc[1]), s0)
    ol_b, _ = gdn_chunked(fl(ql), fl(kl), fl(vl), fl(lal[1]), fl(bl[1]), sc_b)

    def out(o, gate, like):
        y = _merge(head_rms(o) * norm_g.astype(F32)) * jax.nn.silu(gate.astype(F32))
        return y.astype(like.dtype) @ w_out

    return out(ol_f + fl(ol_b), gl, h_lat), out(oc_f + fl(oc_b), gc, h_ctx)


def mlstm_chunked(q, k, v, i_pre, log_f, state):
    b, h, t_len, dqk = q.shape
    c = MLSTM_CHUNK
    n = t_len // c
    ch = lambda t: t.reshape(b, h, n, c, *t.shape[3:])
    q, k, v, i_pre, log_f = map(ch, (q, k, v, i_pre, log_f))
    incl = jnp.tril(jnp.ones((c, c), dtype=bool))
    bcum = jnp.cumsum(log_f, axis=-1)
    log_d = jnp.where(incl, bcum[..., :, None] - bcum[..., None, :] + i_pre[..., None, :], -jnp.inf)
    m_intra = jnp.max(log_d, axis=-1)
    qk = jnp.einsum('bhnid,bhnjd->bhnij', q, k)
    log_end = bcum[..., -1:] - bcum + i_pre
    m_end = jnp.max(log_end, axis=-1)
    b_last = bcum[..., -1]

    def step(carry, xs):
        c_st, n_st, m_st = carry
        q_c, k_c, v_c, b_c, ld_c, mi_c, qk_c, le_c, me_c, bl_c = xs
        m_row = jnp.maximum(b_c + m_st[..., None], mi_c)
        w_intra = jnp.exp(ld_c - m_row[..., None]) * qk_c
        w_state = jnp.exp(b_c + m_st[..., None] - m_row)
        num = w_state[..., None] * jnp.einsum('bhck,bhkv->bhcv', q_c, c_st) + jnp.einsum('bhij,bhjv->bhiv', w_intra, v_c)
        den = w_state * jnp.einsum('bhck,bhk->bhc', q_c, n_st) + jnp.sum(w_intra, axis=-1)
        h_c = num / jnp.maximum(jnp.abs(den), jnp.exp(-m_row))[..., None]
        m_new = jnp.maximum(bl_c + m_st, me_c)
        decay = jnp.exp(bl_c + m_st - m_new)
        k_w = k_c * jnp.exp(le_c - m_new[..., None])[..., None]
        c_st = decay[..., None, None] * c_st + jnp.einsum('bhck,bhcv->bhkv', k_w, v_c)
        n_st = decay[..., None] * n_st + jnp.sum(k_w, axis=2)
        return (c_st, n_st, m_new), h_c

    xs = tuple(jnp.moveaxis(t, 2, 0) for t in (q, k, v, bcum, log_d, m_intra, qk, log_end, m_end, b_last))
    state, hs = lax.scan(step, state, xs)
    return jnp.moveaxis(hs, 0, 2).reshape(b, h, t_len, v.shape[-1]), state


def mlstm_mixer(h_lat, h_ctx, w_in, gate_b, norm_g, w_out):
    wq = MLSTM_HEADS * MLSTM_DQK
    wv = MLSTM_HEADS * MLSTM_DV

    def prep(h, rope):
        b, t_len, _ = h.shape
        z = h @ w_in
        q = _heads(z[..., :wq].astype(F32), MLSTM_HEADS)
        k = _heads(z[..., wq:2 * wq].astype(F32), MLSTM_HEADS)
        v = _heads(z[..., 2 * wq:2 * wq + wv].astype(F32), MLSTM_HEADS)
        o_gate = z[..., 2 * wq + wv:2 * wq + 2 * wv]
        gates = z[..., 2 * wq + 2 * wv:].astype(F32).reshape(b, t_len, 2, 2, MLSTM_HEADS).transpose(2, 3, 0, 4, 1)
        gates = gates + gate_b.astype(F32)[:, :, None, :, None]
        if rope:
            q, k = axial_rope(q), axial_rope(k)
        return q * MLSTM_DQK ** -0.5, k, v, o_gate, gates[:, 0], jax.nn.log_sigmoid(gates[:, 1])

    qc, kc, vc, oc, ic, fc = prep(h_ctx, False)
    ql, kl, vl, ol, il, fl_g = prep(h_lat, True)
    b = h_lat.shape[0]
    s0 = (jnp.zeros((b, MLSTM_HEADS, MLSTM_DQK, MLSTM_DV), F32),
          jnp.zeros((b, MLSTM_HEADS, MLSTM_DQK), F32),
          jnp.zeros((b, MLSTM_HEADS), F32))
    fl = lambda t: _flip_t(t, 2)
    hc_f, sc_f = mlstm_chunked(qc, kc, vc, ic[0], fc[0], s0)
    hl_f, _ = mlstm_chunked(ql, kl, vl, il[0], fl_g[0], sc_f)
    hc_b, sc_b = mlstm_chunked(fl(qc), fl(kc), fl(vc), fl(ic[1]), fl(fc[1]), s0)
    hl_b, _ = mlstm_chunked(fl(ql), fl(kl), fl(vl), fl(il[1]), fl(fl_g[1]), sc_b)

    def out(hsum, o_gate, like):
        y = _merge(head_norm(hsum, 1e-6)) * norm_g.astype(F32) * jax.nn.sigmoid(o_gate.astype(F32))
        return y.astype(like.dtype) @ w_out

    return out(hl_f + fl(hl_b), ol, h_lat), out(hc_f + fl(hc_b), oc, h_ctx)


def rwkv7_scan(r, log_w, k, v, kk, a, s0):
    def step(s, xs):
        r_t, lw_t, k_t, v_t, kk_t, a_t = xs
        sa = jnp.einsum('bhvk,bhk->bhv', s, -kk_t)
        s = (s * jnp.exp(lw_t)[:, :, None, :] + sa[..., None] * (kk_t * a_t)[:, :, None, :]
             + v_t[..., None] * k_t[:, :, None, :])
        return s, jnp.einsum('bhvk,bhk->bhv', s, r_t)

    xs = tuple(jnp.moveaxis(t, 1, 0) for t in (r, log_w, k, v, kk, a))
    s, y = lax.scan(step, s0, xs)
    return jnp.moveaxis(y, 0, 1), s


def rwkv_mixer(h_lat, h_ctx, mu, w_rkv, w0, w1, w2, a0, a1, a2, g1, g2, k_k, k_a, r_k, lnx_g, lnx_b, w_out):
    nh, hd = RWKV_HEADS, RWKV_HEAD

    def prep(h):
        b, t_len, _ = h.shape
        heads = lambda t: t.astype(F32).reshape(b, t_len, nh, hd)
        dx = centred_token_shift(h) - h
        xr, xw, xk, xv, xa, xg = [h + dx * mu[j] for j in range(6)]
        rkv = jnp.einsum('sbtd,sde->sbte', jnp.stack([xr, xk, xv]), w_rkv)
        r, k, v = heads(rkv[0]), heads(rkv[1]), heads(rkv[2])
        g = (jax.nn.sigmoid(xg @ g1) @ g2).astype(F32)
        kk = l2norm(k * k_k.astype(F32).reshape(nh, hd))
        dirs = []
        for d in range(2):
            w_raw = -jax.nn.softplus(-(w0[d] + jnp.tanh(xw @ w1[d]) @ w2[d])) - 0.5
            a = heads(jax.nn.sigmoid(a0[d] + (xa @ a1[d]) @ a2[d]))
            k_d = k * (1 + (a - 1) * k_a.astype(F32).reshape(nh, hd))
            dirs.append((-jnp.exp(heads(w_raw)), a, k_d))
        return r, v, kk, g, dirs

    def run(r, v, kk, dparams, s0, reverse):
        log_w, a, k_d = dparams
        if reverse:
            fl = lambda t: _flip_t(t, 1)
            y, s = rwkv7_scan(fl(r), fl(log_w), fl(k_d), fl(v), fl(kk), fl(a), s0)
            return fl(y), s
        return rwkv7_scan(r, log_w, k_d, v, kk, a, s0)

    rc, vc, kkc, gc, dc = prep(h_ctx)
    rl, vl, kkl, gl, dl = prep(h_lat)
    s0 = jnp.zeros((h_lat.shape[0], nh, hd, hd), F32)
    yc_f, sc_f = run(rc, vc, kkc, dc[0], s0, False)
    yl_f, _ = run(rl, vl, kkl, dl[0], sc_f, False)
    yc_b, sc_b = run(rc, vc, kkc, dc[1], s0, True)
    yl_b, _ = run(rl, vl, kkl, dl[1], sc_b, True)

    def out(y, r, v, g, dirs, like):
        b, t_len = y.shape[0], y.shape[1]
        yn = head_norm(y, RWKV_GN_EPS)
        bonus = sum(jnp.sum(r * k_d * r_k.astype(F32), axis=-1, keepdims=True) * v for (_, _, k_d) in dirs)
        yo = (yn.reshape(b, t_len, nh * hd) * lnx_g.astype(F32) + lnx_b.astype(F32)
              + bonus.reshape(b, t_len, nh * hd)) * g
        return yo.astype(like.dtype) @ w_out

    return out(yl_f + yl_b, rl, vl, gl, dl, h_lat), out(yc_f + yc_b, rc, vc, gc, dc, h_ctx)


def na_mixer(h_lat, h_ctx, w_in, rpb, w_out, need_ctx_out):
    nh, dh = NA_HEADS, NA_DH
    b, t_len, d_model = h_lat.shape
    rows = t_len // GRID_W
    wr = min(NA_WIN_ROWS, rows)
    n_cb = GRID_W // NA_COL_BLOCK
    scale = dh ** -0.5
    ql, kl, vl = (_heads(t, nh) for t in jnp.split(h_lat @ w_in, 3, axis=-1))
    kc, vc = (_heads(t, nh) for t in jnp.split(h_ctx @ w_in[:, d_model:], 2, axis=-1))
    grid = lambda t: t.reshape(b, nh, rows, GRID_W, dh)
    qg, kg, vg = grid(ql), grid(kl), grid(vl)
    cols = np.arange(GRID_W).reshape(n_cb, NA_COL_BLOCK)
    win_c0 = np.clip(cols - NA_WIN_COLS // 2, 0, GRID_W - NA_WIN_COLS)
    band_c0 = np.clip(np.arange(n_cb) * NA_COL_BLOCK - NA_WIN_COLS // 2, 0, GRID_W - NA_BAND)
    band_cols = band_c0[:, None] + np.arange(NA_BAND)
    col_mask = ((band_cols[:, None, :] >= win_c0[..., None])
                & (band_cols[:, None, :] < win_c0[..., None] + NA_WIN_COLS))
    dc_idx = np.clip(band_cols[:, None, :] - cols[..., None] + NA_WIN_COLS - 1, 0, 2 * NA_WIN_COLS - 2)
    rpb_c = rpb[:, :, dc_idx]

    def row_fn(r):
        r0 = jnp.clip(r - NA_WIN_ROWS // 2, 0, rows - wr)
        kb = lax.dynamic_slice_in_dim(kg, r0, wr, axis=2)[:, :, :, band_cols]
        vb = lax.dynamic_slice_in_dim(vg, r0, wr, axis=2)[:, :, :, band_cols]
        qr = lax.dynamic_index_in_dim(qg, r, axis=2, keepdims=False).reshape(b, nh, n_cb, NA_COL_BLOCK, dh)
        s_lat = jnp.einsum('bhjqd,bhijkd->bhjqik', qr, kb).astype(F32) * scale
        dr_idx = r0 + jnp.arange(wr) - r + NA_WIN_ROWS - 1
        bias = jnp.take(rpb_c, dr_idx, axis=1).transpose(0, 2, 3, 1, 4).astype(F32)
        s_lat = jnp.where(col_mask[:, :, None, :], s_lat + bias, -jnp.inf).reshape(b, nh, n_cb, NA_COL_BLOCK, wr * NA_BAND)
        s_ctx = jnp.einsum('bhjqd,bhkd->bhjqk', qr, kc).astype(F32) * scale
        pr = jax.nn.softmax(jnp.concatenate([s_lat, s_ctx], axis=-1), axis=-1).astype(vg.dtype)
        p_lat = pr[..., :wr * NA_BAND].reshape(b, nh, n_cb, NA_COL_BLOCK, wr, NA_BAND)
        o = (jnp.einsum('bhjqik,bhijkd->bhjqd', p_lat, vb)
             + jnp.einsum('bhjqk,bhkd->bhjqd', pr[..., wr * NA_BAND:], vc))
        return o.reshape(b, nh, GRID_W, dh)

    o_rows = lax.map(row_fn, jnp.arange(rows))
    o_lat = o_rows.transpose(1, 2, 0, 3, 4).reshape(b, nh, t_len, dh)
    out_lat = _merge(o_lat) @ w_out
    if not need_ctx_out:
        return out_lat, None
    qc = _heads(h_ctx @ w_in[:, :d_model], nh)
    pc = jax.nn.softmax(jnp.einsum('bhqd,bhkd->bhqk', qc, kc).astype(F32) * scale, axis=-1).astype(vc.dtype)
    return out_lat, _merge(jnp.einsum('bhqk,bhkd->bhqd', pc, vc)) @ w_out


def moe(h, router_w, router_b, w1, w3, w2):
    hf = h.astype(F32)
    probs = jax.nn.softmax(hf @ router_w.astype(F32), axis=-1)
    sel = probs + router_b.astype(F32)
    per_group = N_EXPERTS // N_GROUPS
    group_score = jnp.sum(lax.top_k(sel.reshape(*sel.shape[:-1], N_GROUPS, per_group), TOP_K)[0], axis=-1)
    best = jnp.argmax(group_score, axis=-1)
    in_group = (jnp.arange(N_EXPERTS) // per_group) == best[..., None]
    _, idx = lax.top_k(jnp.where(in_group, sel, -jnp.inf), TOP_K)
    w_sel = jnp.take_along_axis(probs, idx, axis=-1)
    w_sel = w_sel / jnp.sum(w_sel, axis=-1, keepdims=True)
    combine = jnp.sum(jax.nn.one_hot(idx, N_EXPERTS, dtype=F32) * w_sel[..., None], axis=-2)
    out = jnp.zeros_like(hf)
    for e in range(N_EXPERTS):
        hid = jax.nn.silu(h @ w1[e]) * (h @ w3[e])
        out = out + combine[..., e:e + 1] * (hid @ w2[e]).astype(F32)
    return out.astype(h.dtype)


def setup_inputs(seed: int = 0) -> dict:
    key = jax.random.key(seed)
    ks = iter(jax.random.split(key, 64))
    d = D_MODEL

    def nrm(shape, scale):
        return jax.random.normal(next(ks), shape, F32) * scale

    def unif(shape, lo, hi):
        return jax.random.uniform(next(ks), shape, F32, lo, hi)

    gdn_dt = jnp.exp(unif((2, GDN_HEADS), math.log(1e-3), math.log(1e-1)))
    return {
        'x': nrm((BATCH, SEQ, d), 1.0),
        'c': nrm((BATCH, d), 1.0),
        'ctx': nrm((BATCH, CTX_LEN, d), 1.0),
        'c_ctx': nrm((d,), 1.0),
        'ada_w': nrm((DEPTH, d, 6 * d), 0.5 * d ** -0.5),
        'ada_b': nrm((DEPTH, 6 * d), 0.02),
        'ln_g': 1.0 + nrm((DEPTH, 2, d), 0.05),
        'ln_b': nrm((DEPTH, 2, d), 0.02),
        'router_w': nrm((d, N_EXPERTS), d ** -0.5),
        'router_b': nrm((N_EXPERTS,), 0.01),
        'moe_w1': nrm((DEPTH, N_EXPERTS, d, D_EXPERT), d ** -0.5),
        'moe_w3': nrm((DEPTH, N_EXPERTS, d, D_EXPERT), d ** -0.5),
        'moe_w2': nrm((DEPTH, N_EXPERTS, D_EXPERT, d), D_EXPERT ** -0.5 * BETA),
        'gdn_w_in': nrm((d, GDN_IN), d ** -0.5),
        'gdn_conv': nrm((GDN_CONV, 2 * GDN_HEADS * GDN_DK + GDN_HEADS * GDN_DV), GDN_CONV ** -0.5),
        'gdn_a_log': jnp.log(unif((2, GDN_HEADS), 1.0, 16.0)),
        'gdn_dt_bias': gdn_dt + jnp.log(-jnp.expm1(-gdn_dt)),
        'gdn_norm_g': 1.0 + nrm((GDN_DV,), 0.05),
        'gdn_w_out': nrm((GDN_HEADS * GDN_DV, d), (GDN_HEADS * GDN_DV) ** -0.5 * BETA),
        'mlstm_w_in': nrm((d, MLSTM_IN), d ** -0.5),
        'mlstm_gate_b': jnp.stack([nrm((2, MLSTM_HEADS), 0.5), unif((2, MLSTM_HEADS), 3.0, 6.0)], axis=1),
        'mlstm_norm_g': 1.0 + nrm((MLSTM_HEADS * MLSTM_DV,), 0.05),
        'mlstm_w_out': nrm((MLSTM_HEADS * MLSTM_DV, d), (MLSTM_HEADS * MLSTM_DV) ** -0.5 * BETA),
        'rwkv_mu': unif((6, d), 0.0, 1.0),
        'rwkv_w_rkv': nrm((3, d, d), d ** -0.5),
        'rwkv_w0': unif((2, d), -6.0, 1.0),
        'rwkv_w1': nrm((2, d, RWKV_DECAY_LORA), d ** -0.5),
        'rwkv_w2': nrm((2, RWKV_DECAY_LORA, d), 0.5 * RWKV_DECAY_LORA ** -0.5),
        'rwkv_a0': nrm((2, d), 0.1),
        'rwkv_a1': nrm((2, d, RWKV_AAA_LORA), d ** -0.5),
        'rwkv_a2': nrm((2, RWKV_AAA_LORA, d), RWKV_AAA_LORA ** -0.5),
        'rwkv_g1': nrm((d, RWKV_GATE_LORA), d ** -0.5),
        'rwkv_g2': nrm((RWKV_GATE_LORA, d), RWKV_GATE_LORA ** -0.5),
        'rwkv_k_k': 0.85 + nrm((d,), 0.05),
        'rwkv_k_a': 1.0 + nrm((d,), 0.05),
        'rwkv_r_k': nrm((RWKV_HEADS, RWKV_HEAD), 0.1),
        'rwkv_lnx_g': 1.0 + nrm((d,), 0.05),
        'rwkv_lnx_b': nrm((d,), 0.02),
        'rwkv_w_out': nrm((d, d), d ** -0.5 * BETA),
        'na_w_in': nrm((d, 3 * d), d ** -0.5),
        'na_rpb': nrm((NA_HEADS, 2 * NA_WIN_ROWS - 1, 2 * NA_WIN_COLS - 1), 0.5),
        'na_w_out': nrm((d, d), d ** -0.5 * BETA),
    }


def reference(x, c, ctx, c_ctx, ada_w, ada_b, ln_g, ln_b, router_w, router_b,
              moe_w1, moe_w3, moe_w2,
              gdn_w_in, gdn_conv, gdn_a_log, gdn_dt_bias, gdn_norm_g, gdn_w_out,
              mlstm_w_in, mlstm_gate_b, mlstm_norm_g, mlstm_w_out,
              rwkv_mu, rwkv_w_rkv, rwkv_w0, rwkv_w1, rwkv_w2, rwkv_a0, rwkv_a1, rwkv_a2,
              rwkv_g1, rwkv_g2, rwkv_k_k, rwkv_k_a, rwkv_r_k, rwkv_lnx_g, rwkv_lnx_b, rwkv_w_out,
              na_w_in, na_rpb, na_w_out):
    for i in range(DEPTH):
        last = i == DEPTH - 1
        kind = i % N_MIXERS
        mod_lat = jax.nn.silu(c) @ ada_w[i] + ada_b[i]
        mod_ctx = jax.nn.silu(c_ctx) @ ada_w[i] + ada_b[i]
        sh1, sc1, g1, sh2, sc2, g2 = jnp.split(mod_lat[:, None, :], 6, axis=-1)
        csh1, csc1, cg1, csh2, csc2, cg2 = jnp.split(mod_ctx, 6, axis=-1)
        h_lat = x * (1 + sc1) + sh1
        h_ctx = ctx * (1 + csc1) + csh1
        if kind == 0:
            o_lat, o_ctx = gdn_mixer(h_lat, h_ctx, gdn_w_in, gdn_conv, gdn_a_log, gdn_dt_bias, gdn_norm_g, gdn_w_out)
        elif kind == 1:
            o_lat, o_ctx = mlstm_mixer(h_lat, h_ctx, mlstm_w_in, mlstm_gate_b, mlstm_norm_g, mlstm_w_out)
        elif kind == 2:
            o_lat, o_ctx = rwkv_mixer(h_lat, h_ctx, rwkv_mu, rwkv_w_rkv, rwkv_w0, rwkv_w1, rwkv_w2,
                                      rwkv_a0, rwkv_a1, rwkv_a2, rwkv_g1, rwkv_g2, rwkv_k_k, rwkv_k_a,
                                      rwkv_r_k, rwkv_lnx_g, rwkv_lnx_b, rwkv_w_out)
        else:
            o_lat, o_ctx = na_mixer(h_lat, h_ctx, na_w_in, na_rpb, na_w_out, not last)
        x = layer_norm(ALPHA * x + g1 * o_lat, ln_g[i, 0], ln_b[i, 0])
        f_lat = moe(x * (1 + sc2) + sh2, router_w, router_b, moe_w1[i], moe_w3[i], moe_w2[i])
        x = layer_norm(ALPHA * x + g2 * f_lat, ln_g[i, 1], ln_b[i, 1])
        if not last:
            ctx = layer_norm(ALPHA * ctx + cg1 * o_ctx, ln_g[i, 0], ln_b[i, 0])
            f_ctx = moe(ctx * (1 + csc2) + csh2, router_w, router_b, moe_w1[i], moe_w3[i], moe_w2[i])
            ctx = layer_norm(ALPHA * ctx + cg2 * f_ctx, ln_g[i, 1], ln_b[i, 1])
    return x
```

```python
import functools
import math

import numpy as np
import jax
import jax.numpy as jnp
from jax import lax
from jax.experimental import pallas as pl
from jax.experimental.pallas import tpu as pltpu

F32 = jnp.float32
BF16 = jnp.bfloat16

D_MODEL = 1024
DEPTH = 4
GRID_W = 64
CTX_LEN = 256
ALPHA = (2 * DEPTH) ** 0.25
LN_EPS = 1e-5
ROPE_BASE = 10000.0

GDN_HEADS = 8
GDN_DK = 128
GDN_DV = 128
MLSTM_HEADS = 4
MLSTM_DQK = 128
MLSTM_DV = 256
RWKV_HEAD = 64
RWKV_HEADS = 16
RWKV_GN_EPS = 64e-5
NA_HEADS = 16
NA_DH = 64
NA_WIN_ROWS = 8
NA_WIN_COLS = 16
N_EXPERTS = 16
N_GROUPS = 4
D_EXPERT = 512

CHUNK = 64
TOKEN_TILE = 256
N_CHUNK_COLS = 512
FFN_TILE = 512
NEG = -1e30
VMEM_LIMIT = 56 * 1024 * 1024

_HI = lax.Precision.HIGHEST


def _cparams(sem):
    return pltpu.CompilerParams(dimension_semantics=sem, vmem_limit_bytes=VMEM_LIMIT)


def _mm(a, b):
    return jnp.dot(a.astype(BF16), b.astype(BF16), preferred_element_type=F32)


def _mm_nt(a, b):
    return lax.dot_general(a.astype(BF16), b.astype(BF16), (((1,), (1,)), ((), ())),
                           preferred_element_type=F32)


def _mm_tn(a, b):
    return lax.dot_general(a.astype(BF16), b.astype(BF16), (((0,), (0,)), ((), ())),
                           preferred_element_type=F32)


def _mmf(a, b):
    return jnp.dot(a, b, preferred_element_type=F32, precision=_HI)


def _silu(x):
    return x * (1.0 / (1.0 + jnp.exp(-x)))


def _sigmoid(x):
    return 1.0 / (1.0 + jnp.exp(-x))


def _mod_kernel(s_ref, w_ref, b_ref, o_ref):
    o_ref[0] = _mmf(_silu(s_ref[...]), w_ref[0]) + b_ref[0]


def modulation_all(c, c_ctx, ada_w, ada_b):
    b = c.shape[0]
    rows = 8 * ((b + 1 + 7) // 8)
    s = jnp.zeros((rows, D_MODEL), F32).at[:b].set(c).at[b].set(c_ctx)
    tn = 1536
    n = ada_w.shape[-1]
    return pl.pallas_call(
        _mod_kernel,
        grid=(DEPTH, n // tn),
        in_specs=[pl.BlockSpec((rows, D_MODEL), lambda i, j: (0, 0)),
                  pl.BlockSpec((1, D_MODEL, tn), lambda i, j: (i, 0, j)),
                  pl.BlockSpec((1, 1, tn), lambda i, j: (i, 0, j))],
        out_specs=pl.BlockSpec((1, rows, tn), lambda i, j: (i, 0, j)),
        out_shape=jax.ShapeDtypeStruct((DEPTH, rows, n), F32),
        compiler_params=_cparams(("arbitrary", "arbitrary")),
        name="adaln_modulation",
    )(s, ada_w, ada_b.reshape(DEPTH, 1, n))


def _mod_rows(modl_ref, modc_ref, is_ctx, idx):
    sl = slice(idx * D_MODEL, (idx + 1) * D_MODEL)
    return jnp.where(is_ctx, modc_ref[0, :, sl], modl_ref[0, :, sl])


def _linear_kernel(*refs, glob_off, modulate, act, n_main, has_small):
    it = iter(refs)
    x_ref = next(it)
    if modulate is not None:
        modl_ref, modc_ref = next(it), next(it)
    w_ref = next(it)
    ws_ref = next(it) if has_small else None
    o_ref = next(it)
    os_ref = next(it) if has_small else None

    h = x_ref[0]
    if modulate is not None:
        is_ctx = (pl.program_id(1) + glob_off) * TOKEN_TILE < CTX_LEN
        sh = _mod_rows(modl_ref, modc_ref, is_ctx, modulate[0])
        sc = _mod_rows(modl_ref, modc_ref, is_ctx, modulate[1])
        h = h * (1.0 + sc) + sh
    if act == "tanh":
        h = jnp.tanh(h)
    elif act == "sigmoid":
        h = _sigmoid(h)
    hb = h.astype(BF16)
    step = min(N_CHUNK_COLS, n_main)
    for j in range(n_main // step):
        o_ref[0, :, j * step:(j + 1) * step] = jnp.dot(
            hb, w_ref[:, j * step:(j + 1) * step], preferred_element_type=F32)
    if has_small:
        os_ref[0] = _mmf(h, ws_ref[...])


def linear(x, w_bf16, *, mods=None, modulate=None, act=None, w_small=None):
    b, t_out, k = x.shape
    n_main = w_bf16.shape[1]
    has_small = w_small is not None
    in_specs = [pl.BlockSpec((1, TOKEN_TILE, k), lambda bi, ti: (bi, ti, 0))]
    args = [x]
    if modulate is not None:
        modl, modc = mods
        in_specs += [pl.BlockSpec((1, 1, 6 * D_MODEL), lambda bi, ti: (bi, 0, 0)),
                     pl.BlockSpec((1, 1, 6 * D_MODEL), lambda bi, ti: (0, 0, 0))]
        args += [modl, modc]
    in_specs.append(pl.BlockSpec((k, n_main), lambda bi, ti: (0, 0)))
    args.append(w_bf16)
    out_specs = [pl.BlockSpec((1, TOKEN_TILE, n_main), lambda bi, ti: (bi, ti, 0))]
    out_shape = [jax.ShapeDtypeStruct((b, t_out, n_main), F32)]
    if has_small:
        ns = w_small.shape[1]
        in_specs.append(pl.BlockSpec((k, ns), lambda bi, ti: (0, 0)))
        args.append(w_small)
        out_specs.append(pl.BlockSpec((1, TOKEN_TILE, ns), lambda bi, ti: (bi, ti, 0)))
        out_shape.append(jax.ShapeDtypeStruct((b, t_out, ns), F32))
    res = pl.pallas_call(
        functools.partial(_linear_kernel, glob_off=0, modulate=modulate, act=act,
                          n_main=n_main, has_small=has_small),
        grid=(b, t_out // TOKEN_TILE),
        in_specs=in_specs, out_specs=out_specs, out_shape=out_shape,
        compiler_params=_cparams(("parallel", "parallel")),
        name="linear",
    )(*args)
    return res if has_small else res[0]


def _layer_norm_rows(r, g, b):
    mu = jnp.mean(r, axis=-1, keepdims=True)
    rc = r - mu
    var = jnp.mean(rc * rc, axis=-1, keepdims=True)
    return rc * lax.rsqrt(var + LN_EPS) * g + b


def _out_ln_kernel(y_ref, w_ref, x_ref, modl_ref, modc_ref, g_ref, b_ref, o_ref, *, tile_off, gate_idx):
    is_ctx = (pl.program_id(1) + tile_off) * TOKEN_TILE < CTX_LEN
    gate = _mod_rows(modl_ref, modc_ref, is_ctx, gate_idx)
    f = jnp.dot(y_ref[0].astype(BF16), w_ref[...], preferred_element_type=F32)
    r = ALPHA * x_ref[0] + gate * f
    o_ref[0] = _layer_norm_rows(r, g_ref[...], b_ref[...])


def out_proj_ln(y, w_bf16, x, mods, ln_g, ln_b, *, gate_idx, tile_off=0):
    b, t_y, k = y.shape
    modl, modc = mods
    return pl.pallas_call(
        functools.partial(_out_ln_kernel, tile_off=tile_off, gate_idx=gate_idx),
        grid=(b, t_y // TOKEN_TILE),
        in_specs=[pl.BlockSpec((1, TOKEN_TILE, k), lambda bi, ti: (bi, ti, 0)),
                  pl.BlockSpec((k, D_MODEL), lambda bi, ti: (0, 0)),
                  pl.BlockSpec((1, TOKEN_TILE, D_MODEL), lambda bi, ti: (bi, ti + tile_off, 0)),
                  pl.BlockSpec((1, 1, 6 * D_MODEL), lambda bi, ti: (bi, 0, 0)),
                  pl.BlockSpec((1, 1, 6 * D_MODEL), lambda bi, ti: (0, 0, 0)),
                  pl.BlockSpec((1, D_MODEL), lambda bi, ti: (0, 0)),
                  pl.BlockSpec((1, D_MODEL), lambda bi, ti: (0, 0))],
        out_specs=pl.BlockSpec((1, TOKEN_TILE, D_MODEL), lambda bi, ti: (bi, ti, 0)),
        out_shape=jax.ShapeDtypeStruct((b, t_y, D_MODEL), F32),
        compiler_params=_cparams(("parallel", "parallel")),
        name="out_proj_ln",
    )(y, w_bf16, x, modl, modc, ln_g.reshape(1, -1), ln_b.reshape(1, -1))


def _iota2(n, m, axis):
    return lax.broadcasted_iota(jnp.int32, (n, m), axis)


def _row_to_col(row, eye):
    return jnp.sum(jnp.where(eye, row, 0.0), axis=1, keepdims=True)


def _tri_inv(n_mat, eye_f, blockdiag):
    nd = jnp.where(blockdiag, n_mat, 0.0)
    ne = n_mat - nd
    p = eye_f + nd
    n2 = _mmf(nd, nd)
    p = p + _mmf(p, n2)
    n4 = _mmf(n2, n2)
    p = p + _mmf(p, n4)
    n8 = _mmf(n4, n4)
    dinv = p + _mmf(p, n8)
    m = _mmf(dinv, ne)
    m2 = _mmf(m, m)
    q = eye_f + m + m2 + _mmf(m, m2)
    return _mmf(q, dinv)


def _chunk_order(direction, n_ctx, n_tot):
    def order(j):
        if direction == 0:
            return j
        return jnp.where(j < n_ctx, n_ctx - 1 - j, n_tot - 1 - (j - n_ctx))
    return order


def _masks(direction):
    ii = _iota2(CHUNK, CHUNK, 0)
    jj = _iota2(CHUNK, CHUNK, 1)
    if direction == 0:
        incl, strict = jj <= ii, jj < ii
    else:
        incl, strict = jj >= ii, jj > ii
    return ii, jj, incl, strict


def _gdn_kernel(q_ref, k_ref, v_ref, la_ref, be_ref, o_ref, g_s, s_s, *, n_ctx, n_tot):
    ii, jj, _, _ = _masks(0)
    eye = ii == jj
    eye_f = eye.astype(F32)
    blockdiag = (ii // 16) == (jj // 16)
    for d in (0, 1):
        _, _, incl, strict = _masks(d)
        cum = (ii <= jj).astype(F32) if d == 0 else (ii >= jj).astype(F32)
        last = CHUNK - 1 if d == 0 else 0
        g_s[...] = _mmf(la_ref[d, 0, 0], cum)
        s_s[...] = jnp.zeros_like(s_s)
        order = _chunk_order(d, n_ctx, n_tot)

        def body(j, carry, d=d, incl=incl, strict=strict, last=last, order=order):
            n = order(j)
            rows = pl.ds(pl.multiple_of(n * CHUNK, CHUNK), CHUNK)
            qc, kc, vc = q_ref[0, rows, :], k_ref[0, rows, :], v_ref[0, rows, :]
            g_row = g_s[pl.ds(n, 1), :]
            be_row = be_ref[d, 0, 0, pl.ds(n, 1), :]
            g_col = _row_to_col(g_row, eye)
            be_col = _row_to_col(be_row, eye)
            g_last = g_row[:, last:last + 1]
            gamma = jnp.exp(jnp.where(incl, g_col - g_row, NEG))
            kk = _mm_nt(kc, kc)
            n_mat = jnp.where(strict, -(be_col * kk * gamma), 0.0)
            tinv = _tri_inv(n_mat, eye_f, blockdiag)
            eg = jnp.exp(g_col)
            u = _mmf(tinv, be_col * vc)
            w = _mmf(tinv, (be_col * eg) * kc)
            p_mat = jnp.where(incl, _mm_nt(qc, kc) * gamma, 0.0)
            q_dec = qc * eg
            k_dec = kc * jnp.exp(g_last - g_col)
            s = s_s[...]
            delta = u - _mm(w, s)
            o = _mm(q_dec, s) + _mm(p_mat, delta)
            s_s[...] = jnp.exp(g_last) * s + _mm_tn(k_dec, delta)
            if d == 0:
                o_ref[0, rows, :] = o
            else:
                o_ref[0, rows, :] += o
            return carry

        lax.fori_loop(0, n_tot, body, 0)


def gdn_scan(q, k, v, log_alpha, beta):
    b, t, _ = q.shape
    n_tot = t // CHUNK
    ncp = log_alpha.shape[3]
    blk = pl.BlockSpec((1, t, GDN_DK), lambda bi, hi: (bi, 0, hi))
    gate_blk = pl.BlockSpec((2, 1, 1, ncp, CHUNK), lambda bi, hi: (0, bi, hi, 0, 0))
    return pl.pallas_call(
        functools.partial(_gdn_kernel, n_ctx=CTX_LEN // CHUNK, n_tot=n_tot),
        grid=(b, GDN_HEADS),
        in_specs=[blk, blk, blk, gate_blk, gate_blk],
        out_specs=blk,
        out_shape=jax.ShapeDtypeStruct((b, t, GDN_HEADS * GDN_DV), F32),
        scratch_shapes=[pltpu.VMEM((ncp, CHUNK), F32), pltpu.VMEM((GDN_DK, GDN_DV), F32)],
        compiler_params=_cparams(("parallel", "parallel")),
        name="gdn_scan",
    )(q, k, v, log_alpha, beta)


def _mlstm_kernel(q_ref, k_ref, v_ref, ip_ref, lf_ref, o_ref, b_s, c_s, n_s, m_s, *, n_ctx, n_tot):
    ii, jj, _, _ = _masks(0)
    eye = ii == jj
    for d in (0, 1):
        _, _, incl, _ = _masks(d)
        cum = (ii <= jj).astype(F32) if d == 0 else (ii >= jj).astype(F32)
        last = CHUNK - 1 if d == 0 else 0
        b_s[...] = _mmf(lf_ref[d, 0, 0], cum)
        c_s[...] = jnp.zeros_like(c_s)
        n_s[...] = jnp.zeros_like(n_s)
        m_s[...] = jnp.zeros_like(m_s)
        order = _chunk_order(d, n_ctx, n_tot)

        def body(j, carry, d=d, incl=incl, last=last, order=order):
            n = order(j)
            rows = pl.ds(pl.multiple_of(n * CHUNK, CHUNK), CHUNK)
            qc, kc, vc = q_ref[0, rows, :], k_ref[0, rows, :], v_ref[0, rows, :]
            b_row = b_s[pl.ds(n, 1), :]
            ip_row = ip_ref[d, 0, 0, pl.ds(n, 1), :]
            b_col = _row_to_col(b_row, eye)
            b_last = b_row[:, last:last + 1]
            log_d = jnp.where(incl, b_col - b_row + ip_row, NEG)
            m_intra = jnp.max(log_d, axis=1, keepdims=True)
            qk = _mm_nt(qc, kc)
            log_end = b_last - b_row + ip_row
            m_end = jnp.max(log_end, axis=1, keepdims=True)
            c_st, n_st, m_st = c_s[...], n_s[...], m_s[...]
            m_row = jnp.maximum(b_col + m_st, m_intra)
            w_intra = jnp.exp(log_d - m_row) * qk
            w_state = jnp.exp(b_col + m_st - m_row)
            num = w_state * _mm(qc, c_st) + _mm(w_intra, vc)
            den = (w_state * jnp.sum(qc * n_st, axis=1, keepdims=True)
                   + jnp.sum(w_intra, axis=1, keepdims=True))
            h = num / jnp.maximum(jnp.abs(den), jnp.exp(-m_row))
            m_new = jnp.maximum(b_last + m_st, m_end)
            decay = jnp.exp(b_last + m_st - m_new)
            k_w = kc * _row_to_col(jnp.exp(log_end - m_new), eye)
            c_s[...] = decay * c_st + _mm_tn(k_w, vc)
            n_s[...] = decay * n_st + jnp.sum(k_w, axis=0, keepdims=True)
            m_s[...] = m_new
            if d == 0:
                o_ref[0, rows, :] = h
            else:
                o_ref[0, rows, :] += h
            return carry

        lax.fori_loop(0, n_tot, body, 0)


def mlstm_scan(q, k, v, i_pre, log_f):
    b, t, _ = q.shape
    n_tot = t // CHUNK
    ncp = i_pre.shape[3]
    qk_blk = pl.BlockSpec((1, t, MLSTM_DQK), lambda bi, hi: (bi, 0, hi))
    v_blk = pl.BlockSpec((1, t, MLSTM_DV), lambda bi, hi: (bi, 0, hi))
    gate_blk = pl.BlockSpec((2, 1, 1, ncp, CHUNK), lambda bi, hi: (0, bi, hi, 0, 0))
    return pl.pallas_call(
        functools.partial(_mlstm_kernel, n_ctx=CTX_LEN // CHUNK, n_tot=n_tot),
        grid=(b, MLSTM_HEADS),
        in_specs=[qk_blk, qk_blk, v_blk, gate_blk, gate_blk],
        out_specs=v_blk,
        out_shape=jax.ShapeDtypeStruct((b, t, MLSTM_HEADS * MLSTM_DV), F32),
        scratch_shapes=[pltpu.VMEM((ncp, CHUNK), F32), pltpu.VMEM((MLSTM_DQK, MLSTM_DV), F32),
                        pltpu.VMEM((1, MLSTM_DQK), F32), pltpu.VMEM((1, 1), F32)],
        compiler_params=_cparams(("parallel", "parallel")),
        name="mlstm_scan",
    )(q, k, v, i_pre, log_f)


def _rwkv_kernel(r_ref, v_ref, kk_ref, lw_ref, kd_ref, bv_ref, o_ref, s_s, *, n_ctx, n_tot):
    ii, jj, _, _ = _masks(0)
    eye = ii == jj
    eye_f = eye.astype(F32)
    blockdiag = (ii // 16) == (jj // 16)
    hd = RWKV_HEAD
    for d in (0, 1):
        _, _, incl, strict = _masks(d)
        cum = incl.astype(F32)
        last = CHUNK - 1 if d == 0 else 0
        s_s[...] = jnp.zeros_like(s_s)
        order = _chunk_order(d, n_ctx, n_tot)

        def body(j, carry, d=d, incl=incl, strict=strict, cum=cum, last=last, order=order):
            n = order(j)
            rows = pl.ds(pl.multiple_of(n * CHUNK, CHUNK), CHUNK)
            r2, v2, kk2 = r_ref[0, rows, :], v_ref[0, rows, :], kk_ref[0, rows, :]
            lw2, kd2, bv2 = lw_ref[d, 0, rows, :], kd_ref[d, 0, rows, :], bv_ref[d, 0, rows, :]
            gcs2 = _mmf(cum, lw2)
            outs = []
            for hh in range(2):
                sl = slice(hh * hd, (hh + 1) * hd)
                r, v, kk, lw, kd, bv, gcs = (a[:, sl] for a in (r2, v2, kk2, lw2, kd2, bv2, gcs2))
                e_prev = jnp.exp(gcs - lw)
                e_inc = jnp.exp(gcs)
                e_neg = jnp.exp(-gcs)
                a_h = -kk * e_prev
                r_h = r * e_inc
                b_h = bv * e_neg
                k_h = kd * e_neg
                a_ab = jnp.where(strict, _mm_nt(a_h, b_h), 0.0)
                a_ak = jnp.where(strict, _mm_nt(a_h, k_h), 0.0)
                a_rb = jnp.where(incl, _mm_nt(r_h, b_h), 0.0)
                a_rk = jnp.where(incl, _mm_nt(r_h, k_h), 0.0)
                tinv = _tri_inv(a_ab, eye_f, blockdiag)
                st = s_s[hh]
                u = _mmf(tinv, _mm(a_h, st) + _mm(a_ak, v))
                y = _mm(r_h, st) + _mm(a_rb, u) + _mm(a_rk, v)
                gl_row = gcs[last:last + 1, :]
                e_end = jnp.exp(gl_row - gcs)
                gl_col = _row_to_col(jnp.exp(gl_row), eye)
                s_s[hh] = gl_col * st + _mm_tn(bv * e_end, u) + _mm_tn(kd * e_end, v)
                outs.append(y)
            y2 = jnp.concatenate(outs, axis=1)
            if d == 0:
                o_ref[0, rows, :] = y2
            else:
                o_ref[0, rows, :] += y2
            return carry

        lax.fori_loop(0, n_tot, body, 0)


def rwkv_scan(r, v, kk, log_w, k_dir, b_dir):
    b, t, _ = r.shape
    n_tot = t // CHUNK
    blk = pl.BlockSpec((1, t, 128), lambda bi, hi: (bi, 0, hi))
    dblk = pl.BlockSpec((2, 1, t, 128), lambda bi, hi: (0, bi, 0, hi))
    return pl.pallas_call(
        functools.partial(_rwkv_kernel, n_ctx=CTX_LEN // CHUNK, n_tot=n_tot),
        grid=(b, RWKV_HEADS // 2),
        in_specs=[blk, blk, blk, dblk, dblk, dblk],
        out_specs=blk,
        out_shape=jax.ShapeDtypeStruct((b, t, D_MODEL), F32),
        scratch_shapes=[pltpu.VMEM((2, RWKV_HEAD, RWKV_HEAD), F32)],
        compiler_params=_cparams(("parallel", "parallel")),
        name="rwkv_scan",
    )(r, v, kk, log_w, k_dir, b_dir)


def _na_kernel(q_ref, k_ref, v_ref, bias_ref, o_ref, *, rows):
    scale = NA_DH ** -0.5
    slab = NA_WIN_ROWS * GRID_W
    kc2 = k_ref[0, 0:CTX_LEN, :]
    vc2 = v_ref[0, 0:CTX_LEN, :]

    def body(r, carry):
        r0 = jnp.clip(r - NA_WIN_ROWS // 2, 0, rows - NA_WIN_ROWS)
        dr0 = r0 - r + NA_WIN_ROWS - 1
        q2 = q_ref[0, pl.ds(pl.multiple_of(CTX_LEN + r * GRID_W, GRID_W), GRID_W), :]
        krows = pl.ds(pl.multiple_of(CTX_LEN + r0 * GRID_W, GRID_W), slab)
        ks2 = k_ref[0, krows, :]
        vs2 = v_ref[0, krows, :]
        outs = []
        for hh in range(2):
            sl = slice(hh * NA_DH, (hh + 1) * NA_DH)
            qh = q2[:, sl]
            s_lat = _mm_nt(qh, ks2[:, sl]) * scale + bias_ref[hh, pl.ds(dr0, 1)][0]
            s_ctx = _mm_nt(qh, kc2[:, sl]) * scale
            m = jnp.maximum(jnp.max(s_lat, axis=1, keepdims=True), jnp.max(s_ctx, axis=1, keepdims=True))
            p_lat = jnp.exp(s_lat - m)
            p_ctx = jnp.exp(s_ctx - m)
            l = jnp.sum(p_lat, axis=1, keepdims=True) + jnp.sum(p_ctx, axis=1, keepdims=True)
            outs.append((_mm(p_lat, vs2[:, sl]) + _mm(p_ctx, vc2[:, sl])) / l)
        o_ref[0, pl.ds(pl.multiple_of(r * GRID_W, GRID_W), GRID_W), :] = jnp.concatenate(outs, axis=1)
        return carry

    lax.fori_loop(0, rows, body, 0)


def _na_bias_table(rpb):
    cols = np.arange(GRID_W)
    win_c0 = np.clip(cols - NA_WIN_COLS // 2, 0, GRID_W - NA_WIN_COLS)
    kc = np.arange(GRID_W)
    in_win = (kc[None, :] >= win_c0[:, None]) & (kc[None, :] < win_c0[:, None] + NA_WIN_COLS)
    dc = np.clip(kc[None, :] - cols[:, None] + NA_WIN_COLS - 1, 0, 2 * NA_WIN_COLS - 2)
    dr = np.arange(NA_WIN_ROWS)[:, None] + np.arange(NA_WIN_ROWS)[None, :]
    tab = rpb.astype(F32)[:, dr][:, :, :, dc]
    tab = jnp.where(in_win[None, None, None], tab, NEG)
    tab = tab.transpose(0, 1, 3, 2, 4)
    return tab.reshape(NA_HEADS, NA_WIN_ROWS, GRID_W, NA_WIN_ROWS * GRID_W)


def na_attention(z, bias_tab):
    b, t, _ = z.shape
    t_lat = t - CTX_LEN
    n_pairs = NA_HEADS // 2
    return pl.pallas_call(
        functools.partial(_na_kernel, rows=t_lat // GRID_W),
        grid=(n_pairs, b),
        in_specs=[pl.BlockSpec((1, t, 128), lambda hi, bi: (bi, 0, hi)),
                  pl.BlockSpec((1, t, 128), lambda hi, bi: (bi, 0, n_pairs + hi)),
                  pl.BlockSpec((1, t, 128), lambda hi, bi: (bi, 0, 2 * n_pairs + hi)),
                  pl.BlockSpec((2, NA_WIN_ROWS, GRID_W, NA_WIN_ROWS * GRID_W), lambda hi, bi: (hi, 0, 0, 0))],
        out_specs=pl.BlockSpec((1, t_lat, 128), lambda hi, bi: (bi, 0, hi)),
        out_shape=jax.ShapeDtypeStruct((b, t_lat, D_MODEL), F32),
        compiler_params=_cparams(("parallel", "parallel")),
        name="na_attention",
    )(z, z, z, bias_tab)


def _router_kernel(x_ref, modl_ref, modc_ref, rw_ref, rb_ref, hb_ref, idx_ref, wt_ref, *, tile_off):
    is_ctx = (pl.program_id(1) + tile_off) * TOKEN_TILE < CTX_LEN
    sh = _mod_rows(modl_ref, modc_ref, is_ctx, 3)
    sc = _mod_rows(modl_ref, modc_ref, is_ctx, 4)
    h = x_ref[0] * (1.0 + sc) + sh
    hb_ref[0] = h.astype(BF16)
    logits = _mmf(h, rw_ref[...])
    mx = jnp.max(logits, axis=1, keepdims=True)
    ex = jnp.exp(logits - mx)
    probs = ex / jnp.sum(ex, axis=1, keepdims=True)
    sel = probs + rb_ref[...]
    tm = sel.shape[0]
    e_id = _iota2(tm, N_EXPERTS, 1)
    per_group = N_EXPERTS // N_GROUPS
    g_id = e_id // per_group

    def top1(vals):
        m1 = jnp.max(vals, axis=1, keepdims=True)
        i1 = jnp.min(jnp.where(vals == m1, e_id, N_EXPERTS), axis=1, keepdims=True)
        return m1, i1

    best_score, best = None, None
    for g in range(N_GROUPS):
        vals = jnp.where(g_id == g, sel, NEG)
        m1, i1 = top1(vals)
        m2, _ = top1(jnp.where(e_id == i1, NEG, vals))
        score = m1 + m2
        if g == 0:
            best_score, best = score, jnp.zeros_like(i1)
        else:
            better = score > best_score
            best = jnp.where(better, g, best)
            best_score = jnp.where(better, score, best_score)
    vals = jnp.where(g_id == best, sel, NEG)
    _, i1 = top1(vals)
    _, i2 = top1(jnp.where(e_id == i1, NEG, vals))
    p1 = jnp.sum(jnp.where(e_id == i1, probs, 0.0), axis=1, keepdims=True)
    p2 = jnp.sum(jnp.where(e_id == i2, probs, 0.0), axis=1, keepdims=True)
    tot = p1 + p2
    slot = _iota2(tm, 2, 1)
    idx_ref[0] = jnp.where(slot == 0, i1, i2)
    wt_ref[0] = jnp.where(slot == 0, p1 / tot, p2 / tot)


def moe_route(x, mods, router_w, router_b, *, tile_off):
    b, t_out, _ = x.shape
    modl, modc = mods
    return pl.pallas_call(
        functools.partial(_router_kernel, tile_off=tile_off),
        grid=(b, t_out // TOKEN_TILE),
        in_specs=[pl.BlockSpec((1, TOKEN_TILE, D_MODEL), lambda bi, ti: (bi, ti, 0)),
                  pl.BlockSpec((1, 1, 6 * D_MODEL), lambda bi, ti: (bi, 0, 0)),
                  pl.BlockSpec((1, 1, 6 * D_MODEL), lambda bi, ti: (0, 0, 0)),
                  pl.BlockSpec((D_MODEL, N_EXPERTS), lambda bi, ti: (0, 0)),
                  pl.BlockSpec((1, N_EXPERTS), lambda bi, ti: (0, 0))],
        out_specs=[pl.BlockSpec((1, TOKEN_TILE, D_MODEL), lambda bi, ti: (bi, ti, 0)),
                   pl.BlockSpec((1, TOKEN_TILE, 2), lambda bi, ti: (bi, ti, 0)),
                   pl.BlockSpec((1, TOKEN_TILE, 2), lambda bi, ti: (bi, ti, 0))],
        out_shape=[jax.ShapeDtypeStruct((b, t_out, D_MODEL), BF16),
                   jax.ShapeDtypeStruct((b, t_out, 2), jnp.int32),
                   jax.ShapeDtypeStruct((b, t_out, 2), F32)],
        compiler_params=_cparams(("parallel", "parallel")),
        name="moe_route",
    )(x, modl, modc, router_w, router_b.reshape(1, -1))


def _ffn_kernel(te_ref, nt_ref, x_ref, w1_ref, w3_ref, w2_ref, o_ref):
    @pl.when(pl.program_id(0) < nt_ref[0])
    def _():
        xb = x_ref[...]
        h1 = jnp.dot(xb, w1_ref[0], preferred_element_type=F32)
        h3 = jnp.dot(xb, w3_ref[0], preferred_element_type=F32)
        hid = (_silu(h1) * h3).astype(BF16)
        o_ref[...] = jnp.dot(hid, w2_ref[0], preferred_element_type=F32)

    @pl.when(pl.program_id(0) >= nt_ref[0])
    def _():
        o_ref[...] = jnp.zeros_like(o_ref)


def expert_ffn(xs, tile_expert, n_tiles_used, w1, w3, w2):
    p = xs.shape[0]
    grid_spec = pltpu.PrefetchScalarGridSpec(
        num_scalar_prefetch=2,
        grid=(p // FFN_TILE,),
        in_specs=[pl.BlockSpec((FFN_TILE, D_MODEL), lambda i, te, nt: (i, 0)),
                  pl.BlockSpec((1, D_MODEL, D_EXPERT), lambda i, te, nt: (te[i], 0, 0)),
                  pl.BlockSpec((1, D_MODEL, D_EXPERT), lambda i, te, nt: (te[i], 0, 0)),
                  pl.BlockSpec((1, D_EXPERT, D_MODEL), lambda i, te, nt: (te[i], 0, 0))],
        out_specs=pl.BlockSpec((FFN_TILE, D_MODEL), lambda i, te, nt: (i, 0)),
    )
    return pl.pallas_call(
        _ffn_kernel,
        grid_spec=grid_spec,
        out_shape=jax.ShapeDtypeStruct((p, D_MODEL), F32),
        compiler_params=_cparams(("arbitrary",)),
        name="expert_ffn",
    )(tile_expert, n_tiles_used, xs, w1, w3, w2)


def _combine_ln_kernel(x_ref, y0_ref, y1_ref, wt_ref, modl_ref, modc_ref, g_ref, b_ref, o_ref, *, tile_off):
    is_ctx = (pl.program_id(1) + tile_off) * TOKEN_TILE < CTX_LEN
    gate = _mod_rows(modl_ref, modc_ref, is_ctx, 5)
    wt = wt_ref[0]
    f = wt[:, 0:1] * y0_ref[0] + wt[:, 1:2] * y1_ref[0]
    r = ALPHA * x_ref[0] + gate * f
    o_ref[0] = _layer_norm_rows(r, g_ref[...], b_ref[...])


def combine_ln(x, y0, y1, wt, mods, ln_g, ln_b, *, tile_off):
    b, t_out, _ = y0.shape
    modl, modc = mods
    tok = lambda bi, ti: (bi, ti, 0)
    return pl.pallas_call(
        functools.partial(_combine_ln_kernel, tile_off=tile_off),
        grid=(b, t_out // TOKEN_TILE),
        in_specs=[pl.BlockSpec((1, TOKEN_TILE, D_MODEL), tok),
                  pl.BlockSpec((1, TOKEN_TILE, D_MODEL), tok),
                  pl.BlockSpec((1, TOKEN_TILE, D_MODEL), tok),
                  pl.BlockSpec((1, TOKEN_TILE, 2), tok),
                  pl.BlockSpec((1, 1, 6 * D_MODEL), lambda bi, ti: (bi, 0, 0)),
                  pl.BlockSpec((1, 1, 6 * D_MODEL), lambda bi, ti: (0, 0, 0)),
                  pl.BlockSpec((1, D_MODEL), lambda bi, ti: (0, 0)),
                  pl.BlockSpec((1, D_MODEL), lambda bi, ti: (0, 0))],
        out_specs=pl.BlockSpec((1, TOKEN_TILE, D_MODEL), tok),
        out_shape=jax.ShapeDtypeStruct((b, t_out, D_MODEL), F32),
        compiler_params=_cparams(("parallel", "parallel")),
        name="moe_combine_ln",
    )(x, y0, y1, wt, modl, modc, ln_g.reshape(1, -1), ln_b.reshape(1, -1))


def moe_layer(x, mods, router_w, router_b, w1, w3, w2, ln_g, ln_b, *, tile_off):
    hb, idx, wt = moe_route(x, mods, router_w, router_b, tile_off=tile_off)
    b, t, _ = hb.shape
    n_tok = b * t
    n_pair = 2 * n_tok
    e_flat = idx.reshape(n_pair)
    onehot = (e_flat[:, None] == jnp.arange(N_EXPERTS)[None, :]).astype(jnp.int32)
    csum = jnp.cumsum(onehot, axis=0)
    counts = csum[-1]
    rank = jnp.sum((csum - onehot) * onehot, axis=1)
    padded = ((counts + FFN_TILE - 1) // FFN_TILE) * FFN_TILE
    ends = jnp.cumsum(padded)
    offs = ends - padded
    pos = offs[e_flat] + rank
    n_rows = n_pair + N_EXPERTS * FFN_TILE
    n_tiles = n_rows // FFN_TILE
    src = jnp.zeros((n_rows,), jnp.int32).at[pos].set(jnp.arange(n_pair, dtype=jnp.int32) // 2)
    tile_start = jnp.arange(n_tiles, dtype=jnp.int32) * FFN_TILE
    tile_expert = jnp.minimum(jnp.searchsorted(ends, tile_start, side="right"), N_EXPERTS - 1).astype(jnp.int32)
    n_used = (ends[-1] // FFN_TILE).astype(jnp.int32).reshape(1)
    xs = jnp.take(hb.reshape(n_tok, D_MODEL), src, axis=0)
    ys = expert_ffn(xs, tile_expert, n_used, w1, w3, w2)
    pos2 = pos.reshape(n_tok, 2)
    y0 = jnp.take(ys, pos2[:, 0], axis=0).reshape(b, t, D_MODEL)
    y1 = jnp.take(ys, pos2[:, 1], axis=0).reshape(b, t, D_MODEL)
    return combine_ln(x, y0, y1, wt, mods, ln_g, ln_b, tile_off=tile_off)


def _seg_apply(fn, a):
    return jnp.concatenate([fn(a[:, :CTX_LEN]), fn(a[:, CTX_LEN:])], axis=1)


def _rope_tables(t_lat, dh):
    quarter = dh // 4
    pos = jnp.arange(t_lat)
    inv_freq = ROPE_BASE ** (-jnp.arange(quarter, dtype=F32) / quarter)
    ang_r = (pos // GRID_W).astype(F32)[:, None] * inv_freq[None, :]
    ang_c = (pos % GRID_W).astype(F32)[:, None] * inv_freq[None, :]
    return jnp.cos(ang_r), jnp.sin(ang_r), jnp.cos(ang_c), jnp.sin(ang_c)


def _rope_lat(a, n_heads):
    b, t, hd = a.shape
    dh = hd // n_heads
    q4 = dh // 4
    lat = a[:, CTX_LEN:].reshape(b, t - CTX_LEN, n_heads, dh)
    cr, sr, cc, sc = (u[None, :, None, :] for u in _rope_tables(t - CTX_LEN, dh))
    x1, x2, x3, x4 = lat[..., :q4], lat[..., q4:2 * q4], lat[..., 2 * q4:3 * q4], lat[..., 3 * q4:]
    out = jnp.concatenate([x1 * cr - x2 * sr, x1 * sr + x2 * cr, x3 * cc - x4 * sc, x3 * sc + x4 * cc], axis=-1)
    return jnp.concatenate([a[:, :CTX_LEN], out.reshape(b, t - CTX_LEN, hd)], axis=1)


def _head_l2norm(a, n_heads, eps=1e-6):
    b, t, hd = a.shape
    ah = a.reshape(b, t, n_heads, hd // n_heads)
    return (ah * lax.rsqrt(jnp.sum(ah * ah, axis=-1, keepdims=True) + eps)).reshape(b, t, hd)


def _gate_rows(g):
    b, t = g.shape[:2]
    n = t // CHUNK
    ncp = 8 * ((n + 7) // 8)
    g = g.transpose(2, 3, 0, 4, 1).reshape(2, 2, b, g.shape[-1], n, CHUNK)
    return jnp.pad(g, ((0, 0),) * 4 + ((0, ncp - n), (0, 0)))


def _dwconv3(a, w):
    ap = jnp.pad(a, ((0, 0), (1, 1), (0, 0)))
    return ap[:, :-2] * w[0] + ap[:, 1:-1] * w[1] + ap[:, 2:] * w[2]


def gdn_layer(x, mods, w_in, conv_w, a_log, dt_bias, norm_g, w_out):
    wq = GDN_HEADS * GDN_DK
    n_main = 4 * wq
    z, ab = linear(x, w_in[:, :n_main].astype(BF16), mods=mods, modulate=(0, 1), w_small=w_in[:, n_main:])
    qkv = jax.nn.silu(_seg_apply(lambda s: _dwconv3(s, conv_w), z[..., :3 * wq]))
    q = _rope_lat(_head_l2norm(qkv[..., :wq], GDN_HEADS), GDN_HEADS) * GDN_DK ** -0.5
    k = _rope_lat(_head_l2norm(qkv[..., wq:2 * wq], GDN_HEADS), GDN_HEADS)
    v = qkv[..., 2 * wq:]
    gate = z[..., 3 * wq:]
    b, t, _ = x.shape
    ab = ab.reshape(b, t, 2, 2, GDN_HEADS)
    ab = ab.at[:, :, :, 0].add(dt_bias[None, None])
    rows = _gate_rows(ab)
    log_alpha = -jnp.exp(a_log)[:, None, :, None, None] * jax.nn.softplus(rows[:, 0])
    beta = jax.nn.sigmoid(rows[:, 1])
    valid = (jnp.arange(rows.shape[4]) < t // CHUNK)[:, None]
    log_alpha = jnp.where(valid, log_alpha, 0.0)
    o = gdn_scan(q, k, v, log_alpha, beta)
    oh = o.reshape(b, t, GDN_HEADS, GDN_DV)
    oh = oh * lax.rsqrt(jnp.mean(oh * oh, axis=-1, keepdims=True) + 1e-6) * norm_g
    y = oh.reshape(b, t, -1) * jax.nn.silu(gate)
    return y, w_out


def mlstm_layer(x, mods, w_in, gate_b, norm_g, w_out):
    wq = MLSTM_HEADS * MLSTM_DQK
    wv = MLSTM_HEADS * MLSTM_DV
    n_main = 2 * wq + 2 * wv
    z, gt = linear(x, w_in[:, :n_main].astype(BF16), mods=mods, modulate=(0, 1), w_small=w_in[:, n_main:])
    q = _rope_lat(z[..., :wq], MLSTM_HEADS) * MLSTM_DQK ** -0.5
    k = _rope_lat(z[..., wq:2 * wq], MLSTM_HEADS)
    v = z[..., 2 * wq:2 * wq + wv]
    o_gate = z[..., 2 * wq + wv:]
    b, t, _ = x.shape
    gt = gt.reshape(b, t, 2, 2, MLSTM_HEADS) + gate_b[None, None]
    rows = _gate_rows(gt)
    valid = (jnp.arange(rows.shape[4]) < t // CHUNK)[:, None]
    i_pre = rows[:, 0]
    log_f = jnp.where(valid, jax.nn.log_sigmoid(rows[:, 1]), 0.0)
    h = mlstm_scan(q, k, v, i_pre, log_f)
    hh = h.reshape(b, t, MLSTM_HEADS, MLSTM_DV)
    mu = jnp.mean(hh, axis=-1, keepdims=True)
    var = jnp.mean(jnp.square(hh - mu), axis=-1, keepdims=True)
    hn = ((hh - mu) * lax.rsqrt(var + 1e-6)).reshape(b, t, -1)
    y = hn * norm_g * jax.nn.sigmoid(o_gate)
    return y, w_out


def _modulated(x, mods, sh_idx, sc_idx):
    modl, modc = mods
    d = D_MODEL
    is_ctx = (jnp.arange(x.shape[1]) < CTX_LEN)[None, :, None]
    sh_c, sh_l = modc[:, :, sh_idx * d:(sh_idx + 1) * d], modl[:, :, sh_idx * d:(sh_idx + 1) * d]
    sc_c, sc_l = modc[:, :, sc_idx * d:(sc_idx + 1) * d], modl[:, :, sc_idx * d:(sc_idx + 1) * d]
    return x * (1.0 + jnp.where(is_ctx, sc_c, sc_l)) + jnp.where(is_ctx, sh_c, sh_l)


def _token_shift(a):
    ap = jnp.pad(a, ((0, 0), (1, 1), (0, 0)))
    return 0.5 * (ap[:, :-2] + ap[:, 2:])


def rwkv_layer(x, mods, mu, w_rkv, w0, w1, w2, a0, a1, a2, g1, g2, k_k, k_a, r_k, lnx_g, lnx_b, w_out):
    b, t, d = x.shape
    nh, hd = RWKV_HEADS, RWKV_HEAD
    h = _modulated(x, mods, 0, 1)
    dx = _seg_apply(_token_shift, h) - h
    xr, xw, xk, xv, xa, xg = [h + dx * mu[j] for j in range(6)]
    r = linear(xr, w_rkv[0].astype(BF16))
    k = linear(xk, w_rkv[1].astype(BF16))
    v = linear(xv, w_rkv[2].astype(BF16))
    g = linear(linear(xg, g1.astype(BF16)), g2.astype(BF16), act="sigmoid")
    heads = lambda a: a.reshape(b, t, nh, hd)
    kk = _head_l2norm(k * k_k, nh)
    lws, kds, bvs = [], [], []
    for dr in range(2):
        lo = linear(linear(xw, w1[dr].astype(BF16)), w2[dr].astype(BF16), act="tanh")
        w_raw = -jax.nn.softplus(-(w0[dr] + lo)) - 0.5
        a = jax.nn.sigmoid(a0[dr] + linear(linear(xa, a1[dr].astype(BF16)), a2[dr].astype(BF16)))
        lws.append(-jnp.exp(w_raw))
        kds.append(k * (1.0 + (a - 1.0) * k_a))
        bvs.append(kk * a)
    y = rwkv_scan(r, v, kk, jnp.stack(lws), jnp.stack(kds), jnp.stack(bvs))
    yh = heads(y)
    m = jnp.mean(yh, axis=-1, keepdims=True)
    var = jnp.mean(jnp.square(yh - m), axis=-1, keepdims=True)
    yn = ((yh - m) * lax.rsqrt(var + RWKV_GN_EPS)).reshape(b, t, d)
    bonus = sum(jnp.sum(heads(r) * heads(kd) * r_k, axis=-1, keepdims=True) * heads(v) for kd in kds)
    yo = (yn * lnx_g + lnx_b + bonus.reshape(b, t, d)) * g
    return yo, w_out


def kernel(x, c, ctx, c_ctx, ada_w, ada_b, ln_g, ln_b, router_w, router_b, moe_w1, moe_w3, moe_w2, gdn_w_in, gdn_conv, gdn_a_log, gdn_dt_bias, gdn_norm_g, gdn_w_out, mlstm_w_in, mlstm_gate_b, mlstm_norm_g, mlstm_w_out, rwkv_mu, rwkv_w_rkv, rwkv_w0, rwkv_w1, rwkv_w2, rwkv_a0, rwkv_a1, rwkv_a2, rwkv_g1, rwkv_g2, rwkv_k_k, rwkv_k_a, rwkv_r_k, rwkv_lnx_g, rwkv_lnx_b, rwkv_w_out, na_w_in, na_rpb, na_w_out):
    b = x.shape[0]
    ctx_tiles = CTX_LEN // TOKEN_TILE
    mod_all = modulation_all(c, c_ctx, ada_w, ada_b)
    xs = jnp.concatenate([ctx, x], axis=1)
    assert DEPTH == 4
    for i in range(DEPTH):
        mods = (mod_all[i, :b, None, :], mod_all[i, b:b + 1, None, :])
        if i % 4 == 0:
            y, w_out = gdn_layer(xs, mods, gdn_w_in, gdn_conv, gdn_a_log, gdn_dt_bias, gdn_norm_g, gdn_w_out)
        elif i % 4 == 1:
            y, w_out = mlstm_layer(xs, mods, mlstm_w_in, mlstm_gate_b, mlstm_norm_g, mlstm_w_out)
        elif i % 4 == 2:
            y, w_out = rwkv_layer(xs, mods, rwkv_mu, rwkv_w_rkv, rwkv_w0, rwkv_w1, rwkv_w2, rwkv_a0, rwkv_a1,
                                  rwkv_a2, rwkv_g1, rwkv_g2, rwkv_k_k, rwkv_k_a, rwkv_r_k, rwkv_lnx_g,
                                  rwkv_lnx_b, rwkv_w_out)
        else:
            z = linear(xs, na_w_in.astype(BF16), mods=mods, modulate=(0, 1))
            y, w_out = na_attention(z, _na_bias_table(na_rpb)), na_w_out
        off = ctx_tiles if (i % 4 == 3) else 0
        xs1 = out_proj_ln(y, w_out.astype(BF16), xs, mods, ln_g[i, 0], ln_b[i, 0], gate_idx=2, tile_off=off)
        xs = moe_layer(xs1, mods, router_w, router_b, moe_w1[i].astype(BF16), moe_w3[i].astype(BF16),
                       moe_w2[i].astype(BF16), ln_g[i, 1], ln_b[i, 1], tile_off=off)
    return xs
```

```python
import functools
import math

import numpy as np
import jax
import jax.numpy as jnp
from jax import lax
from jax.experimental import pallas as pl
from jax.experimental.pallas import tpu as pltpu

F32 = jnp.float32
BF16 = jnp.bfloat16

D_MODEL = 1024
DEPTH = 4
GRID_W = 64
CTX_LEN = 256
ALPHA = (2 * DEPTH) ** 0.25
LN_EPS = 1e-5
ROPE_BASE = 10000.0

GDN_HEADS = 8
GDN_DK = 128
GDN_DV = 128
MLSTM_HEADS = 4
MLSTM_DQK = 128
MLSTM_DV = 256
RWKV_HEAD = 64
RWKV_HEADS = 16
RWKV_GN_EPS = 64e-5
NA_HEADS = 16
NA_DH = 64
NA_WIN_ROWS = 8
NA_WIN_COLS = 16
N_EXPERTS = 16
N_GROUPS = 4
D_EXPERT = 512

CHUNK = 64
GDN_UNROLL = 4
RWKV_UNROLL = 2
TOKEN_TILE = 256
N_CHUNK_COLS = 512
FFN_TILE = 512
NEG = -1e30
VMEM_LIMIT = 56 * 1024 * 1024

_HI = lax.Precision.HIGHEST


def _cparams(sem):
    return pltpu.CompilerParams(dimension_semantics=sem, vmem_limit_bytes=VMEM_LIMIT)


def _mm(a, b):
    return jnp.dot(a.astype(BF16), b.astype(BF16), preferred_element_type=F32)


def _mm_nt(a, b):
    return lax.dot_general(a.astype(BF16), b.astype(BF16), (((1,), (1,)), ((), ())),
                           preferred_element_type=F32)


def _mm_tn(a, b):
    return lax.dot_general(a.astype(BF16), b.astype(BF16), (((0,), (0,)), ((), ())),
                           preferred_element_type=F32)


def _mmf(a, b):
    return jnp.dot(a, b, preferred_element_type=F32, precision=_HI)


def _silu(x):
    return x * (1.0 / (1.0 + jnp.exp(-x)))


def _sigmoid(x):
    return 1.0 / (1.0 + jnp.exp(-x))


def _mod_kernel(s_ref, w_ref, b_ref, o_ref):
    o_ref[0] = _mmf(_silu(s_ref[...]), w_ref[0]) + b_ref[0]


def modulation_all(c, c_ctx, ada_w, ada_b):
    b = c.shape[0]
    rows = 8 * ((b + 1 + 7) // 8)
    s = jnp.zeros((rows, D_MODEL), F32).at[:b].set(c).at[b].set(c_ctx)
    tn = 1536
    n = ada_w.shape[-1]
    return pl.pallas_call(
        _mod_kernel,
        grid=(DEPTH, n // tn),
        in_specs=[pl.BlockSpec((rows, D_MODEL), lambda i, j: (0, 0)),
                  pl.BlockSpec((1, D_MODEL, tn), lambda i, j: (i, 0, j)),
                  pl.BlockSpec((1, 1, tn), lambda i, j: (i, 0, j))],
        out_specs=pl.BlockSpec((1, rows, tn), lambda i, j: (i, 0, j)),
        out_shape=jax.ShapeDtypeStruct((DEPTH, rows, n), F32),
        compiler_params=_cparams(("arbitrary", "arbitrary")),
        name="adaln_modulation",
    )(s, ada_w, ada_b.reshape(DEPTH, 1, n))


def _mod_rows(modl_ref, modc_ref, is_ctx, idx):
    sl = slice(idx * D_MODEL, (idx + 1) * D_MODEL)
    return jnp.where(is_ctx, modc_ref[0, :, sl], modl_ref[0, :, sl])


def _linear_kernel(*refs, glob_off, modulate, act, n_main, has_small):
    it = iter(refs)
    x_ref = next(it)
    if modulate is not None:
        modl_ref, modc_ref = next(it), next(it)
    w_ref = next(it)
    ws_ref = next(it) if has_small else None
    o_ref = next(it)
    os_ref = next(it) if has_small else None

    h = x_ref[0]
    if modulate is not None:
        is_ctx = (pl.program_id(1) + glob_off) * TOKEN_TILE < CTX_LEN
        sh = _mod_rows(modl_ref, modc_ref, is_ctx, modulate[0])
        sc = _mod_rows(modl_ref, modc_ref, is_ctx, modulate[1])
        h = h * (1.0 + sc) + sh
    if act == "tanh":
        h = jnp.tanh(h)
    elif act == "sigmoid":
        h = _sigmoid(h)
    hb = h.astype(BF16)
    step = min(N_CHUNK_COLS, n_main)
    for j in range(n_main // step):
        o_ref[0, :, j * step:(j + 1) * step] = jnp.dot(
            hb, w_ref[:, j * step:(j + 1) * step], preferred_element_type=F32)
    if has_small:
        os_ref[0] = _mmf(h, ws_ref[...])


def linear(x, w_bf16, *, mods=None, modulate=None, act=None, w_small=None):
    b, t_out, k = x.shape
    n_main = w_bf16.shape[1]
    has_small = w_small is not None
    in_specs = [pl.BlockSpec((1, TOKEN_TILE, k), lambda bi, ti: (bi, ti, 0))]
    args = [x]
    if modulate is not None:
        modl, modc = mods
        in_specs += [pl.BlockSpec((1, 1, 6 * D_MODEL), lambda bi, ti: (bi, 0, 0)),
                     pl.BlockSpec((1, 1, 6 * D_MODEL), lambda bi, ti: (0, 0, 0))]
        args += [modl, modc]
    in_specs.append(pl.BlockSpec((k, n_main), lambda bi, ti: (0, 0)))
    args.append(w_bf16)
    out_specs = [pl.BlockSpec((1, TOKEN_TILE, n_main), lambda bi, ti: (bi, ti, 0))]
    out_shape = [jax.ShapeDtypeStruct((b, t_out, n_main), F32)]
    if has_small:
        ns = w_small.shape[1]
        in_specs.append(pl.BlockSpec((k, ns), lambda bi, ti: (0, 0)))
        args.append(w_small)
        out_specs.append(pl.BlockSpec((1, TOKEN_TILE, ns), lambda bi, ti: (bi, ti, 0)))
        out_shape.append(jax.ShapeDtypeStruct((b, t_out, ns), F32))
    res = pl.pallas_call(
        functools.partial(_linear_kernel, glob_off=0, modulate=modulate, act=act,
                          n_main=n_main, has_small=has_small),
        grid=(b, t_out // TOKEN_TILE),
        in_specs=in_specs, out_specs=out_specs, out_shape=out_shape,
        compiler_params=_cparams(("parallel", "parallel")),
        name="linear",
    )(*args)
    return res if has_small else res[0]


def _layer_norm_rows(r, g, b):
    mu = jnp.mean(r, axis=-1, keepdims=True)
    rc = r - mu
    var = jnp.mean(rc * rc, axis=-1, keepdims=True)
    return rc * lax.rsqrt(var + LN_EPS) * g + b


def _out_ln_kernel(y_ref, w_ref, x_ref, modl_ref, modc_ref, g_ref, b_ref, o_ref, *, tile_off, gate_idx):
    is_ctx = (pl.program_id(1) + tile_off) * TOKEN_TILE < CTX_LEN
    gate = _mod_rows(modl_ref, modc_ref, is_ctx, gate_idx)
    f = jnp.dot(y_ref[0].astype(BF16), w_ref[...], preferred_element_type=F32)
    r = ALPHA * x_ref[0] + gate * f
    o_ref[0] = _layer_norm_rows(r, g_ref[...], b_ref[...])


def out_proj_ln(y, w_bf16, x, mods, ln_g, ln_b, *, gate_idx, tile_off=0):
    b, t_y, k = y.shape
    modl, modc = mods
    return pl.pallas_call(
        functools.partial(_out_ln_kernel, tile_off=tile_off, gate_idx=gate_idx),
        grid=(b, t_y // TOKEN_TILE),
        in_specs=[pl.BlockSpec((1, TOKEN_TILE, k), lambda bi, ti: (bi, ti, 0)),
                  pl.BlockSpec((k, D_MODEL), lambda bi, ti: (0, 0)),
                  pl.BlockSpec((1, TOKEN_TILE, D_MODEL), lambda bi, ti: (bi, ti + tile_off, 0)),
                  pl.BlockSpec((1, 1, 6 * D_MODEL), lambda bi, ti: (bi, 0, 0)),
                  pl.BlockSpec((1, 1, 6 * D_MODEL), lambda bi, ti: (0, 0, 0)),
                  pl.BlockSpec((1, D_MODEL), lambda bi, ti: (0, 0)),
                  pl.BlockSpec((1, D_MODEL), lambda bi, ti: (0, 0))],
        out_specs=pl.BlockSpec((1, TOKEN_TILE, D_MODEL), lambda bi, ti: (bi, ti, 0)),
        out_shape=jax.ShapeDtypeStruct((b, t_y, D_MODEL), F32),
        compiler_params=_cparams(("parallel", "parallel")),
        name="out_proj_ln",
    )(y, w_bf16, x, modl, modc, ln_g.reshape(1, -1), ln_b.reshape(1, -1))


def _iota2(n, m, axis):
    return lax.broadcasted_iota(jnp.int32, (n, m), axis)


def _row_to_col(row, eye):
    return jnp.sum(jnp.where(eye, row, 0.0), axis=1, keepdims=True)


def _tri_solve_steps(n_mat, rhs_list, eye_f, blockdiag):
    nd = jnp.where(blockdiag, n_mat, 0.0)
    ne = n_mat - nd
    p = eye_f + nd
    n2 = _mm(nd, nd)
    yield
    p = p + _mm(p, n2)
    n4 = _mm(n2, n2)
    yield
    p = p + _mm(p, n4)
    n8 = _mm(n4, n4)
    yield
    dinv = p + _mm(p, n8)
    yield
    m = _mm(dinv, ne)
    xs = [_mm(dinv, r) for r in rhs_list]
    yield
    m2 = _mm(m, m)
    xs = [x + _mm(m, x) for x in xs]
    yield
    xs = [x + _mm(m2, x) for x in xs]
    yield
    return xs


def _run_interleaved(gens):
    gens = list(gens)
    while gens:
        alive = []
        for g in gens:
            try:
                next(g)
                alive.append(g)
            except StopIteration:
                pass
        gens = alive


def _split2(x):
    hi = x.astype(BF16)
    return hi, (x - hi.astype(F32)).astype(BF16)


def _cumsum_rows(x, cum):
    hi, lo = _split2(x)
    cb = cum.astype(BF16)
    return jnp.dot(hi, cb, preferred_element_type=F32) + jnp.dot(lo, cb, preferred_element_type=F32)


def _cumsum_cols(cum, x):
    hi, lo = _split2(x)
    cb = cum.astype(BF16)
    return jnp.dot(cb, hi, preferred_element_type=F32) + jnp.dot(cb, lo, preferred_element_type=F32)


def _chunk_order(direction, n_ctx, n_tot):
    def order(j):
        if direction == 0:
            return j
        return jnp.where(j < n_ctx, n_ctx - 1 - j, n_tot - 1 - (j - n_ctx))
    return order


def _masks(direction):
    ii = _iota2(CHUNK, CHUNK, 0)
    jj = _iota2(CHUNK, CHUNK, 1)
    if direction == 0:
        incl, strict = jj <= ii, jj < ii
    else:
        incl, strict = jj >= ii, jj > ii
    return ii, jj, incl, strict


def _gdn_kernel(q_ref, k_ref, v_ref, la_ref, be_ref, o_ref, g_s, lhs_s, add_s, s_s, ob_s, *, n_ctx, n_tot):
    ii, jj, _, _ = _masks(0)
    eye = ii == jj
    eye_f = eye.astype(F32)
    blockdiag = (ii // 16) == (jj // 16)
    dirs = []
    for d in (0, 1):
        _, _, incl, strict = _masks(d)
        cum = (ii <= jj).astype(F32) if d == 0 else (ii >= jj).astype(F32)
        g_s[d] = _cumsum_rows(la_ref[d, 0, 0], cum)
        dirs.append((incl, strict, CHUNK - 1 if d == 0 else 0))

    def prep(j, carry):
        loaded = []
        for uu in range(GDN_UNROLL):
            n = j * GDN_UNROLL + uu
            rows = pl.ds(pl.multiple_of(n * CHUNK, CHUNK), CHUNK)
            loaded.append((n, q_ref[0, rows, :], k_ref[0, rows, :], v_ref[0, rows, :],
                           [g_s[d, pl.ds(n, 1), :] for d in (0, 1)],
                           [be_ref[d, 0, 0, pl.ds(n, 1), :] for d in (0, 1)]))
        stores = []
        grams = [(_mm_nt(kc, kc), _mm_nt(qc, kc)) for _, qc, kc, _, _, _ in loaded]

        def chain(n, qc, kc, vc, g_row, be_row, kk, qk, d, incl, strict, last):
            g_col = _row_to_col(g_row, eye)
            be_col = _row_to_col(be_row, eye)
            g_last = g_row[:, last:last + 1]
            gamma = jnp.exp(jnp.where(incl, g_col - g_row, NEG))
            n_mat = jnp.where(strict, -(be_col * kk * gamma), 0.0)
            eg = jnp.exp(g_col)
            rhs = jnp.concatenate([be_col * vc, (be_col * eg) * kc], axis=1)
            (uw,) = yield from _tri_solve_steps(n_mat, [rhs], eye_f, blockdiag)
            p_mat = jnp.where(incl, qk * gamma, 0.0)
            p_uw = _mm(p_mat, uw)
            k_dec = kc * jnp.exp(g_last - g_col)
            k_uw = _mm_tn(k_dec, uw)
            q_t = qc * eg - p_uw[:, GDN_DV:]
            stores.append((d, n, jnp.concatenate([q_t, -k_uw[:, GDN_DV:]], axis=0).astype(BF16),
                           jnp.concatenate([p_uw[:, :GDN_DV], k_uw[:, :GDN_DV]], axis=0)))

        _run_interleaved(
            chain(n, qc, kc, vc, g_rows[d], be_rows[d], kk, qk, d, *dirs[d])
            for (n, qc, kc, vc, g_rows, be_rows), (kk, qk) in zip(loaded, grams) for d in (0, 1))
        for d, n, lhs, add in stores:
            lhs_s[d, n] = lhs
            add_s[d, n] = add
        return carry

    lax.fori_loop(0, n_tot // GDN_UNROLL, prep, 0)

    s_s[...] = jnp.zeros_like(s_s)
    orders = [_chunk_order(d, n_ctx, n_tot) for d in (0, 1)]
    o_refs = (o_ref.at[0], ob_s)

    def step(j, carry):
        loaded = []
        for d, (_, _, last) in enumerate(dirs):
            n = orders[d](j)
            loaded.append((n, s_s[d], lhs_s[d, n], add_s[d, n], g_s[d, pl.ds(n, 1), last:last + 1]))
        results = []
        for n, s, lhs, add, g_last in loaded:
            z = jnp.dot(lhs, s.astype(BF16), preferred_element_type=F32) + add
            results.append((n, jnp.exp(g_last) * s + z[CHUNK:], z[:CHUNK]))
        for d, (n, s_new, o) in enumerate(results):
            s_s[d] = s_new
            o_refs[d][pl.ds(pl.multiple_of(n * CHUNK, CHUNK), CHUNK), :] = o
        return carry

    lax.fori_loop(0, n_tot, step, 0)
    o_ref[0] += ob_s[...]


def gdn_scan(q, k, v, log_alpha, beta):
    b, t, _ = q.shape
    n_tot = t // CHUNK
    ncp = log_alpha.shape[3]
    blk = pl.BlockSpec((1, t, GDN_DK), lambda bi, hi: (bi, 0, hi))
    gate_blk = pl.BlockSpec((2, 1, 1, ncp, CHUNK), lambda bi, hi: (0, bi, hi, 0, 0))
    return pl.pallas_call(
        functools.partial(_gdn_kernel, n_ctx=CTX_LEN // CHUNK, n_tot=n_tot),
        grid=(b, GDN_HEADS),
        in_specs=[blk, blk, blk, gate_blk, gate_blk],
        out_specs=blk,
        out_shape=jax.ShapeDtypeStruct((b, t, GDN_HEADS * GDN_DV), F32),
        scratch_shapes=[pltpu.VMEM((2, ncp, CHUNK), F32),
                        pltpu.VMEM((2, n_tot, CHUNK + GDN_DK, GDN_DV), BF16),
                        pltpu.VMEM((2, n_tot, CHUNK + GDN_DK, GDN_DV), F32),
                        pltpu.VMEM((2, GDN_DK, GDN_DV), F32),
                        pltpu.VMEM((t, GDN_DV), F32)],
        compiler_params=_cparams(("parallel", "parallel")),
        name="gdn_scan",
    )(q, k, v, log_alpha, beta)


def _mlstm_kernel(q_ref, k_ref, v_ref, ip_ref, lf_ref, o_ref, b_s, c_s, n_s, m_s, *, n_ctx, n_tot):
    ii, jj, _, _ = _masks(0)
    eye = ii == jj
    for d in (0, 1):
        _, _, incl, _ = _masks(d)
        cum = (ii <= jj).astype(F32) if d == 0 else (ii >= jj).astype(F32)
        last = CHUNK - 1 if d == 0 else 0
        b_s[...] = _cumsum_rows(lf_ref[d, 0, 0], cum)
        c_s[...] = jnp.zeros_like(c_s)
        n_s[...] = jnp.zeros_like(n_s)
        m_s[...] = jnp.zeros_like(m_s)
        order = _chunk_order(d, n_ctx, n_tot)

        def body(j, carry, d=d, incl=incl, last=last, order=order):
            n = order(j)
            rows = pl.ds(pl.multiple_of(n * CHUNK, CHUNK), CHUNK)
            qc, kc, vc = q_ref[0, rows, :], k_ref[0, rows, :], v_ref[0, rows, :]
            b_row = b_s[pl.ds(n, 1), :]
            ip_row = ip_ref[d, 0, 0, pl.ds(n, 1), :]
            b_col = _row_to_col(b_row, eye)
            b_last = b_row[:, last:last + 1]
            log_d = jnp.where(incl, b_col - b_row + ip_row, NEG)
            m_intra = jnp.max(log_d, axis=1, keepdims=True)
            qk = _mm_nt(qc, kc)
            log_end = b_last - b_row + ip_row
            m_end = jnp.max(log_end, axis=1, keepdims=True)
            c_st, n_st, m_st = c_s[...], n_s[...], m_s[...]
            m_row = jnp.maximum(b_col + m_st, m_intra)
            w_intra = jnp.exp(log_d - m_row) * qk
            w_state = jnp.exp(b_col + m_st - m_row)
            num = w_state * _mm(qc, c_st) + _mm(w_intra, vc)
            den = (w_state * jnp.sum(qc * n_st, axis=1, keepdims=True)
                   + jnp.sum(w_intra, axis=1, keepdims=True))
            h = num / jnp.maximum(jnp.abs(den), jnp.exp(-m_row))
            m_new = jnp.maximum(b_last + m_st, m_end)
            decay = jnp.exp(b_last + m_st - m_new)
            k_w = kc * _row_to_col(jnp.exp(log_end - m_new), eye)
            c_s[...] = decay * c_st + _mm_tn(k_w, vc)
            n_s[...] = decay * n_st + jnp.sum(k_w, axis=0, keepdims=True)
            m_s[...] = m_new
            if d == 0:
                o_ref[0, rows, :] = h
            else:
                o_ref[0, rows, :] += h
            return carry

        lax.fori_loop(0, n_tot, body, 0)


def mlstm_scan(q, k, v, i_pre, log_f):
    b, t, _ = q.shape
    n_tot = t // CHUNK
    ncp = i_pre.shape[3]
    qk_blk = pl.BlockSpec((1, t, MLSTM_DQK), lambda bi, hi: (bi, 0, hi))
    v_blk = pl.BlockSpec((1, t, MLSTM_DV), lambda bi, hi: (bi, 0, hi))
    gate_blk = pl.BlockSpec((2, 1, 1, ncp, CHUNK), lambda bi, hi: (0, bi, hi, 0, 0))
    return pl.pallas_call(
        functools.partial(_mlstm_kernel, n_ctx=CTX_LEN // CHUNK, n_tot=n_tot),
        grid=(b, MLSTM_HEADS),
        in_specs=[qk_blk, qk_blk, v_blk, gate_blk, gate_blk],
        out_specs=v_blk,
        out_shape=jax.ShapeDtypeStruct((b, t, MLSTM_HEADS * MLSTM_DV), F32),
        scratch_shapes=[pltpu.VMEM((ncp, CHUNK), F32), pltpu.VMEM((MLSTM_DQK, MLSTM_DV), F32),
                        pltpu.VMEM((1, MLSTM_DQK), F32), pltpu.VMEM((1, 1), F32)],
        compiler_params=_cparams(("parallel", "parallel")),
        name="mlstm_scan",
    )(q, k, v, i_pre, log_f)


def _rwkv_kernel(r_ref, v_ref, kk_ref, lw_ref, kd_ref, bv_ref, o_ref, lhs_s, add_s, gl_s, s_s, ob_s, *, n_ctx, n_tot):
    ii, jj, _, _ = _masks(0)
    eye = ii == jj
    eye_f = eye.astype(F32)
    blockdiag = (ii // 16) == (jj // 16)
    hd = RWKV_HEAD
    dirs = []
    for d in (0, 1):
        _, _, incl, strict = _masks(d)
        dirs.append((incl, strict, CHUNK - 1 if d == 0 else 0))
    zero = jnp.zeros((hd, hd), F32)

    def prep(j, carry):
        loaded = []
        for uu in range(RWKV_UNROLL):
            n = j * RWKV_UNROLL + uu
            rows = pl.ds(pl.multiple_of(n * CHUNK, CHUNK), CHUNK)
            loaded.append((n, r_ref[0, rows, :], v_ref[0, rows, :], kk_ref[0, rows, :],
                           [(lw_ref[d, 0, rows, :], kd_ref[d, 0, rows, :], bv_ref[d, 0, rows, :]) for d in (0, 1)]))
        parts = {}

        def chain(key, r, v, kk, lw, kd, bv, gcs, e_end, incl, strict):
            e_neg = jnp.exp(-gcs)
            a_h = -kk * jnp.exp(gcs - lw)
            r_h = r * jnp.exp(gcs)
            b_h = bv * e_neg
            k_h = kd * e_neg
            a_ab = jnp.where(strict, _mm_nt(a_h, b_h), 0.0)
            a_ak = jnp.where(strict, _mm_nt(a_h, k_h), 0.0)
            a_rb = jnp.where(incl, _mm_nt(r_h, b_h), 0.0)
            a_rk = jnp.where(incl, _mm_nt(r_h, k_h), 0.0)
            yield
            av = _mm(a_ak, v)
            ta, tav = yield from _tri_solve_steps(a_ab, [a_h, av], eye_f, blockdiag)
            b_g = bv * e_end
            parts[key] = (r_h + _mm(a_rb, ta), _mm_tn(b_g, ta),
                          _mm(a_rb, tav) + _mm(a_rk, v), _mm_tn(b_g, tav) + _mm_tn(kd * e_end, v))

        gens, gls = [], {}
        for uu, (_, r2, v2, kk2, per_dir) in enumerate(loaded):
            for d, (incl, strict, last) in enumerate(dirs):
                lw2, kd2, bv2 = per_dir[d]
                gcs2 = _cumsum_cols(incl.astype(F32), lw2)
                gl_row2 = gcs2[last:last + 1, :]
                e_end2 = jnp.exp(gl_row2 - gcs2)
                gls[(uu, d)] = jnp.exp(gl_row2)
                for hh in range(2):
                    sl = slice(hh * hd, (hh + 1) * hd)
                    gens.append(chain((uu, d, hh), *(a[:, sl] for a in (r2, v2, kk2, lw2, kd2, bv2, gcs2, e_end2)),
                                      incl, strict))
        _run_interleaved(gens)
        for uu, (n, _, _, _, _) in enumerate(loaded):
            for d in (0, 1):
                (rt0, mx0, yc0, kv0), (rt1, mx1, yc1, kv1) = parts[(uu, d, 0)], parts[(uu, d, 1)]
                stack = lambda t0, t1, b0, b1: jnp.concatenate(
                    [jnp.concatenate([t0, t1], axis=1), jnp.concatenate([b0, zero], axis=1),
                     jnp.concatenate([zero, b1], axis=1)], axis=0)
                lhs_s[d, n] = stack(rt0, rt1, mx0, mx1).astype(BF16)
                add_s[d, n] = stack(yc0, yc1, kv0, kv1)
                gl_s[d, pl.ds(n, 1), :] = gls[(uu, d)]
        return carry

    lax.fori_loop(0, n_tot // RWKV_UNROLL, prep, 0)

    s_s[...] = jnp.zeros_like(s_s)
    orders = [_chunk_order(d, n_ctx, n_tot) for d in (0, 1)]
    eye2 = _iota2(2 * hd, 2 * hd, 0) == _iota2(2 * hd, 2 * hd, 1)
    o_refs = (o_ref.at[0], ob_s)

    def step(j, carry):
        loaded = []
        for d in (0, 1):
            n = orders[d](j)
            loaded.append((n, s_s[d], lhs_s[d, n], add_s[d, n], gl_s[d, pl.ds(n, 1), :]))
        results = []
        for n, s, lhs, add, gl in loaded:
            z = jnp.dot(lhs, s.astype(BF16), preferred_element_type=F32) + add
            results.append((n, _row_to_col(gl, eye2) * s + z[CHUNK:], z[:CHUNK]))
        for d, (n, s_new, y) in enumerate(results):
            s_s[d] = s_new
            o_refs[d][pl.ds(pl.multiple_of(n * CHUNK, CHUNK), CHUNK), :] = y
        return carry

    lax.fori_loop(0, n_tot, step, 0)
    o_ref[0] += ob_s[...]


def rwkv_scan(r, v, kk, log_w, k_dir, b_dir):
    b, t, _ = r.shape
    n_tot = t // CHUNK
    blk = pl.BlockSpec((1, t, 128), lambda bi, hi: (bi, 0, hi))
    dblk = pl.BlockSpec((2, 1, t, 128), lambda bi, hi: (0, bi, 0, hi))
    return pl.pallas_call(
        functools.partial(_rwkv_kernel, n_ctx=CTX_LEN // CHUNK, n_tot=n_tot),
        grid=(b, RWKV_HEADS // 2),
        in_specs=[blk, blk, blk, dblk, dblk, dblk],
        out_specs=blk,
        out_shape=jax.ShapeDtypeStruct((b, t, D_MODEL), F32),
        scratch_shapes=[pltpu.VMEM((2, n_tot, CHUNK + 2 * RWKV_HEAD, 2 * RWKV_HEAD), BF16),
                        pltpu.VMEM((2, n_tot, CHUNK + 2 * RWKV_HEAD, 2 * RWKV_HEAD), F32),
                        pltpu.VMEM((2, 8 * ((n_tot + 7) // 8), 2 * RWKV_HEAD), F32),
                        pltpu.VMEM((2, 2 * RWKV_HEAD, 2 * RWKV_HEAD), F32),
                        pltpu.VMEM((t, 2 * RWKV_HEAD), F32)],
        compiler_params=_cparams(("parallel", "parallel")),
        name="rwkv_scan",
    )(r, v, kk, log_w, k_dir, b_dir)


def _na_kernel(q_ref, k_ref, v_ref, bias_ref, o_ref, *, rows):
    scale = NA_DH ** -0.5
    slab = NA_WIN_ROWS * GRID_W
    kc2 = k_ref[0, 0:CTX_LEN, :]
    vc2 = v_ref[0, 0:CTX_LEN, :]

    def body(r, carry):
        r0 = jnp.clip(r - NA_WIN_ROWS // 2, 0, rows - NA_WIN_ROWS)
        dr0 = r0 - r + NA_WIN_ROWS - 1
        q2 = q_ref[0, pl.ds(pl.multiple_of(CTX_LEN + r * GRID_W, GRID_W), GRID_W), :]
        krows = pl.ds(pl.multiple_of(CTX_LEN + r0 * GRID_W, GRID_W), slab)
        ks2 = k_ref[0, krows, :]
        vs2 = v_ref[0, krows, :]
        outs = []
        for hh in range(2):
            sl = slice(hh * NA_DH, (hh + 1) * NA_DH)
            qh = q2[:, sl]
            s_lat = _mm_nt(qh, ks2[:, sl]) * scale + bias_ref[hh, pl.ds(dr0, 1)][0]
            s_ctx = _mm_nt(qh, kc2[:, sl]) * scale
            m = jnp.maximum(jnp.max(s_lat, axis=1, keepdims=True), jnp.max(s_ctx, axis=1, keepdims=True))
            p_lat = jnp.exp(s_lat - m)
            p_ctx = jnp.exp(s_ctx - m)
            l = jnp.sum(p_lat, axis=1, keepdims=True) + jnp.sum(p_ctx, axis=1, keepdims=True)
            outs.append((_mm(p_lat, vs2[:, sl]) + _mm(p_ctx, vc2[:, sl])) / l)
        o_ref[0, pl.ds(pl.multiple_of(r * GRID_W, GRID_W), GRID_W), :] = jnp.concatenate(outs, axis=1)
        return carry

    lax.fori_loop(0, rows, body, 0)


def _na_bias_table(rpb):
    cols = np.arange(GRID_W)
    win_c0 = np.clip(cols - NA_WIN_COLS // 2, 0, GRID_W - NA_WIN_COLS)
    kc = np.arange(GRID_W)
    in_win = (kc[None, :] >= win_c0[:, None]) & (kc[None, :] < win_c0[:, None] + NA_WIN_COLS)
    dc = np.clip(kc[None, :] - cols[:, None] + NA_WIN_COLS - 1, 0, 2 * NA_WIN_COLS - 2)
    dr = np.arange(NA_WIN_ROWS)[:, None] + np.arange(NA_WIN_ROWS)[None, :]
    tab = rpb.astype(F32)[:, dr][:, :, :, dc]
    tab = jnp.where(in_win[None, None, None], tab, NEG)
    tab = tab.transpose(0, 1, 3, 2, 4)
    return tab.reshape(NA_HEADS, NA_WIN_ROWS, GRID_W, NA_WIN_ROWS * GRID_W)


def na_attention(z, bias_tab):
    b, t, _ = z.shape
    t_lat = t - CTX_LEN
    n_pairs = NA_HEADS // 2
    return pl.pallas_call(
        functools.partial(_na_kernel, rows=t_lat // GRID_W),
        grid=(n_pairs, b),
        in_specs=[pl.BlockSpec((1, t, 128), lambda hi, bi: (bi, 0, hi)),
                  pl.BlockSpec((1, t, 128), lambda hi, bi: (bi, 0, n_pairs + hi)),
                  pl.BlockSpec((1, t, 128), lambda hi, bi: (bi, 0, 2 * n_pairs + hi)),
                  pl.BlockSpec((2, NA_WIN_ROWS, GRID_W, NA_WIN_ROWS * GRID_W), lambda hi, bi: (hi, 0, 0, 0))],
        out_specs=pl.BlockSpec((1, t_lat, 128), lambda hi, bi: (bi, 0, hi)),
        out_shape=jax.ShapeDtypeStruct((b, t_lat, D_MODEL), F32),
        compiler_params=_cparams(("parallel", "parallel")),
        name="na_attention",
    )(z, z, z, bias_tab)


def _router_kernel(x_ref, modl_ref, modc_ref, rw_ref, rb_ref, hb_ref, idx_ref, wt_ref, *, tile_off):
    is_ctx = (pl.program_id(1) + tile_off) * TOKEN_TILE < CTX_LEN
    sh = _mod_rows(modl_ref, modc_ref, is_ctx, 3)
    sc = _mod_rows(modl_ref, modc_ref, is_ctx, 4)
    h = x_ref[0] * (1.0 + sc) + sh
    hb_ref[0] = h.astype(BF16)
    logits = _mmf(h, rw_ref[...])
    mx = jnp.max(logits, axis=1, keepdims=True)
    ex = jnp.exp(logits - mx)
    probs = ex / jnp.sum(ex, axis=1, keepdims=True)
    sel = probs + rb_ref[...]
    tm = sel.shape[0]
    e_id = _iota2(tm, N_EXPERTS, 1)
    per_group = N_EXPERTS // N_GROUPS
    g_id = e_id // per_group

    def top1(vals):
        m1 = jnp.max(vals, axis=1, keepdims=True)
        i1 = jnp.min(jnp.where(vals == m1, e_id, N_EXPERTS), axis=1, keepdims=True)
        return m1, i1

    best_score, best = None, None
    for g in range(N_GROUPS):
        vals = jnp.where(g_id == g, sel, NEG)
        m1, i1 = top1(vals)
        m2, _ = top1(jnp.where(e_id == i1, NEG, vals))
        score = m1 + m2
        if g == 0:
            best_score, best = score, jnp.zeros_like(i1)
        else:
            better = score > best_score
            best = jnp.where(better, g, best)
            best_score = jnp.where(better, score, best_score)
    vals = jnp.where(g_id == best, sel, NEG)
    _, i1 = top1(vals)
    _, i2 = top1(jnp.where(e_id == i1, NEG, vals))
    p1 = jnp.sum(jnp.where(e_id == i1, probs, 0.0), axis=1, keepdims=True)
    p2 = jnp.sum(jnp.where(e_id == i2, probs, 0.0), axis=1, keepdims=True)
    tot = p1 + p2
    slot = _iota2(tm, 2, 1)
    idx_ref[0] = jnp.where(slot == 0, i1, i2)
    wt_ref[0] = jnp.where(slot == 0, p1 / tot, p2 / tot)


def moe_route(x, mods, router_w, router_b, *, tile_off):
    b, t_out, _ = x.shape
    modl, modc = mods
    return pl.pallas_call(
        functools.partial(_router_kernel, tile_off=tile_off),
        grid=(b, t_out // TOKEN_TILE),
        in_specs=[pl.BlockSpec((1, TOKEN_TILE, D_MODEL), lambda bi, ti: (bi, ti, 0)),
                  pl.BlockSpec((1, 1, 6 * D_MODEL), lambda bi, ti: (bi, 0, 0)),
                  pl.BlockSpec((1, 1, 6 * D_MODEL), lambda bi, ti: (0, 0, 0)),
                  pl.BlockSpec((D_MODEL, N_EXPERTS), lambda bi, ti: (0, 0)),
                  pl.BlockSpec((1, N_EXPERTS), lambda bi, ti: (0, 0))],
        out_specs=[pl.BlockSpec((1, TOKEN_TILE, D_MODEL), lambda bi, ti: (bi, ti, 0)),
                   pl.BlockSpec((1, TOKEN_TILE, 2), lambda bi, ti: (bi, ti, 0)),
                   pl.BlockSpec((1, TOKEN_TILE, 2), lambda bi, ti: (bi, ti, 0))],
        out_shape=[jax.ShapeDtypeStruct((b, t_out, D_MODEL), BF16),
                   jax.ShapeDtypeStruct((b, t_out, 2), jnp.int32),
                   jax.ShapeDtypeStruct((b, t_out, 2), F32)],
        compiler_params=_cparams(("parallel", "parallel")),
        name="moe_route",
    )(x, modl, modc, router_w, router_b.reshape(1, -1))


def _ffn_kernel(te_ref, nt_ref, x_ref, w1_ref, w3_ref, w2_ref, o_ref):
    @pl.when(pl.program_id(0) < nt_ref[0])
    def _():
        xb = x_ref[...]
        h1 = jnp.dot(xb, w1_ref[0], preferred_element_type=F32)
        h3 = jnp.dot(xb, w3_ref[0], preferred_element_type=F32)
        hid = (_silu(h1) * h3).astype(BF16)
        o_ref[...] = jnp.dot(hid, w2_ref[0], preferred_element_type=F32)

    @pl.when(pl.program_id(0) >= nt_ref[0])
    def _():
        o_ref[...] = jnp.zeros_like(o_ref)


def expert_ffn(xs, tile_expert, n_tiles_used, w1, w3, w2):
    p = xs.shape[0]
    grid_spec = pltpu.PrefetchScalarGridSpec(
        num_scalar_prefetch=2,
        grid=(p // FFN_TILE,),
        in_specs=[pl.BlockSpec((FFN_TILE, D_MODEL), lambda i, te, nt: (i, 0)),
                  pl.BlockSpec((1, D_MODEL, D_EXPERT), lambda i, te, nt: (te[i], 0, 0)),
                  pl.BlockSpec((1, D_MODEL, D_EXPERT), lambda i, te, nt: (te[i], 0, 0)),
                  pl.BlockSpec((1, D_EXPERT, D_MODEL), lambda i, te, nt: (te[i], 0, 0))],
        out_specs=pl.BlockSpec((FFN_TILE, D_MODEL), lambda i, te, nt: (i, 0)),
    )
    return pl.pallas_call(
        _ffn_kernel,
        grid_spec=grid_spec,
        out_shape=jax.ShapeDtypeStruct((p, D_MODEL), F32),
        compiler_params=_cparams(("arbitrary",)),
        name="expert_ffn",
    )(tile_expert, n_tiles_used, xs, w1, w3, w2)


def _combine_ln_kernel(x_ref, y0_ref, y1_ref, wt_ref, modl_ref, modc_ref, g_ref, b_ref, o_ref, *, tile_off):
    is_ctx = (pl.program_id(1) + tile_off) * TOKEN_TILE < CTX_LEN
    gate = _mod_rows(modl_ref, modc_ref, is_ctx, 5)
    wt = wt_ref[0]
    f = wt[:, 0:1] * y0_ref[0] + wt[:, 1:2] * y1_ref[0]
    r = ALPHA * x_ref[0] + gate * f
    o_ref[0] = _layer_norm_rows(r, g_ref[...], b_ref[...])


def combine_ln(x, y0, y1, wt, mods, ln_g, ln_b, *, tile_off):
    b, t_out, _ = y0.shape
    modl, modc = mods
    tok = lambda bi, ti: (bi, ti, 0)
    return pl.pallas_call(
        functools.partial(_combine_ln_kernel, tile_off=tile_off),
        grid=(b, t_out // TOKEN_TILE),
        in_specs=[pl.BlockSpec((1, TOKEN_TILE, D_MODEL), tok),
                  pl.BlockSpec((1, TOKEN_TILE, D_MODEL), tok),
                  pl.BlockSpec((1, TOKEN_TILE, D_MODEL), tok),
                  pl.BlockSpec((1, TOKEN_TILE, 2), tok),
                  pl.BlockSpec((1, 1, 6 * D_MODEL), lambda bi, ti: (bi, 0, 0)),
                  pl.BlockSpec((1, 1, 6 * D_MODEL), lambda bi, ti: (0, 0, 0)),
                  pl.BlockSpec((1, D_MODEL), lambda bi, ti: (0, 0)),
                  pl.BlockSpec((1, D_MODEL), lambda bi, ti: (0, 0))],
        out_specs=pl.BlockSpec((1, TOKEN_TILE, D_MODEL), tok),
        out_shape=jax.ShapeDtypeStruct((b, t_out, D_MODEL), F32),
        compiler_params=_cparams(("parallel", "parallel")),
        name="moe_combine_ln",
    )(x, y0, y1, wt, modl, modc, ln_g.reshape(1, -1), ln_b.reshape(1, -1))


def moe_layer(x, mods, router_w, router_b, w1, w3, w2, ln_g, ln_b, *, tile_off):
    hb, idx, wt = moe_route(x, mods, router_w, router_b, tile_off=tile_off)
    b, t, _ = hb.shape
    n_tok = b * t
    n_pair = 2 * n_tok
    e_flat = idx.reshape(n_pair)
    onehot = (e_flat[:, None] == jnp.arange(N_EXPERTS)[None, :]).astype(jnp.int32)
    csum = jnp.cumsum(onehot, axis=0)
    counts = csum[-1]
    rank = jnp.sum((csum - onehot) * onehot, axis=1)
    padded = ((counts + FFN_TILE - 1) // FFN_TILE) * FFN_TILE
    ends = jnp.cumsum(padded)
    offs = ends - padded
    pos = offs[e_flat] + rank
    n_rows = n_pair + N_EXPERTS * FFN_TILE
    n_tiles = n_rows // FFN_TILE
    src = jnp.zeros((n_rows,), jnp.int32).at[pos].set(jnp.arange(n_pair, dtype=jnp.int32) // 2)
    tile_start = jnp.arange(n_tiles, dtype=jnp.int32) * FFN_TILE
    tile_expert = jnp.minimum(jnp.searchsorted(ends, tile_start, side="right"), N_EXPERTS - 1).astype(jnp.int32)
    n_used = (ends[-1] // FFN_TILE).astype(jnp.int32).reshape(1)
    xs = jnp.take(hb.reshape(n_tok, D_MODEL), src, axis=0)
    ys = expert_ffn(xs, tile_expert, n_used, w1, w3, w2)
    pos2 = pos.reshape(n_tok, 2)
    y0 = jnp.take(ys, pos2[:, 0], axis=0).reshape(b, t, D_MODEL)
    y1 = jnp.take(ys, pos2[:, 1], axis=0).reshape(b, t, D_MODEL)
    return combine_ln(x, y0, y1, wt, mods, ln_g, ln_b, tile_off=tile_off)


def _seg_apply(fn, a):
    return jnp.concatenate([fn(a[:, :CTX_LEN]), fn(a[:, CTX_LEN:])], axis=1)


def _rope_tables(t_lat, dh):
    quarter = dh // 4
    pos = jnp.arange(t_lat)
    inv_freq = ROPE_BASE ** (-jnp.arange(quarter, dtype=F32) / quarter)
    ang_r = (pos // GRID_W).astype(F32)[:, None] * inv_freq[None, :]
    ang_c = (pos % GRID_W).astype(F32)[:, None] * inv_freq[None, :]
    return jnp.cos(ang_r), jnp.sin(ang_r), jnp.cos(ang_c), jnp.sin(ang_c)


def _rope_lat(a, n_heads):
    b, t, hd = a.shape
    dh = hd // n_heads
    q4 = dh // 4
    lat = a[:, CTX_LEN:].reshape(b, t - CTX_LEN, n_heads, dh)
    cr, sr, cc, sc = (u[None, :, None, :] for u in _rope_tables(t - CTX_LEN, dh))
    x1, x2, x3, x4 = lat[..., :q4], lat[..., q4:2 * q4], lat[..., 2 * q4:3 * q4], lat[..., 3 * q4:]
    out = jnp.concatenate([x1 * cr - x2 * sr, x1 * sr + x2 * cr, x3 * cc - x4 * sc, x3 * sc + x4 * cc], axis=-1)
    return jnp.concatenate([a[:, :CTX_LEN], out.reshape(b, t - CTX_LEN, hd)], axis=1)


def _head_l2norm(a, n_heads, eps=1e-6):
    b, t, hd = a.shape
    ah = a.reshape(b, t, n_heads, hd // n_heads)
    return (ah * lax.rsqrt(jnp.sum(ah * ah, axis=-1, keepdims=True) + eps)).reshape(b, t, hd)


def _gate_rows(g):
    b, t = g.shape[:2]
    n = t // CHUNK
    ncp = 8 * ((n + 7) // 8)
    g = g.transpose(2, 3, 0, 4, 1).reshape(2, 2, b, g.shape[-1], n, CHUNK)
    return jnp.pad(g, ((0, 0),) * 4 + ((0, ncp - n), (0, 0)))


def _dwconv3(a, w):
    ap = jnp.pad(a, ((0, 0), (1, 1), (0, 0)))
    return ap[:, :-2] * w[0] + ap[:, 1:-1] * w[1] + ap[:, 2:] * w[2]


def gdn_layer(x, mods, w_in, conv_w, a_log, dt_bias, norm_g, w_out):
    wq = GDN_HEADS * GDN_DK
    n_main = 4 * wq
    z, ab = linear(x, w_in[:, :n_main].astype(BF16), mods=mods, modulate=(0, 1), w_small=w_in[:, n_main:])
    qkv = jax.nn.silu(_seg_apply(lambda s: _dwconv3(s, conv_w), z[..., :3 * wq]))
    q = _rope_lat(_head_l2norm(qkv[..., :wq], GDN_HEADS), GDN_HEADS) * GDN_DK ** -0.5
    k = _rope_lat(_head_l2norm(qkv[..., wq:2 * wq], GDN_HEADS), GDN_HEADS)
    v = qkv[..., 2 * wq:]
    gate = z[..., 3 * wq:]
    b, t, _ = x.shape
    ab = ab.reshape(b, t, 2, 2, GDN_HEADS)
    ab = ab.at[:, :, :, 0].add(dt_bias[None, None])
    rows = _gate_rows(ab)
    log_alpha = -jnp.exp(a_log)[:, None, :, None, None] * jax.nn.softplus(rows[:, 0])
    beta = jax.nn.sigmoid(rows[:, 1])
    valid = (jnp.arange(rows.shape[4]) < t // CHUNK)[:, None]
    log_alpha = jnp.where(valid, log_alpha, 0.0)
    o = gdn_scan(q, k, v, log_alpha, beta)
    oh = o.reshape(b, t, GDN_HEADS, GDN_DV)
    oh = oh * lax.rsqrt(jnp.mean(oh * oh, axis=-1, keepdims=True) + 1e-6) * norm_g
    y = oh.reshape(b, t, -1) * jax.nn.silu(gate)
    return y, w_out


def mlstm_layer(x, mods, w_in, gate_b, norm_g, w_out):
    wq = MLSTM_HEADS * MLSTM_DQK
    wv = MLSTM_HEADS * MLSTM_DV
    n_main = 2 * wq + 2 * wv
    z, gt = linear(x, w_in[:, :n_main].astype(BF16), mods=mods, modulate=(0, 1), w_small=w_in[:, n_main:])
    q = _rope_lat(z[..., :wq], MLSTM_HEADS) * MLSTM_DQK ** -0.5
    k = _rope_lat(z[..., wq:2 * wq], MLSTM_HEADS)
    v = z[..., 2 * wq:2 * wq + wv]
    o_gate = z[..., 2 * wq + wv:]
    b, t, _ = x.shape
    gt = gt.reshape(b, t, 2, 2, MLSTM_HEADS) + gate_b[None, None]
    rows = _gate_rows(gt)
    valid = (jnp.arange(rows.shape[4]) < t // CHUNK)[:, None]
    i_pre = rows[:, 0]
    log_f = jnp.where(valid, jax.nn.log_sigmoid(rows[:, 1]), 0.0)
    h = mlstm_scan(q, k, v, i_pre, log_f)
    hh = h.reshape(b, t, MLSTM_HEADS, MLSTM_DV)
    mu = jnp.mean(hh, axis=-1, keepdims=True)
    var = jnp.mean(jnp.square(hh - mu), axis=-1, keepdims=True)
    hn = ((hh - mu) * lax.rsqrt(var + 1e-6)).reshape(b, t, -1)
    y = hn * norm_g * jax.nn.sigmoid(o_gate)
    return y, w_out


def _modulated(x, mods, sh_idx, sc_idx):
    modl, modc = mods
    d = D_MODEL
    is_ctx = (jnp.arange(x.shape[1]) < CTX_LEN)[None, :, None]
    sh_c, sh_l = modc[:, :, sh_idx * d:(sh_idx + 1) * d], modl[:, :, sh_idx * d:(sh_idx + 1) * d]
    sc_c, sc_l = modc[:, :, sc_idx * d:(sc_idx + 1) * d], modl[:, :, sc_idx * d:(sc_idx + 1) * d]
    return x * (1.0 + jnp.where(is_ctx, sc_c, sc_l)) + jnp.where(is_ctx, sh_c, sh_l)


def _token_shift(a):
    ap = jnp.pad(a, ((0, 0), (1, 1), (0, 0)))
    return 0.5 * (ap[:, :-2] + ap[:, 2:])


def rwkv_layer(x, mods, mu, w_rkv, w0, w1, w2, a0, a1, a2, g1, g2, k_k, k_a, r_k, lnx_g, lnx_b, w_out):
    b, t, d = x.shape
    nh, hd = RWKV_HEADS, RWKV_HEAD
    h = _modulated(x, mods, 0, 1)
    dx = _seg_apply(_token_shift, h) - h
    xr, xw, xk, xv, xa, xg = [h + dx * mu[j] for j in range(6)]
    r = linear(xr, w_rkv[0].astype(BF16))
    k = linear(xk, w_rkv[1].astype(BF16))
    v = linear(xv, w_rkv[2].astype(BF16))
    g = linear(linear(xg, g1.astype(BF16)), g2.astype(BF16), act="sigmoid")
    heads = lambda a: a.reshape(b, t, nh, hd)
    kk = _head_l2norm(k * k_k, nh)
    lws, kds, bvs = [], [], []
    for dr in range(2):
        lo = linear(linear(xw, w1[dr].astype(BF16)), w2[dr].astype(BF16), act="tanh")
        w_raw = -jax.nn.softplus(-(w0[dr] + lo)) - 0.5
        a = jax.nn.sigmoid(a0[dr] + linear(linear(xa, a1[dr].astype(BF16)), a2[dr].astype(BF16)))
        lws.append(-jnp.exp(w_raw))
        kds.append(k * (1.0 + (a - 1.0) * k_a))
        bvs.append(kk * a)
    y = rwkv_scan(r, v, kk, jnp.stack(lws), jnp.stack(kds), jnp.stack(bvs))
    yh = heads(y)
    m = jnp.mean(yh, axis=-1, keepdims=True)
    var = jnp.mean(jnp.square(yh - m), axis=-1, keepdims=True)
    yn = ((yh - m) * lax.rsqrt(var + RWKV_GN_EPS)).reshape(b, t, d)
    bonus = sum(jnp.sum(heads(r) * heads(kd) * r_k, axis=-1, keepdims=True) * heads(v) for kd in kds)
    yo = (yn * lnx_g + lnx_b + bonus.reshape(b, t, d)) * g
    return yo, w_out


def kernel(x, c, ctx, c_ctx, ada_w, ada_b, ln_g, ln_b, router_w, router_b, moe_w1, moe_w3, moe_w2, gdn_w_in, gdn_conv, gdn_a_log, gdn_dt_bias, gdn_norm_g, gdn_w_out, mlstm_w_in, mlstm_gate_b, mlstm_norm_g, mlstm_w_out, rwkv_mu, rwkv_w_rkv, rwkv_w0, rwkv_w1, rwkv_w2, rwkv_a0, rwkv_a1, rwkv_a2, rwkv_g1, rwkv_g2, rwkv_k_k, rwkv_k_a, rwkv_r_k, rwkv_lnx_g, rwkv_lnx_b, rwkv_w_out, na_w_in, na_rpb, na_w_out):
    b = x.shape[0]
    ctx_tiles = CTX_LEN // TOKEN_TILE
    mod_all = modulation_all(c, c_ctx, ada_w, ada_b)
    xs = jnp.concatenate([ctx, x], axis=1)
    assert DEPTH == 4
    for i in range(DEPTH):
        mods = (mod_all[i, :b, None, :], mod_all[i, b:b + 1, None, :])
        if i % 4 == 0:
            y, w_out = gdn_layer(xs, mods, gdn_w_in, gdn_conv, gdn_a_log, gdn_dt_bias, gdn_norm_g, gdn_w_out)
        elif i % 4 == 1:
            y, w_out = mlstm_layer(xs, mods, mlstm_w_in, mlstm_gate_b, mlstm_norm_g, mlstm_w_out)
        elif i % 4 == 2:
            y, w_out = rwkv_layer(xs, mods, rwkv_mu, rwkv_w_rkv, rwkv_w0, rwkv_w1, rwkv_w2, rwkv_a0, rwkv_a1,
                                  rwkv_a2, rwkv_g1, rwkv_g2, rwkv_k_k, rwkv_k_a, rwkv_r_k, rwkv_lnx_g,
                                  rwkv_lnx_b, rwkv_w_out)
        else:
            z = linear(xs, na_w_in.astype(BF16), mods=mods, modulate=(0, 1))
            y, w_out = na_attention(z, _na_bias_table(na_rpb)), na_w_out
        off = ctx_tiles if (i % 4 == 3) else 0
        xs1 = out_proj_ln(y, w_out.astype(BF16), xs, mods, ln_g[i, 0], ln_b[i, 0], gate_idx=2, tile_off=off)
        xs = moe_layer(xs1, mods, router_w, router_b, moe_w1[i].astype(BF16), moe_w3[i].astype(BF16),
                       moe_w2[i].astype(BF16), ln_g[i, 1], ln_b[i, 1], tile_off=off)
    return xs
```

```python
import functools
import math

import numpy as np
import jax
import jax.numpy as jnp
from jax import lax
from jax.experimental import pallas as pl
from jax.experimental.pallas import tpu as pltpu

F32 = jnp.float32
BF16 = jnp.bfloat16

D_MODEL = 1024
DEPTH = 4
GRID_W = 64
CTX_LEN = 256
ALPHA = (2 * DEPTH) ** 0.25
LN_EPS = 1e-5
ROPE_BASE = 10000.0

GDN_HEADS = 8
GDN_DK = 128
GDN_DV = 128
MLSTM_HEADS = 4
MLSTM_DQK = 128
MLSTM_DV = 256
RWKV_HEAD = 64
RWKV_HEADS = 16
RWKV_GN_EPS = 64e-5
NA_HEADS = 16
NA_DH = 64
NA_WIN_ROWS = 8
NA_WIN_COLS = 16
N_EXPERTS = 16
N_GROUPS = 4
D_EXPERT = 512

CHUNK = 64
GDN_UNROLL = 4
RWKV_UNROLL = 2
NA_ROWS_PER_STEP = 2
MLSTM_HEADS_PER_STEP = 2
TOKEN_TILE = 256
N_CHUNK_COLS = 512
FFN_TILE = 512
NEG = -1e30
VMEM_LIMIT = 56 * 1024 * 1024

_HI = lax.Precision.HIGHEST


def _cparams(sem):
    return pltpu.CompilerParams(dimension_semantics=sem, vmem_limit_bytes=VMEM_LIMIT)


def _mm(a, b):
    return jnp.dot(a.astype(BF16), b.astype(BF16), preferred_element_type=F32)


def _mm_nt(a, b):
    return lax.dot_general(a.astype(BF16), b.astype(BF16), (((1,), (1,)), ((), ())),
                           preferred_element_type=F32)


def _mm_tn(a, b):
    return lax.dot_general(a.astype(BF16), b.astype(BF16), (((0,), (0,)), ((), ())),
                           preferred_element_type=F32)


def _mmf(a, b):
    return jnp.dot(a, b, preferred_element_type=F32, precision=_HI)


def _silu(x):
    return x * (1.0 / (1.0 + jnp.exp(-x)))


def _sigmoid(x):
    return 1.0 / (1.0 + jnp.exp(-x))


def _mod_kernel(s_ref, w_ref, b_ref, o_ref):
    o_ref[0] = _mmf(_silu(s_ref[...]), w_ref[0]) + b_ref[0]


def modulation_all(c, c_ctx, ada_w, ada_b):
    b = c.shape[0]
    rows = 8 * ((b + 1 + 7) // 8)
    s = jnp.zeros((rows, D_MODEL), F32).at[:b].set(c).at[b].set(c_ctx)
    tn = 1536
    n = ada_w.shape[-1]
    return pl.pallas_call(
        _mod_kernel,
        grid=(DEPTH, n // tn),
        in_specs=[pl.BlockSpec((rows, D_MODEL), lambda i, j: (0, 0)),
                  pl.BlockSpec((1, D_MODEL, tn), lambda i, j: (i, 0, j)),
                  pl.BlockSpec((1, 1, tn), lambda i, j: (i, 0, j))],
        out_specs=pl.BlockSpec((1, rows, tn), lambda i, j: (i, 0, j)),
        out_shape=jax.ShapeDtypeStruct((DEPTH, rows, n), F32),
        compiler_params=_cparams(("arbitrary", "arbitrary")),
        name="adaln_modulation",
    )(s, ada_w, ada_b.reshape(DEPTH, 1, n))


def _mod_rows(modl_ref, modc_ref, is_ctx, idx):
    sl = slice(idx * D_MODEL, (idx + 1) * D_MODEL)
    return jnp.where(is_ctx, modc_ref[0, :, sl], modl_ref[0, :, sl])


def _linear_kernel(*refs, glob_off, modulate, act, n_main, has_small):
    it = iter(refs)
    x_ref = next(it)
    if modulate is not None:
        modl_ref, modc_ref = next(it), next(it)
    w_ref = next(it)
    ws_ref = next(it) if has_small else None
    o_ref = next(it)
    os_ref = next(it) if has_small else None

    h = x_ref[0]
    if modulate is not None:
        is_ctx = (pl.program_id(1) + glob_off) * TOKEN_TILE < CTX_LEN
        sh = _mod_rows(modl_ref, modc_ref, is_ctx, modulate[0])
        sc = _mod_rows(modl_ref, modc_ref, is_ctx, modulate[1])
        h = h * (1.0 + sc) + sh
    if act == "tanh":
        h = jnp.tanh(h)
    elif act == "sigmoid":
        h = _sigmoid(h)
    hb = h.astype(BF16)
    step = min(N_CHUNK_COLS, n_main)
    for j in range(n_main // step):
        o_ref[0, :, j * step:(j + 1) * step] = jnp.dot(
            hb, w_ref[:, j * step:(j + 1) * step], preferred_element_type=F32)
    if has_small:
        os_ref[0] = _mmf(h, ws_ref[...])


def linear(x, w_bf16, *, mods=None, modulate=None, act=None, w_small=None):
    b, t_out, k = x.shape
    n_main = w_bf16.shape[1]
    has_small = w_small is not None
    in_specs = [pl.BlockSpec((1, TOKEN_TILE, k), lambda bi, ti: (bi, ti, 0))]
    args = [x]
    if modulate is not None:
        modl, modc = mods
        in_specs += [pl.BlockSpec((1, 1, 6 * D_MODEL), lambda bi, ti: (bi, 0, 0)),
                     pl.BlockSpec((1, 1, 6 * D_MODEL), lambda bi, ti: (0, 0, 0))]
        args += [modl, modc]
    in_specs.append(pl.BlockSpec((k, n_main), lambda bi, ti: (0, 0)))
    args.append(w_bf16)
    out_specs = [pl.BlockSpec((1, TOKEN_TILE, n_main), lambda bi, ti: (bi, ti, 0))]
    out_shape = [jax.ShapeDtypeStruct((b, t_out, n_main), F32)]
    if has_small:
        ns = w_small.shape[1]
        in_specs.append(pl.BlockSpec((k, ns), lambda bi, ti: (0, 0)))
        args.append(w_small)
        out_specs.append(pl.BlockSpec((1, TOKEN_TILE, ns), lambda bi, ti: (bi, ti, 0)))
        out_shape.append(jax.ShapeDtypeStruct((b, t_out, ns), F32))
    res = pl.pallas_call(
        functools.partial(_linear_kernel, glob_off=0, modulate=modulate, act=act,
                          n_main=n_main, has_small=has_small),
        grid=(b, t_out // TOKEN_TILE),
        in_specs=in_specs, out_specs=out_specs, out_shape=out_shape,
        compiler_params=_cparams(("parallel", "parallel")),
        name="linear",
    )(*args)
    return res if has_small else res[0]


def _layer_norm_rows(r, g, b):
    mu = jnp.mean(r, axis=-1, keepdims=True)
    rc = r - mu
    var = jnp.mean(rc * rc, axis=-1, keepdims=True)
    return rc * lax.rsqrt(var + LN_EPS) * g + b


def _out_ln_kernel(y_ref, w_ref, x_ref, modl_ref, modc_ref, g_ref, b_ref, o_ref, *, tile_off, gate_idx):
    is_ctx = (pl.program_id(1) + tile_off) * TOKEN_TILE < CTX_LEN
    gate = _mod_rows(modl_ref, modc_ref, is_ctx, gate_idx)
    f = jnp.dot(y_ref[0].astype(BF16), w_ref[...], preferred_element_type=F32)
    r = ALPHA * x_ref[0] + gate * f
    o_ref[0] = _layer_norm_rows(r, g_ref[...], b_ref[...])


def out_proj_ln(y, w_bf16, x, mods, ln_g, ln_b, *, gate_idx, tile_off=0):
    b, t_y, k = y.shape
    modl, modc = mods
    return pl.pallas_call(
        functools.partial(_out_ln_kernel, tile_off=tile_off, gate_idx=gate_idx),
        grid=(b, t_y // TOKEN_TILE),
        in_specs=[pl.BlockSpec((1, TOKEN_TILE, k), lambda bi, ti: (bi, ti, 0)),
                  pl.BlockSpec((k, D_MODEL), lambda bi, ti: (0, 0)),
                  pl.BlockSpec((1, TOKEN_TILE, D_MODEL), lambda bi, ti: (bi, ti + tile_off, 0)),
                  pl.BlockSpec((1, 1, 6 * D_MODEL), lambda bi, ti: (bi, 0, 0)),
                  pl.BlockSpec((1, 1, 6 * D_MODEL), lambda bi, ti: (0, 0, 0)),
                  pl.BlockSpec((1, D_MODEL), lambda bi, ti: (0, 0)),
                  pl.BlockSpec((1, D_MODEL), lambda bi, ti: (0, 0))],
        out_specs=pl.BlockSpec((1, TOKEN_TILE, D_MODEL), lambda bi, ti: (bi, ti, 0)),
        out_shape=jax.ShapeDtypeStruct((b, t_y, D_MODEL), F32),
        compiler_params=_cparams(("parallel", "parallel")),
        name="out_proj_ln",
    )(y, w_bf16, x, modl, modc, ln_g.reshape(1, -1), ln_b.reshape(1, -1))


def _iota2(n, m, axis):
    return lax.broadcasted_iota(jnp.int32, (n, m), axis)


def _row_to_col(row, eye):
    return jnp.sum(jnp.where(eye, row, 0.0), axis=1, keepdims=True)


def _tri_solve_steps(n_mat, rhs_list, eye_f, blockdiag):
    nd = jnp.where(blockdiag, n_mat, 0.0)
    ne = n_mat - nd
    p = eye_f + nd
    n2 = _mm(nd, nd)
    yield
    p = p + _mm(p, n2)
    n4 = _mm(n2, n2)
    yield
    p = p + _mm(p, n4)
    n8 = _mm(n4, n4)
    yield
    dinv = p + _mm(p, n8)
    yield
    m = _mm(dinv, ne)
    xs = [_mm(dinv, r) for r in rhs_list]
    yield
    m2 = _mm(m, m)
    xs = [x + _mm(m, x) for x in xs]
    yield
    xs = [x + _mm(m2, x) for x in xs]
    yield
    return xs


def _run_interleaved(gens):
    gens = list(gens)
    while gens:
        alive = []
        for g in gens:
            try:
                next(g)
                alive.append(g)
            except StopIteration:
                pass
        gens = alive


def _split2(x):
    hi = x.astype(BF16)
    return hi, (x - hi.astype(F32)).astype(BF16)


def _cumsum_rows(x, cum):
    hi, lo = _split2(x)
    cb = cum.astype(BF16)
    return jnp.dot(hi, cb, preferred_element_type=F32) + jnp.dot(lo, cb, preferred_element_type=F32)


def _cumsum_cols(cum, x):
    hi, lo = _split2(x)
    cb = cum.astype(BF16)
    return jnp.dot(cb, hi, preferred_element_type=F32) + jnp.dot(cb, lo, preferred_element_type=F32)


def _chunk_order(direction, n_ctx, n_tot):
    def order(j):
        if direction == 0:
            return j
        return jnp.where(j < n_ctx, n_ctx - 1 - j, n_tot - 1 - (j - n_ctx))
    return order


def _masks(direction):
    ii = _iota2(CHUNK, CHUNK, 0)
    jj = _iota2(CHUNK, CHUNK, 1)
    if direction == 0:
        incl, strict = jj <= ii, jj < ii
    else:
        incl, strict = jj >= ii, jj > ii
    return ii, jj, incl, strict


def _gdn_kernel(q_ref, k_ref, v_ref, la_ref, be_ref, o_ref, g_s, lhs_s, add_s, s_s, ob_s, *, n_ctx, n_tot):
    ii, jj, _, _ = _masks(0)
    eye = ii == jj
    eye_f = eye.astype(F32)
    blockdiag = (ii // 16) == (jj // 16)
    dirs = []
    for d in (0, 1):
        _, _, incl, strict = _masks(d)
        cum = (ii <= jj).astype(F32) if d == 0 else (ii >= jj).astype(F32)
        g_s[d] = _cumsum_rows(la_ref[d, 0, 0], cum)
        dirs.append((incl, strict, CHUNK - 1 if d == 0 else 0))

    def prep(j, carry):
        loaded = []
        for uu in range(GDN_UNROLL):
            n = j * GDN_UNROLL + uu
            rows = pl.ds(pl.multiple_of(n * CHUNK, CHUNK), CHUNK)
            loaded.append((n, q_ref[0, rows, :], k_ref[0, rows, :], v_ref[0, rows, :],
                           [g_s[d, pl.ds(n, 1), :] for d in (0, 1)],
                           [be_ref[d, 0, 0, pl.ds(n, 1), :] for d in (0, 1)]))
        stores = []
        grams = [(_mm_nt(kc, kc), _mm_nt(qc, kc)) for _, qc, kc, _, _, _ in loaded]

        def chain(n, qc, kc, vc, g_row, be_row, kk, qk, d, incl, strict, last):
            g_col = _row_to_col(g_row, eye)
            be_col = _row_to_col(be_row, eye)
            g_last = g_row[:, last:last + 1]
            gamma = jnp.exp(jnp.where(incl, g_col - g_row, NEG))
            n_mat = jnp.where(strict, -(be_col * kk * gamma), 0.0)
            eg = jnp.exp(g_col)
            rhs = jnp.concatenate([be_col * vc, (be_col * eg) * kc], axis=1)
            (uw,) = yield from _tri_solve_steps(n_mat, [rhs], eye_f, blockdiag)
            p_mat = jnp.where(incl, qk * gamma, 0.0)
            p_uw = _mm(p_mat, uw)
            k_dec = kc * jnp.exp(g_last - g_col)
            k_uw = _mm_tn(k_dec, uw)
            q_t = qc * eg - p_uw[:, GDN_DV:]
            stores.append((d, n, jnp.concatenate([q_t, -k_uw[:, GDN_DV:]], axis=0).astype(BF16),
                           jnp.concatenate([p_uw[:, :GDN_DV], k_uw[:, :GDN_DV]], axis=0)))

        _run_interleaved(
            chain(n, qc, kc, vc, g_rows[d], be_rows[d], kk, qk, d, *dirs[d])
            for (n, qc, kc, vc, g_rows, be_rows), (kk, qk) in zip(loaded, grams) for d in (0, 1))
        for d, n, lhs, add in stores:
            lhs_s[d, n] = lhs
            add_s[d, n] = add
        return carry

    lax.fori_loop(0, n_tot // GDN_UNROLL, prep, 0)

    s_s[...] = jnp.zeros_like(s_s)
    orders = [_chunk_order(d, n_ctx, n_tot) for d in (0, 1)]
    o_refs = (o_ref.at[0], ob_s)

    def step(j, carry):
        loaded = []
        for d, (_, _, last) in enumerate(dirs):
            n = orders[d](j)
            loaded.append((n, s_s[d], lhs_s[d, n], add_s[d, n], g_s[d, pl.ds(n, 1), last:last + 1]))
        results = []
        for n, s, lhs, add, g_last in loaded:
            z = jnp.dot(lhs, s.astype(BF16), preferred_element_type=F32) + add
            results.append((n, jnp.exp(g_last) * s + z[CHUNK:], z[:CHUNK]))
        for d, (n, s_new, o) in enumerate(results):
            s_s[d] = s_new
            o_refs[d][pl.ds(pl.multiple_of(n * CHUNK, CHUNK), CHUNK), :] = o
        return carry

    lax.fori_loop(0, n_tot, step, 0)
    o_ref[0] += ob_s[...]


def gdn_scan(q, k, v, log_alpha, beta):
    b, t, _ = q.shape
    n_tot = t // CHUNK
    ncp = log_alpha.shape[3]
    blk = pl.BlockSpec((1, t, GDN_DK), lambda bi, hi: (bi, 0, hi))
    gate_blk = pl.BlockSpec((2, 1, 1, ncp, CHUNK), lambda bi, hi: (0, bi, hi, 0, 0))
    return pl.pallas_call(
        functools.partial(_gdn_kernel, n_ctx=CTX_LEN // CHUNK, n_tot=n_tot),
        grid=(b, GDN_HEADS),
        in_specs=[blk, blk, blk, gate_blk, gate_blk],
        out_specs=blk,
        out_shape=jax.ShapeDtypeStruct((b, t, GDN_HEADS * GDN_DV), F32),
        scratch_shapes=[pltpu.VMEM((2, ncp, CHUNK), F32),
                        pltpu.VMEM((2, n_tot, CHUNK + GDN_DK, GDN_DV), BF16),
                        pltpu.VMEM((2, n_tot, CHUNK + GDN_DK, GDN_DV), F32),
                        pltpu.VMEM((2, GDN_DK, GDN_DV), F32),
                        pltpu.VMEM((t, GDN_DV), F32)],
        compiler_params=_cparams(("parallel", "parallel")),
        name="gdn_scan",
    )(q, k, v, log_alpha, beta)


def _mlstm_kernel(q_ref, k_ref, v_ref, ip_ref, lf_ref, o_ref, b_s, c_s, n_s, m_s, ob_s, *, n_ctx, n_tot):
    ii, jj, _, _ = _masks(0)
    eye = ii == jj
    hps = MLSTM_HEADS_PER_STEP
    dirs = []
    for d in (0, 1):
        _, _, incl, _ = _masks(d)
        cum = (ii <= jj).astype(F32) if d == 0 else (ii >= jj).astype(F32)
        for hh in range(hps):
            b_s[d, hh] = _cumsum_rows(lf_ref[d, 0, hh], cum)
        dirs.append((incl, CHUNK - 1 if d == 0 else 0))
    c_s[...] = jnp.zeros_like(c_s)
    n_s[...] = jnp.zeros_like(n_s)
    m_s[...] = jnp.zeros_like(m_s)
    orders = [_chunk_order(d, n_ctx, n_tot) for d in (0, 1)]
    o_refs = (o_ref.at[0], ob_s)

    def step(j, carry):
        loaded = {}
        for d in (0, 1):
            n = orders[d](j)
            rows = pl.ds(pl.multiple_of(n * CHUNK, CHUNK), CHUNK)
            for hh in range(hps):
                qsl = slice(hh * MLSTM_DQK, (hh + 1) * MLSTM_DQK)
                vsl = slice(hh * MLSTM_DV, (hh + 1) * MLSTM_DV)
                loaded[(d, hh)] = (q_ref[0, rows, qsl], k_ref[0, rows, qsl], v_ref[0, rows, vsl],
                                   b_s[d, hh, pl.ds(n, 1), :], ip_ref[d, 0, hh, pl.ds(n, 1), :],
                                   c_s[d, hh], n_s[d, hh], m_s[d, hh], rows, vsl)
        results = {}

        def chain(key, qc, kc, vc, b_row, ip_row, c_st, n_st, m_st, incl, last):
            b_col = _row_to_col(b_row, eye)
            b_last = b_row[:, last:last + 1]
            log_d = jnp.where(incl, b_col - b_row + ip_row, NEG)
            m_intra = jnp.max(log_d, axis=1, keepdims=True)
            qk = _mm_nt(qc, kc)
            qc_st = _mm(qc, c_st)
            log_end = b_last - b_row + ip_row
            m_end = jnp.max(log_end, axis=1, keepdims=True)
            m_row = jnp.maximum(b_col + m_st, m_intra)
            w_state = jnp.exp(b_col + m_st - m_row)
            m_new = jnp.maximum(b_last + m_st, m_end)
            decay = jnp.exp(b_last + m_st - m_new)
            k_w = kc * _row_to_col(jnp.exp(log_end - m_new), eye)
            c_new = decay * c_st + _mm_tn(k_w, vc)
            n_new = decay * n_st + jnp.sum(k_w, axis=0, keepdims=True)
            yield
            w_intra = jnp.exp(log_d - m_row) * qk
            num = w_state * qc_st + _mm(w_intra, vc)
            den = (w_state * jnp.sum(qc * n_st, axis=1, keepdims=True)
                   + jnp.sum(w_intra, axis=1, keepdims=True))
            yield
            results[key] = (num / jnp.maximum(jnp.abs(den), jnp.exp(-m_row)), c_new, n_new, m_new)

        _run_interleaved(chain(key, *vals[:8], *dirs[key[0]]) for key, vals in loaded.items())
        for (d, hh), (h, c_new, n_new, m_new) in results.items():
            rows, vsl = loaded[(d, hh)][8:]
            c_s[d, hh] = c_new
            n_s[d, hh] = n_new
            m_s[d, hh] = m_new
            o_refs[d][rows, vsl] = h
        return carry

    lax.fori_loop(0, n_tot, step, 0)
    o_ref[0] += ob_s[...]


def mlstm_scan(q, k, v, i_pre, log_f):
    b, t, _ = q.shape
    n_tot = t // CHUNK
    ncp = i_pre.shape[3]
    hps = MLSTM_HEADS_PER_STEP
    qk_blk = pl.BlockSpec((1, t, hps * MLSTM_DQK), lambda bi, hi: (bi, 0, hi))
    v_blk = pl.BlockSpec((1, t, hps * MLSTM_DV), lambda bi, hi: (bi, 0, hi))
    gate_blk = pl.BlockSpec((2, 1, hps, ncp, CHUNK), lambda bi, hi: (0, bi, hi, 0, 0))
    return pl.pallas_call(
        functools.partial(_mlstm_kernel, n_ctx=CTX_LEN // CHUNK, n_tot=n_tot),
        grid=(b, MLSTM_HEADS // hps),
        in_specs=[qk_blk, qk_blk, v_blk, gate_blk, gate_blk],
        out_specs=v_blk,
        out_shape=jax.ShapeDtypeStruct((b, t, MLSTM_HEADS * MLSTM_DV), F32),
        scratch_shapes=[pltpu.VMEM((2, hps, ncp, CHUNK), F32),
                        pltpu.VMEM((2, hps, MLSTM_DQK, MLSTM_DV), F32),
                        pltpu.VMEM((2, hps, 1, MLSTM_DQK), F32),
                        pltpu.VMEM((2, hps, 1, 1), F32),
                        pltpu.VMEM((t, hps * MLSTM_DV), F32)],
        compiler_params=_cparams(("parallel", "parallel")),
        name="mlstm_scan",
    )(q, k, v, i_pre, log_f)


def _rwkv_kernel(r_ref, v_ref, kk_ref, lw0_ref, lw1_ref, kd0_ref, kd1_ref, bv0_ref, bv1_ref, o_ref,
                 lhs_s, add_s, gl_s, s_s, ob_s, *, n_ctx, n_tot):
    dir_refs = ((lw0_ref, kd0_ref, bv0_ref), (lw1_ref, kd1_ref, bv1_ref))
    ii, jj, _, _ = _masks(0)
    eye = ii == jj
    eye_f = eye.astype(F32)
    blockdiag = (ii // 16) == (jj // 16)
    hd = RWKV_HEAD
    dirs = []
    for d in (0, 1):
        _, _, incl, strict = _masks(d)
        dirs.append((incl, strict, CHUNK - 1 if d == 0 else 0))
    zero = jnp.zeros((hd, hd), F32)

    def prep(j, carry):
        loaded = []
        for uu in range(RWKV_UNROLL):
            n = j * RWKV_UNROLL + uu
            rows = pl.ds(pl.multiple_of(n * CHUNK, CHUNK), CHUNK)
            loaded.append((n, r_ref[0, rows, :], v_ref[0, rows, :], kk_ref[0, rows, :],
                           [tuple(ref[0, rows, :] for ref in dir_refs[d]) for d in (0, 1)]))
        parts = {}

        def chain(key, r, v, kk, lw, kd, bv, gcs, e_end, incl, strict):
            e_neg = jnp.exp(-gcs)
            a_h = -kk * jnp.exp(gcs - lw)
            r_h = r * jnp.exp(gcs)
            b_h = bv * e_neg
            k_h = kd * e_neg
            a_ab = jnp.where(strict, _mm_nt(a_h, b_h), 0.0)
            a_ak = jnp.where(strict, _mm_nt(a_h, k_h), 0.0)
            a_rb = jnp.where(incl, _mm_nt(r_h, b_h), 0.0)
            a_rk = jnp.where(incl, _mm_nt(r_h, k_h), 0.0)
            yield
            av = _mm(a_ak, v)
            ta, tav = yield from _tri_solve_steps(a_ab, [a_h, av], eye_f, blockdiag)
            b_g = bv * e_end
            parts[key] = (r_h + _mm(a_rb, ta), _mm_tn(b_g, ta),
                          _mm(a_rb, tav) + _mm(a_rk, v), _mm_tn(b_g, tav) + _mm_tn(kd * e_end, v))

        gens, gls = [], {}
        for uu, (_, r2, v2, kk2, per_dir) in enumerate(loaded):
            for d, (incl, strict, last) in enumerate(dirs):
                lw2, kd2, bv2 = per_dir[d]
                gcs2 = _cumsum_cols(incl.astype(F32), lw2)
                gl_row2 = gcs2[last:last + 1, :]
                e_end2 = jnp.exp(gl_row2 - gcs2)
                gls[(uu, d)] = jnp.exp(gl_row2)
                for hh in range(2):
                    sl = slice(hh * hd, (hh + 1) * hd)
                    gens.append(chain((uu, d, hh), *(a[:, sl] for a in (r2, v2, kk2, lw2, kd2, bv2, gcs2, e_end2)),
                                      incl, strict))
        _run_interleaved(gens)
        for uu, (n, _, _, _, _) in enumerate(loaded):
            for d in (0, 1):
                (rt0, mx0, yc0, kv0), (rt1, mx1, yc1, kv1) = parts[(uu, d, 0)], parts[(uu, d, 1)]
                stack = lambda t0, t1, b0, b1: jnp.concatenate(
                    [jnp.concatenate([t0, t1], axis=1), jnp.concatenate([b0, zero], axis=1),
                     jnp.concatenate([zero, b1], axis=1)], axis=0)
                lhs_s[d, n] = stack(rt0, rt1, mx0, mx1).astype(BF16)
                add_s[d, n] = stack(yc0, yc1, kv0, kv1)
                gl_s[d, pl.ds(n, 1), :] = gls[(uu, d)]
        return carry

    lax.fori_loop(0, n_tot // RWKV_UNROLL, prep, 0)

    s_s[...] = jnp.zeros_like(s_s)
    orders = [_chunk_order(d, n_ctx, n_tot) for d in (0, 1)]
    eye2 = _iota2(2 * hd, 2 * hd, 0) == _iota2(2 * hd, 2 * hd, 1)
    o_refs = (o_ref.at[0], ob_s)

    def step(j, carry):
        loaded = []
        for d in (0, 1):
            n = orders[d](j)
            loaded.append((n, s_s[d], lhs_s[d, n], add_s[d, n], gl_s[d, pl.ds(n, 1), :]))
        results = []
        for n, s, lhs, add, gl in loaded:
            z = jnp.dot(lhs, s.astype(BF16), preferred_element_type=F32) + add
            results.append((n, _row_to_col(gl, eye2) * s + z[CHUNK:], z[:CHUNK]))
        for d, (n, s_new, y) in enumerate(results):
            s_s[d] = s_new
            o_refs[d][pl.ds(pl.multiple_of(n * CHUNK, CHUNK), CHUNK), :] = y
        return carry

    lax.fori_loop(0, n_tot, step, 0)
    o_ref[0] += ob_s[...]


def rwkv_scan(r, v, kk, log_w, k_dir, b_dir):
    b, t, _ = r.shape
    n_tot = t // CHUNK
    blk = pl.BlockSpec((1, t, 128), lambda bi, hi: (bi, 0, hi))
    return pl.pallas_call(
        functools.partial(_rwkv_kernel, n_ctx=CTX_LEN // CHUNK, n_tot=n_tot),
        grid=(b, RWKV_HEADS // 2),
        in_specs=[blk] * 9,
        out_specs=blk,
        out_shape=jax.ShapeDtypeStruct((b, t, D_MODEL), F32),
        scratch_shapes=[pltpu.VMEM((2, n_tot, CHUNK + 2 * RWKV_HEAD, 2 * RWKV_HEAD), BF16),
                        pltpu.VMEM((2, n_tot, CHUNK + 2 * RWKV_HEAD, 2 * RWKV_HEAD), F32),
                        pltpu.VMEM((2, 8 * ((n_tot + 7) // 8), 2 * RWKV_HEAD), F32),
                        pltpu.VMEM((2, 2 * RWKV_HEAD, 2 * RWKV_HEAD), F32),
                        pltpu.VMEM((t, 2 * RWKV_HEAD), F32)],
        compiler_params=_cparams(("parallel", "parallel")),
        name="rwkv_scan",
    )(r, v, kk, *log_w, *k_dir, *b_dir)


def _na_kernel(q_ref, k_ref, v_ref, bias_ref, o_ref, *, rows):
    scale = NA_DH ** -0.5
    slab = NA_WIN_ROWS * GRID_W
    lane = _iota2(1, 2 * NA_DH, 1)
    head_masks = (lane < NA_DH, lane >= NA_DH)
    kc2 = k_ref[0, 0:CTX_LEN, :].astype(BF16)
    vc2 = v_ref[0, 0:CTX_LEN, :].astype(BF16)

    def body(j, carry):
        loaded = []
        for uu in range(NA_ROWS_PER_STEP):
            r = j * NA_ROWS_PER_STEP + uu
            r0 = jnp.clip(r - NA_WIN_ROWS // 2, 0, rows - NA_WIN_ROWS)
            dr0 = r0 - r + NA_WIN_ROWS - 1
            krows = pl.ds(pl.multiple_of(CTX_LEN + r0 * GRID_W, GRID_W), slab)
            loaded.append((r, q_ref[0, pl.ds(pl.multiple_of(CTX_LEN + r * GRID_W, GRID_W), GRID_W), :],
                           k_ref[0, krows, :].astype(BF16), v_ref[0, krows, :].astype(BF16),
                           [bias_ref[hh, pl.ds(dr0, 1)][0] for hh in range(2)]))
        outs = {}

        def chain(key, q2, ks2, vs2, bias, mask):
            qh = jnp.where(mask, q2, 0.0)
            s_lat = _mm_nt(qh, ks2) * scale + bias
            s_ctx = _mm_nt(qh, kc2) * scale
            yield
            m = jnp.maximum(jnp.max(s_lat, axis=1, keepdims=True), jnp.max(s_ctx, axis=1, keepdims=True))
            p_lat = jnp.exp(s_lat - m)
            p_ctx = jnp.exp(s_ctx - m)
            l = jnp.sum(p_lat, axis=1, keepdims=True) + jnp.sum(p_ctx, axis=1, keepdims=True)
            outs[key] = (_mm(p_lat, vs2) + _mm(p_ctx, vc2)) / l
            yield

        _run_interleaved(chain((uu, hh), q2, ks2, vs2, biases[hh], head_masks[hh])
                         for uu, (_, q2, ks2, vs2, biases) in enumerate(loaded) for hh in range(2))
        for uu, (r, _, _, _, _) in enumerate(loaded):
            o_ref[0, pl.ds(pl.multiple_of(r * GRID_W, GRID_W), GRID_W), :] = jnp.where(
                head_masks[0], outs[(uu, 0)], outs[(uu, 1)])
        return carry

    lax.fori_loop(0, rows // NA_ROWS_PER_STEP, body, 0)


def _na_bias_table(rpb):
    cols = np.arange(GRID_W)
    win_c0 = np.clip(cols - NA_WIN_COLS // 2, 0, GRID_W - NA_WIN_COLS)
    kc = np.arange(GRID_W)
    in_win = (kc[None, :] >= win_c0[:, None]) & (kc[None, :] < win_c0[:, None] + NA_WIN_COLS)
    dc = np.clip(kc[None, :] - cols[:, None] + NA_WIN_COLS - 1, 0, 2 * NA_WIN_COLS - 2)
    dr = np.arange(NA_WIN_ROWS)[:, None] + np.arange(NA_WIN_ROWS)[None, :]
    tab = rpb.astype(F32)[:, dr][:, :, :, dc]
    tab = jnp.where(in_win[None, None, None], tab, NEG)
    tab = tab.transpose(0, 1, 3, 2, 4)
    return tab.reshape(NA_HEADS, NA_WIN_ROWS, GRID_W, NA_WIN_ROWS * GRID_W)


def na_attention(z, bias_tab):
    b, t, _ = z.shape
    t_lat = t - CTX_LEN
    n_pairs = NA_HEADS // 2
    return pl.pallas_call(
        functools.partial(_na_kernel, rows=t_lat // GRID_W),
        grid=(n_pairs, b),
        in_specs=[pl.BlockSpec((1, t, 128), lambda hi, bi: (bi, 0, hi)),
                  pl.BlockSpec((1, t, 128), lambda hi, bi: (bi, 0, n_pairs + hi)),
                  pl.BlockSpec((1, t, 128), lambda hi, bi: (bi, 0, 2 * n_pairs + hi)),
                  pl.BlockSpec((2, NA_WIN_ROWS, GRID_W, NA_WIN_ROWS * GRID_W), lambda hi, bi: (hi, 0, 0, 0))],
        out_specs=pl.BlockSpec((1, t_lat, 128), lambda hi, bi: (bi, 0, hi)),
        out_shape=jax.ShapeDtypeStruct((b, t_lat, D_MODEL), F32),
        compiler_params=_cparams(("parallel", "parallel")),
        name="na_attention",
    )(z, z, z, bias_tab)


def _router_kernel(x_ref, modl_ref, modc_ref, rw_ref, rb_ref, hb_ref, idx_ref, wt_ref, *, tile_off):
    is_ctx = (pl.program_id(1) + tile_off) * TOKEN_TILE < CTX_LEN
    sh = _mod_rows(modl_ref, modc_ref, is_ctx, 3)
    sc = _mod_rows(modl_ref, modc_ref, is_ctx, 4)
    h = x_ref[0] * (1.0 + sc) + sh
    hb_ref[0] = h.astype(BF16)
    logits = _mmf(h, rw_ref[...])
    mx = jnp.max(logits, axis=1, keepdims=True)
    ex = jnp.exp(logits - mx)
    probs = ex / jnp.sum(ex, axis=1, keepdims=True)
    sel = probs + rb_ref[...]
    tm = sel.shape[0]
    e_id = _iota2(tm, N_EXPERTS, 1)
    per_group = N_EXPERTS // N_GROUPS
    g_id = e_id // per_group

    def top1(vals):
        m1 = jnp.max(vals, axis=1, keepdims=True)
        i1 = jnp.min(jnp.where(vals == m1, e_id, N_EXPERTS), axis=1, keepdims=True)
        return m1, i1

    best_score, best = None, None
    for g in range(N_GROUPS):
        vals = jnp.where(g_id == g, sel, NEG)
        m1, i1 = top1(vals)
        m2, _ = top1(jnp.where(e_id == i1, NEG, vals))
        score = m1 + m2
        if g == 0:
            best_score, best = score, jnp.zeros_like(i1)
        else:
            better = score > best_score
            best = jnp.where(better, g, best)
            best_score = jnp.where(better, score, best_score)
    vals = jnp.where(g_id == best, sel, NEG)
    _, i1 = top1(vals)
    _, i2 = top1(jnp.where(e_id == i1, NEG, vals))
    p1 = jnp.sum(jnp.where(e_id == i1, probs, 0.0), axis=1, keepdims=True)
    p2 = jnp.sum(jnp.where(e_id == i2, probs, 0.0), axis=1, keepdims=True)
    tot = p1 + p2
    slot = _iota2(tm, 2, 1)
    idx_ref[0] = jnp.where(slot == 0, i1, i2)
    wt_ref[0] = jnp.where(slot == 0, p1 / tot, p2 / tot)


def moe_route(x, mods, router_w, router_b, *, tile_off):
    b, t_out, _ = x.shape
    modl, modc = mods
    return pl.pallas_call(
        functools.partial(_router_kernel, tile_off=tile_off),
        grid=(b, t_out // TOKEN_TILE),
        in_specs=[pl.BlockSpec((1, TOKEN_TILE, D_MODEL), lambda bi, ti: (bi, ti, 0)),
                  pl.BlockSpec((1, 1, 6 * D_MODEL), lambda bi, ti: (bi, 0, 0)),
                  pl.BlockSpec((1, 1, 6 * D_MODEL), lambda bi, ti: (0, 0, 0)),
                  pl.BlockSpec((D_MODEL, N_EXPERTS), lambda bi, ti: (0, 0)),
                  pl.BlockSpec((1, N_EXPERTS), lambda bi, ti: (0, 0))],
        out_specs=[pl.BlockSpec((1, TOKEN_TILE, D_MODEL), lambda bi, ti: (bi, ti, 0)),
                   pl.BlockSpec((1, TOKEN_TILE, 2), lambda bi, ti: (bi, ti, 0)),
                   pl.BlockSpec((1, TOKEN_TILE, 2), lambda bi, ti: (bi, ti, 0))],
        out_shape=[jax.ShapeDtypeStruct((b, t_out, D_MODEL), BF16),
                   jax.ShapeDtypeStruct((b, t_out, 2), jnp.int32),
                   jax.ShapeDtypeStruct((b, t_out, 2), F32)],
        compiler_params=_cparams(("parallel", "parallel")),
        name="moe_route",
    )(x, modl, modc, router_w, router_b.reshape(1, -1))


def _ffn_kernel(te_ref, nt_ref, x_ref, w1_ref, w3_ref, w2_ref, o_ref):
    @pl.when(pl.program_id(0) < nt_ref[0])
    def _():
        xb = x_ref[...]
        h1 = jnp.dot(xb, w1_ref[0, 0].astype(BF16), preferred_element_type=F32)
        h3 = jnp.dot(xb, w3_ref[0, 0].astype(BF16), preferred_element_type=F32)
        hid = (_silu(h1) * h3).astype(BF16)
        o_ref[...] = jnp.dot(hid, w2_ref[0, 0].astype(BF16), preferred_element_type=F32)

    @pl.when(pl.program_id(0) >= nt_ref[0])
    def _():
        o_ref[...] = jnp.zeros_like(o_ref)


def expert_ffn(xs, tile_expert, n_tiles_used, w1, w3, w2, layer):
    p = xs.shape[0]
    grid_spec = pltpu.PrefetchScalarGridSpec(
        num_scalar_prefetch=2,
        grid=(p // FFN_TILE,),
        in_specs=[pl.BlockSpec((FFN_TILE, D_MODEL), lambda i, te, nt: (i, 0)),
                  pl.BlockSpec((1, 1, D_MODEL, D_EXPERT), lambda i, te, nt: (layer, te[i], 0, 0)),
                  pl.BlockSpec((1, 1, D_MODEL, D_EXPERT), lambda i, te, nt: (layer, te[i], 0, 0)),
                  pl.BlockSpec((1, 1, D_EXPERT, D_MODEL), lambda i, te, nt: (layer, te[i], 0, 0))],
        out_specs=pl.BlockSpec((FFN_TILE, D_MODEL), lambda i, te, nt: (i, 0)),
    )
    return pl.pallas_call(
        _ffn_kernel,
        grid_spec=grid_spec,
        out_shape=jax.ShapeDtypeStruct((p, D_MODEL), F32),
        compiler_params=_cparams(("arbitrary",)),
        name="expert_ffn",
    )(tile_expert, n_tiles_used, xs, w1, w3, w2)


def _combine_ln_kernel(x_ref, y0_ref, y1_ref, wt_ref, modl_ref, modc_ref, g_ref, b_ref, o_ref, *, tile_off):
    is_ctx = (pl.program_id(1) + tile_off) * TOKEN_TILE < CTX_LEN
    gate = _mod_rows(modl_ref, modc_ref, is_ctx, 5)
    wt = wt_ref[0]
    f = wt[:, 0:1] * y0_ref[0] + wt[:, 1:2] * y1_ref[0]
    r = ALPHA * x_ref[0] + gate * f
    o_ref[0] = _layer_norm_rows(r, g_ref[...], b_ref[...])


def combine_ln(x, y0, y1, wt, mods, ln_g, ln_b, *, tile_off):
    b, t_out, _ = y0.shape
    modl, modc = mods
    tok = lambda bi, ti: (bi, ti, 0)
    return pl.pallas_call(
        functools.partial(_combine_ln_kernel, tile_off=tile_off),
        grid=(b, t_out // TOKEN_TILE),
        in_specs=[pl.BlockSpec((1, TOKEN_TILE, D_MODEL), tok),
                  pl.BlockSpec((1, TOKEN_TILE, D_MODEL), tok),
                  pl.BlockSpec((1, TOKEN_TILE, D_MODEL), tok),
                  pl.BlockSpec((1, TOKEN_TILE, 2), tok),
                  pl.BlockSpec((1, 1, 6 * D_MODEL), lambda bi, ti: (bi, 0, 0)),
                  pl.BlockSpec((1, 1, 6 * D_MODEL), lambda bi, ti: (0, 0, 0)),
                  pl.BlockSpec((1, D_MODEL), lambda bi, ti: (0, 0)),
                  pl.BlockSpec((1, D_MODEL), lambda bi, ti: (0, 0))],
        out_specs=pl.BlockSpec((1, TOKEN_TILE, D_MODEL), tok),
        out_shape=jax.ShapeDtypeStruct((b, t_out, D_MODEL), F32),
        compiler_params=_cparams(("parallel", "parallel")),
        name="moe_combine_ln",
    )(x, y0, y1, wt, modl, modc, ln_g.reshape(1, -1), ln_b.reshape(1, -1))


def moe_layer(x, mods, router_w, router_b, w1, w3, w2, layer, ln_g, ln_b, *, tile_off):
    hb, idx, wt = moe_route(x, mods, router_w, router_b, tile_off=tile_off)
    b, t, _ = hb.shape
    n_tok = b * t
    n_pair = 2 * n_tok
    e_flat = idx.reshape(n_pair)
    onehot = (e_flat[:, None] == jnp.arange(N_EXPERTS)[None, :]).astype(jnp.int32)
    csum = jnp.cumsum(onehot, axis=0)
    counts = csum[-1]
    rank = jnp.sum((csum - onehot) * onehot, axis=1)
    padded = ((counts + FFN_TILE - 1) // FFN_TILE) * FFN_TILE
    ends = jnp.cumsum(padded)
    offs = ends - padded
    pos = offs[e_flat] + rank
    n_rows = n_pair + N_EXPERTS * FFN_TILE
    n_tiles = n_rows // FFN_TILE
    src = jnp.zeros((n_rows,), jnp.int32).at[pos].set(jnp.arange(n_pair, dtype=jnp.int32) // 2)
    tile_start = jnp.arange(n_tiles, dtype=jnp.int32) * FFN_TILE
    tile_expert = jnp.minimum(jnp.searchsorted(ends, tile_start, side="right"), N_EXPERTS - 1).astype(jnp.int32)
    n_used = (ends[-1] // FFN_TILE).astype(jnp.int32).reshape(1)
    xs = jnp.take(hb.reshape(n_tok, D_MODEL), src, axis=0)
    ys = expert_ffn(xs, tile_expert, n_used, w1, w3, w2, layer)
    pos2 = pos.reshape(n_tok, 2)
    y0 = jnp.take(ys, pos2[:, 0], axis=0).reshape(b, t, D_MODEL)
    y1 = jnp.take(ys, pos2[:, 1], axis=0).reshape(b, t, D_MODEL)
    return combine_ln(x, y0, y1, wt, mods, ln_g, ln_b, tile_off=tile_off)


def _seg_apply(fn, a):
    return jnp.concatenate([fn(a[:, :CTX_LEN]), fn(a[:, CTX_LEN:])], axis=1)


def _rope_tables(t_lat, dh):
    quarter = dh // 4
    pos = jnp.arange(t_lat)
    inv_freq = ROPE_BASE ** (-jnp.arange(quarter, dtype=F32) / quarter)
    ang_r = (pos // GRID_W).astype(F32)[:, None] * inv_freq[None, :]
    ang_c = (pos % GRID_W).astype(F32)[:, None] * inv_freq[None, :]
    return jnp.cos(ang_r), jnp.sin(ang_r), jnp.cos(ang_c), jnp.sin(ang_c)


def _rope_lat(a, n_heads):
    b, t, hd = a.shape
    dh = hd // n_heads
    q4 = dh // 4
    lat = a[:, CTX_LEN:].reshape(b, t - CTX_LEN, n_heads, dh)
    cr, sr, cc, sc = (u[None, :, None, :] for u in _rope_tables(t - CTX_LEN, dh))
    x1, x2, x3, x4 = lat[..., :q4], lat[..., q4:2 * q4], lat[..., 2 * q4:3 * q4], lat[..., 3 * q4:]
    out = jnp.concatenate([x1 * cr - x2 * sr, x1 * sr + x2 * cr, x3 * cc - x4 * sc, x3 * sc + x4 * cc], axis=-1)
    return jnp.concatenate([a[:, :CTX_LEN], out.reshape(b, t - CTX_LEN, hd)], axis=1)


def _head_l2norm(a, n_heads, eps=1e-6):
    b, t, hd = a.shape
    ah = a.reshape(b, t, n_heads, hd // n_heads)
    return (ah * lax.rsqrt(jnp.sum(ah * ah, axis=-1, keepdims=True) + eps)).reshape(b, t, hd)


def _gate_rows(g):
    b, t = g.shape[:2]
    n = t // CHUNK
    ncp = 8 * ((n + 7) // 8)
    g = g.transpose(2, 3, 0, 4, 1).reshape(2, 2, b, g.shape[-1], n, CHUNK)
    return jnp.pad(g, ((0, 0),) * 4 + ((0, ncp - n), (0, 0)))


def _dwconv3(a, w):
    ap = jnp.pad(a, ((0, 0), (1, 1), (0, 0)))
    return ap[:, :-2] * w[0] + ap[:, 1:-1] * w[1] + ap[:, 2:] * w[2]


def gdn_layer(x, mods, w_in, conv_w, a_log, dt_bias, norm_g, w_out):
    wq = GDN_HEADS * GDN_DK
    n_main = 4 * wq
    z, ab = linear(x, w_in[:, :n_main].astype(BF16), mods=mods, modulate=(0, 1), w_small=w_in[:, n_main:])
    qkv = jax.nn.silu(_seg_apply(lambda s: _dwconv3(s, conv_w), z[..., :3 * wq]))
    q = _rope_lat(_head_l2norm(qkv[..., :wq], GDN_HEADS), GDN_HEADS) * GDN_DK ** -0.5
    k = _rope_lat(_head_l2norm(qkv[..., wq:2 * wq], GDN_HEADS), GDN_HEADS)
    v = qkv[..., 2 * wq:]
    gate = z[..., 3 * wq:]
    b, t, _ = x.shape
    ab = ab.reshape(b, t, 2, 2, GDN_HEADS)
    ab = ab.at[:, :, :, 0].add(dt_bias[None, None])
    rows = _gate_rows(ab)
    log_alpha = -jnp.exp(a_log)[:, None, :, None, None] * jax.nn.softplus(rows[:, 0])
    beta = jax.nn.sigmoid(rows[:, 1])
    valid = (jnp.arange(rows.shape[4]) < t // CHUNK)[:, None]
    log_alpha = jnp.where(valid, log_alpha, 0.0)
    o = gdn_scan(q, k, v, log_alpha, beta)
    oh = o.reshape(b, t, GDN_HEADS, GDN_DV)
    oh = oh * lax.rsqrt(jnp.mean(oh * oh, axis=-1, keepdims=True) + 1e-6) * norm_g
    y = oh.reshape(b, t, -1) * jax.nn.silu(gate)
    return y, w_out


def mlstm_layer(x, mods, w_in, gate_b, norm_g, w_out):
    wq = MLSTM_HEADS * MLSTM_DQK
    wv = MLSTM_HEADS * MLSTM_DV
    n_main = 2 * wq + 2 * wv
    z, gt = linear(x, w_in[:, :n_main].astype(BF16), mods=mods, modulate=(0, 1), w_small=w_in[:, n_main:])
    q = _rope_lat(z[..., :wq], MLSTM_HEADS) * MLSTM_DQK ** -0.5
    k = _rope_lat(z[..., wq:2 * wq], MLSTM_HEADS)
    v = z[..., 2 * wq:2 * wq + wv]
    o_gate = z[..., 2 * wq + wv:]
    b, t, _ = x.shape
    gt = gt.reshape(b, t, 2, 2, MLSTM_HEADS) + gate_b[None, None]
    rows = _gate_rows(gt)
    valid = (jnp.arange(rows.shape[4]) < t // CHUNK)[:, None]
    i_pre = rows[:, 0]
    log_f = jnp.where(valid, jax.nn.log_sigmoid(rows[:, 1]), 0.0)
    h = mlstm_scan(q, k, v, i_pre, log_f)
    hh = h.reshape(b, t, MLSTM_HEADS, MLSTM_DV)
    mu = jnp.mean(hh, axis=-1, keepdims=True)
    var = jnp.mean(jnp.square(hh - mu), axis=-1, keepdims=True)
    hn = ((hh - mu) * lax.rsqrt(var + 1e-6)).reshape(b, t, -1)
    y = hn * norm_g * jax.nn.sigmoid(o_gate)
    return y, w_out


def _modulated(x, mods, sh_idx, sc_idx):
    modl, modc = mods
    d = D_MODEL
    is_ctx = (jnp.arange(x.shape[1]) < CTX_LEN)[None, :, None]
    sh_c, sh_l = modc[:, :, sh_idx * d:(sh_idx + 1) * d], modl[:, :, sh_idx * d:(sh_idx + 1) * d]
    sc_c, sc_l = modc[:, :, sc_idx * d:(sc_idx + 1) * d], modl[:, :, sc_idx * d:(sc_idx + 1) * d]
    return x * (1.0 + jnp.where(is_ctx, sc_c, sc_l)) + jnp.where(is_ctx, sh_c, sh_l)


def _token_shift(a):
    ap = jnp.pad(a, ((0, 0), (1, 1), (0, 0)))
    return 0.5 * (ap[:, :-2] + ap[:, 2:])


def rwkv_layer(x, mods, mu, w_rkv, w0, w1, w2, a0, a1, a2, g1, g2, k_k, k_a, r_k, lnx_g, lnx_b, w_out):
    b, t, d = x.shape
    nh, hd = RWKV_HEADS, RWKV_HEAD
    h = _modulated(x, mods, 0, 1)
    dx = _seg_apply(_token_shift, h) - h
    xr, xw, xk, xv, xa, xg = [h + dx * mu[j] for j in range(6)]
    r = linear(xr, w_rkv[0].astype(BF16))
    k = linear(xk, w_rkv[1].astype(BF16))
    v = linear(xv, w_rkv[2].astype(BF16))
    g = linear(linear(xg, g1.astype(BF16)), g2.astype(BF16), act="sigmoid")
    heads = lambda a: a.reshape(b, t, nh, hd)
    kk = _head_l2norm(k * k_k, nh)
    lws, kds, bvs = [], [], []
    for dr in range(2):
        lo = linear(linear(xw, w1[dr].astype(BF16)), w2[dr].astype(BF16), act="tanh")
        w_raw = -jax.nn.softplus(-(w0[dr] + lo)) - 0.5
        a = jax.nn.sigmoid(a0[dr] + linear(linear(xa, a1[dr].astype(BF16)), a2[dr].astype(BF16)))
        lws.append(-jnp.exp(w_raw))
        kds.append(k * (1.0 + (a - 1.0) * k_a))
        bvs.append(kk * a)
    y = rwkv_scan(r, v, kk, lws, kds, bvs)
    yh = heads(y)
    m = jnp.mean(yh, axis=-1, keepdims=True)
    var = jnp.mean(jnp.square(yh - m), axis=-1, keepdims=True)
    yn = ((yh - m) * lax.rsqrt(var + RWKV_GN_EPS)).reshape(b, t, d)
    bonus = sum(jnp.sum(heads(r) * heads(kd) * r_k, axis=-1, keepdims=True) * heads(v) for kd in kds)
    yo = (yn * lnx_g + lnx_b + bonus.reshape(b, t, d)) * g
    return yo, w_out


def kernel(x, c, ctx, c_ctx, ada_w, ada_b, ln_g, ln_b, router_w, router_b, moe_w1, moe_w3, moe_w2, gdn_w_in, gdn_conv, gdn_a_log, gdn_dt_bias, gdn_norm_g, gdn_w_out, mlstm_w_in, mlstm_gate_b, mlstm_norm_g, mlstm_w_out, rwkv_mu, rwkv_w_rkv, rwkv_w0, rwkv_w1, rwkv_w2, rwkv_a0, rwkv_a1, rwkv_a2, rwkv_g1, rwkv_g2, rwkv_k_k, rwkv_k_a, rwkv_r_k, rwkv_lnx_g, rwkv_lnx_b, rwkv_w_out, na_w_in, na_rpb, na_w_out):
    b = x.shape[0]
    ctx_tiles = CTX_LEN // TOKEN_TILE
    mod_all = modulation_all(c, c_ctx, ada_w, ada_b)
    xs = jnp.concatenate([ctx, x], axis=1)
    assert DEPTH == 4
    for i in range(DEPTH):
        mods = (mod_all[i, :b, None, :], mod_all[i, b:b + 1, None, :])
        if i % 4 == 0:
            y, w_out = gdn_layer(xs, mods, gdn_w_in, gdn_conv, gdn_a_log, gdn_dt_bias, gdn_norm_g, gdn_w_out)
        elif i % 4 == 1:
            y, w_out = mlstm_layer(xs, mods, mlstm_w_in, mlstm_gate_b, mlstm_norm_g, mlstm_w_out)
        elif i % 4 == 2:
            y, w_out = rwkv_layer(xs, mods, rwkv_mu, rwkv_w_rkv, rwkv_w0, rwkv_w1, rwkv_w2, rwkv_a0, rwkv_a1,
                                  rwkv_a2, rwkv_g1, rwkv_g2, rwkv_k_k, rwkv_k_a, rwkv_r_k, rwkv_lnx_g,
                                  rwkv_lnx_b, rwkv_w_out)
        else:
            z = linear(xs, na_w_in.astype(BF16), mods=mods, modulate=(0, 1))
            y, w_out = na_attention(z, _na_bias_table(na_rpb)), na_w_out
        off = ctx_tiles if (i % 4 == 3) else 0
        xs1 = out_proj_ln(y, w_out.astype(BF16), xs, mods, ln_g[i, 0], ln_b[i, 0], gate_idx=2, tile_off=off)
        xs = moe_layer(xs1, mods, router_w, router_b, moe_w1, moe_w3, moe_w2, i, ln_g[i, 1], ln_b[i, 1],
                       tile_off=off)
    return xs
```

```python
import functools
import math

import numpy as np
import jax
import jax.numpy as jnp
from jax import lax
from jax.experimental import pallas as pl
from jax.experimental.pallas import tpu as pltpu

F32 = jnp.float32
BF16 = jnp.bfloat16

D_MODEL = 1024
DEPTH = 4
GRID_W = 64
CTX_LEN = 256
ALPHA = (2 * DEPTH) ** 0.25
LN_EPS = 1e-5
ROPE_BASE = 10000.0

GDN_HEADS = 8
GDN_DK = 128
GDN_DV = 128
MLSTM_HEADS = 4
MLSTM_DQK = 128
MLSTM_DV = 256
RWKV_HEAD = 64
RWKV_HEADS = 16
RWKV_GN_EPS = 64e-5
NA_HEADS = 16
NA_DH = 64
NA_WIN_ROWS = 8
NA_WIN_COLS = 16
N_EXPERTS = 16
N_GROUPS = 4
D_EXPERT = 512

CHUNK = 64
GDN_UNROLL = 4
RWKV_UNROLL = 2
NA_ROWS_PER_STEP = 2
MLSTM_HEADS_PER_STEP = 2
TOKEN_TILE = 256
N_CHUNK_COLS = 512
FFN_TILE = 512
NEG = -1e30
VMEM_LIMIT = 56 * 1024 * 1024

_HI = lax.Precision.HIGHEST


def _cparams(sem):
    return pltpu.CompilerParams(dimension_semantics=sem, vmem_limit_bytes=VMEM_LIMIT)


def _mm(a, b):
    return jnp.dot(a.astype(BF16), b.astype(BF16), preferred_element_type=F32)


def _mm_nt(a, b):
    return lax.dot_general(a.astype(BF16), b.astype(BF16), (((1,), (1,)), ((), ())),
                           preferred_element_type=F32)


def _mm_tn(a, b):
    return lax.dot_general(a.astype(BF16), b.astype(BF16), (((0,), (0,)), ((), ())),
                           preferred_element_type=F32)


def _mmf(a, b):
    return jnp.dot(a, b, preferred_element_type=F32, precision=_HI)


def _silu(x):
    return x * (1.0 / (1.0 + jnp.exp(-x)))


def _sigmoid(x):
    return 1.0 / (1.0 + jnp.exp(-x))


def _mod_kernel(s_ref, w_ref, b_ref, o_ref):
    o_ref[0] = _mmf(_silu(s_ref[...]), w_ref[0]) + b_ref[0]


def modulation_all(c, c_ctx, ada_w, ada_b):
    b = c.shape[0]
    rows = 8 * ((b + 1 + 7) // 8)
    s = jnp.zeros((rows, D_MODEL), F32).at[:b].set(c).at[b].set(c_ctx)
    tn = 1536
    n = ada_w.shape[-1]
    return pl.pallas_call(
        _mod_kernel,
        grid=(DEPTH, n // tn),
        in_specs=[pl.BlockSpec((rows, D_MODEL), lambda i, j: (0, 0)),
                  pl.BlockSpec((1, D_MODEL, tn), lambda i, j: (i, 0, j)),
                  pl.BlockSpec((1, 1, tn), lambda i, j: (i, 0, j))],
        out_specs=pl.BlockSpec((1, rows, tn), lambda i, j: (i, 0, j)),
        out_shape=jax.ShapeDtypeStruct((DEPTH, rows, n), F32),
        compiler_params=_cparams(("arbitrary", "arbitrary")),
        name="adaln_modulation",
    )(s, ada_w, ada_b.reshape(DEPTH, 1, n))


def _mod_rows(modl_ref, modc_ref, is_ctx, idx):
    sl = slice(idx * D_MODEL, (idx + 1) * D_MODEL)
    return jnp.where(is_ctx, modc_ref[0, :, sl], modl_ref[0, :, sl])


def _linear_kernel(*refs, glob_off, modulate, act, n_main, has_small):
    it = iter(refs)
    x_ref = next(it)
    if modulate is not None:
        modl_ref, modc_ref = next(it), next(it)
    w_ref = next(it)
    ws_ref = next(it) if has_small else None
    o_ref = next(it)
    os_ref = next(it) if has_small else None

    h = x_ref[0]
    if modulate is not None:
        is_ctx = (pl.program_id(1) + glob_off) * TOKEN_TILE < CTX_LEN
        sh = _mod_rows(modl_ref, modc_ref, is_ctx, modulate[0])
        sc = _mod_rows(modl_ref, modc_ref, is_ctx, modulate[1])
        h = h * (1.0 + sc) + sh
    if act == "tanh":
        h = jnp.tanh(h)
    elif act == "sigmoid":
        h = _sigmoid(h)
    hb = h.astype(BF16)
    step = min(N_CHUNK_COLS, n_main)
    for j in range(n_main // step):
        o_ref[0, :, j * step:(j + 1) * step] = jnp.dot(
            hb, w_ref[:, j * step:(j + 1) * step], preferred_element_type=F32)
    if has_small:
        os_ref[0] = _mmf(h, ws_ref[...])


def linear(x, w_bf16, *, mods=None, modulate=None, act=None, w_small=None):
    b, t_out, k = x.shape
    n_main = w_bf16.shape[1]
    has_small = w_small is not None
    in_specs = [pl.BlockSpec((1, TOKEN_TILE, k), lambda bi, ti: (bi, ti, 0))]
    args = [x]
    if modulate is not None:
        modl, modc = mods
        in_specs += [pl.BlockSpec((1, 1, 6 * D_MODEL), lambda bi, ti: (bi, 0, 0)),
                     pl.BlockSpec((1, 1, 6 * D_MODEL), lambda bi, ti: (0, 0, 0))]
        args += [modl, modc]
    in_specs.append(pl.BlockSpec((k, n_main), lambda bi, ti: (0, 0)))
    args.append(w_bf16)
    out_specs = [pl.BlockSpec((1, TOKEN_TILE, n_main), lambda bi, ti: (bi, ti, 0))]
    out_shape = [jax.ShapeDtypeStruct((b, t_out, n_main), F32)]
    if has_small:
        ns = w_small.shape[1]
        in_specs.append(pl.BlockSpec((k, ns), lambda bi, ti: (0, 0)))
        args.append(w_small)
        out_specs.append(pl.BlockSpec((1, TOKEN_TILE, ns), lambda bi, ti: (bi, ti, 0)))
        out_shape.append(jax.ShapeDtypeStruct((b, t_out, ns), F32))
    res = pl.pallas_call(
        functools.partial(_linear_kernel, glob_off=0, modulate=modulate, act=act,
                          n_main=n_main, has_small=has_small),
        grid=(b, t_out // TOKEN_TILE),
        in_specs=in_specs, out_specs=out_specs, out_shape=out_shape,
        compiler_params=_cparams(("parallel", "parallel")),
        name="linear",
    )(*args)
    return res if has_small else res[0]


def _layer_norm_rows(r, g, b):
    mu = jnp.mean(r, axis=-1, keepdims=True)
    rc = r - mu
    var = jnp.mean(rc * rc, axis=-1, keepdims=True)
    return rc * lax.rsqrt(var + LN_EPS) * g + b


def _head_post(y, gate, norm_g, post):
    head_w, centre, act = post
    parts = []
    for h in range(y.shape[1] // head_w):
        seg = y[:, h * head_w:(h + 1) * head_w]
        if centre:
            seg = seg - jnp.mean(seg, axis=1, keepdims=True)
        parts.append(seg * lax.rsqrt(jnp.mean(seg * seg, axis=1, keepdims=True) + 1e-6))
    return jnp.concatenate(parts, axis=1) * norm_g * act(gate)


def _out_ln_kernel(*refs, tile_off, gate_idx, post):
    if post is None:
        y_ref, w_ref, x_ref, modl_ref, modc_ref, g_ref, b_ref, o_ref = refs
        y = y_ref[0]
    elif post == "rwkv":
        y_ref, r_ref, v_ref, kd0_ref, kd1_ref, gg_ref, pr_ref = refs[:7]
        w_ref, x_ref, modl_ref, modc_ref, g_ref, b_ref, o_ref = refs[7:]
        y = _rwkv_post(y_ref[0], r_ref[0], v_ref[0], kd0_ref[0], kd1_ref[0], gg_ref[0], pr_ref[...])
    else:
        y_ref, gate_ref, ng_ref, w_ref, x_ref, modl_ref, modc_ref, g_ref, b_ref, o_ref = refs
        y = _head_post(y_ref[0], gate_ref[0], ng_ref[...], post)
    is_ctx = (pl.program_id(1) + tile_off) * TOKEN_TILE < CTX_LEN
    gate = _mod_rows(modl_ref, modc_ref, is_ctx, gate_idx)
    f = jnp.dot(y.astype(BF16), w_ref[...], preferred_element_type=F32)
    r = ALPHA * x_ref[0] + gate * f
    o_ref[0] = _layer_norm_rows(r, g_ref[...], b_ref[...])


def out_proj_ln(y, w_bf16, x, mods, ln_g, ln_b, *, gate_idx, tile_off=0, post=None, post_args=None):
    b, t_y, k = y.shape
    modl, modc = mods
    tok = pl.BlockSpec((1, TOKEN_TILE, k), lambda bi, ti: (bi, ti, 0))
    in_specs, args = [tok], [y]
    if post == "rwkv":
        *tiles, params = post_args
        in_specs += [tok] * len(tiles) + [pl.BlockSpec(params.shape, lambda bi, ti: (0, 0))]
        args += [*tiles, params]
    elif post is not None:
        z, gate_block, norm_g = post_args
        in_specs += [pl.BlockSpec((1, TOKEN_TILE, k), lambda bi, ti: (bi, ti, gate_block)),
                     pl.BlockSpec((1, k), lambda bi, ti: (0, 0))]
        args += [z, norm_g.reshape(1, k)]
    in_specs += [pl.BlockSpec((k, D_MODEL), lambda bi, ti: (0, 0)),
                 pl.BlockSpec((1, TOKEN_TILE, D_MODEL), lambda bi, ti: (bi, ti + tile_off, 0)),
                 pl.BlockSpec((1, 1, 6 * D_MODEL), lambda bi, ti: (bi, 0, 0)),
                 pl.BlockSpec((1, 1, 6 * D_MODEL), lambda bi, ti: (0, 0, 0)),
                 pl.BlockSpec((1, D_MODEL), lambda bi, ti: (0, 0)),
                 pl.BlockSpec((1, D_MODEL), lambda bi, ti: (0, 0))]
    args += [w_bf16, x, modl, modc, ln_g.reshape(1, -1), ln_b.reshape(1, -1)]
    return pl.pallas_call(
        functools.partial(_out_ln_kernel, tile_off=tile_off, gate_idx=gate_idx, post=post),
        grid=(b, t_y // TOKEN_TILE),
        in_specs=in_specs,
        out_specs=pl.BlockSpec((1, TOKEN_TILE, D_MODEL), lambda bi, ti: (bi, ti, 0)),
        out_shape=jax.ShapeDtypeStruct((b, t_y, D_MODEL), F32),
        compiler_params=_cparams(("parallel", "parallel")),
        name="out_proj_ln",
    )(*args)


def _seq_prep_kernel(z_ref, cos_ref, sin_ref, cw_ref, o_ref, *, conv, norm_blocks, rope_blocks, scale_blocks, scale):
    x = z_ref[0]
    t = x.shape[0]
    j = pl.program_id(1)
    if conv:
        row = _iota2(t, 1, 0)
        seg_first = (row == 0) | (row == CTX_LEN)
        seg_last = (row == CTX_LEN - 1) | (row == t - 1)
        x_prev = jnp.where(seg_first, 0.0, pltpu.roll(x, 1, axis=0))
        x_next = jnp.where(seg_last, 0.0, pltpu.roll(x, t - 1, axis=0))
        w = cw_ref[...]
        x = _silu(x_prev * w[0:1] + x * w[1:2] + x_next * w[2:3])
    if norm_blocks:
        normed = x * lax.rsqrt(jnp.sum(x * x, axis=1, keepdims=True) + 1e-6)
        x = jnp.where(j < norm_blocks, normed, x)
    lane = _iota2(1, x.shape[1], 1)
    partner = jnp.where((lane % 64) < 32, pltpu.roll(x, 96, axis=1), pltpu.roll(x, 32, axis=1))
    roped = x * cos_ref[...] + partner * sin_ref[...]
    x = jnp.where(j < rope_blocks, roped, x)
    o_ref[0] = jnp.where(j < scale_blocks, x * scale, x)


def _rope_lane_tables(t):
    pos = jnp.arange(t - CTX_LEN)
    inv_freq = ROPE_BASE ** (-jnp.arange(32, dtype=F32) / 32)
    ang_r = (pos // GRID_W).astype(F32)[:, None] * inv_freq[None, :]
    ang_c = (pos % GRID_W).astype(F32)[:, None] * inv_freq[None, :]
    cos = jnp.concatenate([jnp.cos(ang_r), jnp.cos(ang_r), jnp.cos(ang_c), jnp.cos(ang_c)], axis=1)
    sin = jnp.concatenate([-jnp.sin(ang_r), jnp.sin(ang_r), -jnp.sin(ang_c), jnp.sin(ang_c)], axis=1)
    pad = lambda a, v: jnp.concatenate([jnp.full((CTX_LEN, 128), v, F32), a], axis=0)
    return pad(cos, 1.0), pad(sin, 0.0)


def seq_prep(z, n_blocks, *, conv_w=None, norm_blocks, rope_blocks, scale_blocks, scale):
    b, t, _ = z.shape
    cos, sin = _rope_lane_tables(t)
    conv = conv_w is not None
    cw = conv_w if conv else jnp.zeros((3, n_blocks * 128), F32)
    return pl.pallas_call(
        functools.partial(_seq_prep_kernel, conv=conv, norm_blocks=norm_blocks, rope_blocks=rope_blocks,
                          scale_blocks=scale_blocks, scale=scale),
        grid=(b, n_blocks),
        in_specs=[pl.BlockSpec((1, t, 128), lambda bi, ji: (bi, 0, ji)),
                  pl.BlockSpec((t, 128), lambda bi, ji: (0, 0)),
                  pl.BlockSpec((t, 128), lambda bi, ji: (0, 0)),
                  pl.BlockSpec((3, 128), lambda bi, ji: (0, ji))],
        out_specs=pl.BlockSpec((1, t, 128), lambda bi, ji: (bi, 0, ji)),
        out_shape=jax.ShapeDtypeStruct((b, t, n_blocks * 128), F32),
        compiler_params=_cparams(("parallel", "parallel")),
        name="seq_prep",
    )(z, cos, sin, cw)


def _iota2(n, m, axis):
    return lax.broadcasted_iota(jnp.int32, (n, m), axis)


def _row_to_col(row, eye):
    return jnp.sum(jnp.where(eye, row, 0.0), axis=1, keepdims=True)


def _tri_solve_steps(n_mat, rhs_list, eye_f, blockdiag):
    nd = jnp.where(blockdiag, n_mat, 0.0)
    ne = n_mat - nd
    p = eye_f + nd
    n2 = _mm(nd, nd)
    yield
    p = p + _mm(p, n2)
    n4 = _mm(n2, n2)
    yield
    p = p + _mm(p, n4)
    n8 = _mm(n4, n4)
    yield
    dinv = p + _mm(p, n8)
    yield
    m = _mm(dinv, ne)
    xs = [_mm(dinv, r) for r in rhs_list]
    yield
    m2 = _mm(m, m)
    xs = [x + _mm(m, x) for x in xs]
    yield
    xs = [x + _mm(m2, x) for x in xs]
    yield
    return xs


def _run_interleaved(gens):
    gens = list(gens)
    while gens:
        alive = []
        for g in gens:
            try:
                next(g)
                alive.append(g)
            except StopIteration:
                pass
        gens = alive


def _split2(x):
    hi = x.astype(BF16)
    return hi, (x - hi.astype(F32)).astype(BF16)


def _cumsum_rows(x, cum):
    hi, lo = _split2(x)
    cb = cum.astype(BF16)
    return jnp.dot(hi, cb, preferred_element_type=F32) + jnp.dot(lo, cb, preferred_element_type=F32)


def _cumsum_cols(cum, x):
    hi, lo = _split2(x)
    cb = cum.astype(BF16)
    return jnp.dot(cb, hi, preferred_element_type=F32) + jnp.dot(cb, lo, preferred_element_type=F32)


def _chunk_order(direction, n_ctx, n_tot):
    def order(j):
        if direction == 0:
            return j
        return jnp.where(j < n_ctx, n_ctx - 1 - j, n_tot - 1 - (j - n_ctx))
    return order


def _masks(direction):
    ii = _iota2(CHUNK, CHUNK, 0)
    jj = _iota2(CHUNK, CHUNK, 1)
    if direction == 0:
        incl, strict = jj <= ii, jj < ii
    else:
        incl, strict = jj >= ii, jj > ii
    return ii, jj, incl, strict


def _gdn_kernel(q_ref, k_ref, v_ref, la_ref, be_ref, o_ref, g_s, lhs_s, add_s, s_s, ob_s, *, n_ctx, n_tot):
    ii, jj, _, _ = _masks(0)
    eye = ii == jj
    eye_f = eye.astype(F32)
    blockdiag = (ii // 16) == (jj // 16)
    dirs = []
    for d in (0, 1):
        _, _, incl, strict = _masks(d)
        cum = (ii <= jj).astype(F32) if d == 0 else (ii >= jj).astype(F32)
        g_s[d] = _cumsum_rows(la_ref[d, 0, 0], cum)
        dirs.append((incl, strict, CHUNK - 1 if d == 0 else 0))

    def prep(j, carry):
        loaded = []
        for uu in range(GDN_UNROLL):
            n = j * GDN_UNROLL + uu
            rows = pl.ds(pl.multiple_of(n * CHUNK, CHUNK), CHUNK)
            loaded.append((n, q_ref[0, rows, :], k_ref[0, rows, :], v_ref[0, rows, :],
                           [g_s[d, pl.ds(n, 1), :] for d in (0, 1)],
                           [be_ref[d, 0, 0, pl.ds(n, 1), :] for d in (0, 1)]))
        stores = []
        grams = [(_mm_nt(kc, kc), _mm_nt(qc, kc)) for _, qc, kc, _, _, _ in loaded]

        def chain(n, qc, kc, vc, g_row, be_row, kk, qk, d, incl, strict, last):
            g_col = _row_to_col(g_row, eye)
            be_col = _row_to_col(be_row, eye)
            g_last = g_row[:, last:last + 1]
            gamma = jnp.exp(jnp.where(incl, g_col - g_row, NEG))
            n_mat = jnp.where(strict, -(be_col * kk * gamma), 0.0)
            eg = jnp.exp(g_col)
            rhs = jnp.concatenate([be_col * vc, (be_col * eg) * kc], axis=1)
            (uw,) = yield from _tri_solve_steps(n_mat, [rhs], eye_f, blockdiag)
            p_mat = jnp.where(incl, qk * gamma, 0.0)
            p_uw = _mm(p_mat, uw)
            k_dec = kc * jnp.exp(g_last - g_col)
            k_uw = _mm_tn(k_dec, uw)
            q_t = qc * eg - p_uw[:, GDN_DV:]
            stores.append((d, n, jnp.concatenate([q_t, -k_uw[:, GDN_DV:]], axis=0).astype(BF16),
                           jnp.concatenate([p_uw[:, :GDN_DV], k_uw[:, :GDN_DV]], axis=0)))

        _run_interleaved(
            chain(n, qc, kc, vc, g_rows[d], be_rows[d], kk, qk, d, *dirs[d])
            for (n, qc, kc, vc, g_rows, be_rows), (kk, qk) in zip(loaded, grams) for d in (0, 1))
        for d, n, lhs, add in stores:
            lhs_s[d, n] = lhs
            add_s[d, n] = add
        return carry

    lax.fori_loop(0, n_tot // GDN_UNROLL, prep, 0)

    s_s[...] = jnp.zeros_like(s_s)
    orders = [_chunk_order(d, n_ctx, n_tot) for d in (0, 1)]
    o_refs = (o_ref.at[0], ob_s)

    def step(j, carry):
        loaded = []
        for d, (_, _, last) in enumerate(dirs):
            n = orders[d](j)
            loaded.append((n, s_s[d], lhs_s[d, n], add_s[d, n], g_s[d, pl.ds(n, 1), last:last + 1]))
        results = []
        for n, s, lhs, add, g_last in loaded:
            z = jnp.dot(lhs, s.astype(BF16), preferred_element_type=F32) + add
            results.append((n, jnp.exp(g_last) * s + z[CHUNK:], z[:CHUNK]))
        for d, (n, s_new, o) in enumerate(results):
            s_s[d] = s_new
            o_refs[d][pl.ds(pl.multiple_of(n * CHUNK, CHUNK), CHUNK), :] = o
        return carry

    lax.fori_loop(0, n_tot, step, 0)
    o_ref[0] += ob_s[...]


def gdn_scan(qkv, log_alpha, beta):
    b, t, _ = qkv.shape
    n_tot = t // CHUNK
    ncp = log_alpha.shape[3]
    blk = pl.BlockSpec((1, t, GDN_DK), lambda bi, hi: (bi, 0, hi))
    k_blk = pl.BlockSpec((1, t, GDN_DK), lambda bi, hi: (bi, 0, GDN_HEADS + hi))
    v_blk = pl.BlockSpec((1, t, GDN_DV), lambda bi, hi: (bi, 0, 2 * GDN_HEADS + hi))
    gate_blk = pl.BlockSpec((2, 1, 1, ncp, CHUNK), lambda bi, hi: (0, bi, hi, 0, 0))
    return pl.pallas_call(
        functools.partial(_gdn_kernel, n_ctx=CTX_LEN // CHUNK, n_tot=n_tot),
        grid=(b, GDN_HEADS),
        in_specs=[blk, k_blk, v_blk, gate_blk, gate_blk],
        out_specs=blk,
        out_shape=jax.ShapeDtypeStruct((b, t, GDN_HEADS * GDN_DV), F32),
        scratch_shapes=[pltpu.VMEM((2, ncp, CHUNK), F32),
                        pltpu.VMEM((2, n_tot, CHUNK + GDN_DK, GDN_DV), BF16),
                        pltpu.VMEM((2, n_tot, CHUNK + GDN_DK, GDN_DV), F32),
                        pltpu.VMEM((2, GDN_DK, GDN_DV), F32),
                        pltpu.VMEM((t, GDN_DV), F32)],
        compiler_params=_cparams(("parallel", "parallel")),
        name="gdn_scan",
    )(qkv, qkv, qkv, log_alpha, beta)


def _mlstm_kernel(q_ref, k_ref, v_ref, ip_ref, lf_ref, o_ref, b_s, c_s, n_s, m_s, ob_s, *, n_ctx, n_tot):
    ii, jj, _, _ = _masks(0)
    eye = ii == jj
    hps = MLSTM_HEADS_PER_STEP
    dirs = []
    for d in (0, 1):
        _, _, incl, _ = _masks(d)
        cum = (ii <= jj).astype(F32) if d == 0 else (ii >= jj).astype(F32)
        for hh in range(hps):
            b_s[d, hh] = _cumsum_rows(lf_ref[d, 0, hh], cum)
        dirs.append((incl, CHUNK - 1 if d == 0 else 0))
    c_s[...] = jnp.zeros_like(c_s)
    n_s[...] = jnp.zeros_like(n_s)
    m_s[...] = jnp.zeros_like(m_s)
    orders = [_chunk_order(d, n_ctx, n_tot) for d in (0, 1)]
    o_refs = (o_ref.at[0], ob_s)

    def step(j, carry):
        loaded = {}
        for d in (0, 1):
            n = orders[d](j)
            rows = pl.ds(pl.multiple_of(n * CHUNK, CHUNK), CHUNK)
            for hh in range(hps):
                qsl = slice(hh * MLSTM_DQK, (hh + 1) * MLSTM_DQK)
                vsl = slice(hh * MLSTM_DV, (hh + 1) * MLSTM_DV)
                loaded[(d, hh)] = (q_ref[0, rows, qsl], k_ref[0, rows, qsl], v_ref[0, rows, vsl],
                                   b_s[d, hh, pl.ds(n, 1), :], ip_ref[d, 0, hh, pl.ds(n, 1), :],
                                   c_s[d, hh], n_s[d, hh], m_s[d, hh], rows, vsl)
        results = {}

        def chain(key, qc, kc, vc, b_row, ip_row, c_st, n_st, m_st, incl, last):
            b_col = _row_to_col(b_row, eye)
            b_last = b_row[:, last:last + 1]
            log_d = jnp.where(incl, b_col - b_row + ip_row, NEG)
            m_intra = jnp.max(log_d, axis=1, keepdims=True)
            qk = _mm_nt(qc, kc)
            qc_st = _mm(qc, c_st)
            log_end = b_last - b_row + ip_row
            m_end = jnp.max(log_end, axis=1, keepdims=True)
            m_row = jnp.maximum(b_col + m_st, m_intra)
            w_state = jnp.exp(b_col + m_st - m_row)
            m_new = jnp.maximum(b_last + m_st, m_end)
            decay = jnp.exp(b_last + m_st - m_new)
            k_w = kc * _row_to_col(jnp.exp(log_end - m_new), eye)
            c_new = decay * c_st + _mm_tn(k_w, vc)
            n_new = decay * n_st + jnp.sum(k_w, axis=0, keepdims=True)
            yield
            w_intra = jnp.exp(log_d - m_row) * qk
            num = w_state * qc_st + _mm(w_intra, vc)
            den = (w_state * jnp.sum(qc * n_st, axis=1, keepdims=True)
                   + jnp.sum(w_intra, axis=1, keepdims=True))
            yield
            results[key] = (num / jnp.maximum(jnp.abs(den), jnp.exp(-m_row)), c_new, n_new, m_new)

        _run_interleaved(chain(key, *vals[:8], *dirs[key[0]]) for key, vals in loaded.items())
        for (d, hh), (h, c_new, n_new, m_new) in results.items():
            rows, vsl = loaded[(d, hh)][8:]
            c_s[d, hh] = c_new
            n_s[d, hh] = n_new
            m_s[d, hh] = m_new
            o_refs[d][rows, vsl] = h
        return carry

    lax.fori_loop(0, n_tot, step, 0)
    o_ref[0] += ob_s[...]


def mlstm_scan(qk, z, i_pre, log_f):
    b, t, _ = qk.shape
    n_tot = t // CHUNK
    ncp = i_pre.shape[3]
    hps = MLSTM_HEADS_PER_STEP
    n_steps = MLSTM_HEADS // hps
    v_off = 2 * MLSTM_HEADS * MLSTM_DQK // (hps * MLSTM_DV)
    q_blk = pl.BlockSpec((1, t, hps * MLSTM_DQK), lambda bi, hi: (bi, 0, hi))
    k_blk = pl.BlockSpec((1, t, hps * MLSTM_DQK), lambda bi, hi: (bi, 0, n_steps + hi))
    v_blk = pl.BlockSpec((1, t, hps * MLSTM_DV), lambda bi, hi: (bi, 0, v_off + hi))
    o_blk = pl.BlockSpec((1, t, hps * MLSTM_DV), lambda bi, hi: (bi, 0, hi))
    gate_blk = pl.BlockSpec((2, 1, hps, ncp, CHUNK), lambda bi, hi: (0, bi, hi, 0, 0))
    return pl.pallas_call(
        functools.partial(_mlstm_kernel, n_ctx=CTX_LEN // CHUNK, n_tot=n_tot),
        grid=(b, n_steps),
        in_specs=[q_blk, k_blk, v_blk, gate_blk, gate_blk],
        out_specs=o_blk,
        out_shape=jax.ShapeDtypeStruct((b, t, MLSTM_HEADS * MLSTM_DV), F32),
        scratch_shapes=[pltpu.VMEM((2, hps, ncp, CHUNK), F32),
                        pltpu.VMEM((2, hps, MLSTM_DQK, MLSTM_DV), F32),
                        pltpu.VMEM((2, hps, 1, MLSTM_DQK), F32),
                        pltpu.VMEM((2, hps, 1, 1), F32),
                        pltpu.VMEM((t, hps * MLSTM_DV), F32)],
        compiler_params=_cparams(("parallel", "parallel")),
        name="mlstm_scan",
    )(qk, qk, z, i_pre, log_f)


def _rwkv_kernel(r_ref, v_ref, kk_ref, lw0_ref, lw1_ref, kd0_ref, kd1_ref, bv0_ref, bv1_ref, o_ref,
                 lhs_s, add_s, gl_s, s_s, ob_s, *, n_ctx, n_tot):
    dir_refs = ((lw0_ref, kd0_ref, bv0_ref), (lw1_ref, kd1_ref, bv1_ref))
    ii, jj, _, _ = _masks(0)
    eye = ii == jj
    eye_f = eye.astype(F32)
    blockdiag = (ii // 16) == (jj // 16)
    hd = RWKV_HEAD
    dirs = []
    for d in (0, 1):
        _, _, incl, strict = _masks(d)
        dirs.append((incl, strict, CHUNK - 1 if d == 0 else 0))
    zero = jnp.zeros((hd, hd), F32)

    def prep(j, carry):
        loaded = []
        for uu in range(RWKV_UNROLL):
            n = j * RWKV_UNROLL + uu
            rows = pl.ds(pl.multiple_of(n * CHUNK, CHUNK), CHUNK)
            loaded.append((n, r_ref[0, rows, :], v_ref[0, rows, :], kk_ref[0, rows, :],
                           [tuple(ref[0, rows, :] for ref in dir_refs[d]) for d in (0, 1)]))
        parts = {}

        def chain(key, r, v, kk, lw, kd, bv, gcs, e_end, incl, strict):
            e_neg = jnp.exp(-gcs)
            a_h = -kk * jnp.exp(gcs - lw)
            r_h = r * jnp.exp(gcs)
            b_h = bv * e_neg
            k_h = kd * e_neg
            a_ab = jnp.where(strict, _mm_nt(a_h, b_h), 0.0)
            a_ak = jnp.where(strict, _mm_nt(a_h, k_h), 0.0)
            a_rb = jnp.where(incl, _mm_nt(r_h, b_h), 0.0)
            a_rk = jnp.where(incl, _mm_nt(r_h, k_h), 0.0)
            yield
            av = _mm(a_ak, v)
            ta, tav = yield from _tri_solve_steps(a_ab, [a_h, av], eye_f, blockdiag)
            b_g = bv * e_end
            parts[key] = (r_h + _mm(a_rb, ta), _mm_tn(b_g, ta),
                          _mm(a_rb, tav) + _mm(a_rk, v), _mm_tn(b_g, tav) + _mm_tn(kd * e_end, v))

        gens, gls = [], {}
        for uu, (_, r2, v2, kk2, per_dir) in enumerate(loaded):
            for d, (incl, strict, last) in enumerate(dirs):
                lw2, kd2, bv2 = per_dir[d]
                gcs2 = _cumsum_cols(incl.astype(F32), lw2)
                gl_row2 = gcs2[last:last + 1, :]
                e_end2 = jnp.exp(gl_row2 - gcs2)
                gls[(uu, d)] = jnp.exp(gl_row2)
                for hh in range(2):
                    sl = slice(hh * hd, (hh + 1) * hd)
                    gens.append(chain((uu, d, hh), *(a[:, sl] for a in (r2, v2, kk2, lw2, kd2, bv2, gcs2, e_end2)),
                                      incl, strict))
        _run_interleaved(gens)
        for uu, (n, _, _, _, _) in enumerate(loaded):
            for d in (0, 1):
                (rt0, mx0, yc0, kv0), (rt1, mx1, yc1, kv1) = parts[(uu, d, 0)], parts[(uu, d, 1)]
                stack = lambda t0, t1, b0, b1: jnp.concatenate(
                    [jnp.concatenate([t0, t1], axis=1), jnp.concatenate([b0, zero], axis=1),
                     jnp.concatenate([zero, b1], axis=1)], axis=0)
                lhs_s[d, n] = stack(rt0, rt1, mx0, mx1).astype(BF16)
                add_s[d, n] = stack(yc0, yc1, kv0, kv1)
                gl_s[d, pl.ds(n, 1), :] = gls[(uu, d)]
        return carry

    lax.fori_loop(0, n_tot // RWKV_UNROLL, prep, 0)

    s_s[...] = jnp.zeros_like(s_s)
    orders = [_chunk_order(d, n_ctx, n_tot) for d in (0, 1)]
    eye2 = _iota2(2 * hd, 2 * hd, 0) == _iota2(2 * hd, 2 * hd, 1)
    o_refs = (o_ref.at[0], ob_s)

    def step(j, carry):
        loaded = []
        for d in (0, 1):
            n = orders[d](j)
            loaded.append((n, s_s[d], lhs_s[d, n], add_s[d, n], gl_s[d, pl.ds(n, 1), :]))
        results = []
        for n, s, lhs, add, gl in loaded:
            z = jnp.dot(lhs, s.astype(BF16), preferred_element_type=F32) + add
            results.append((n, _row_to_col(gl, eye2) * s + z[CHUNK:], z[:CHUNK]))
        for d, (n, s_new, y) in enumerate(results):
            s_s[d] = s_new
            o_refs[d][pl.ds(pl.multiple_of(n * CHUNK, CHUNK), CHUNK), :] = y
        return carry

    lax.fori_loop(0, n_tot, step, 0)
    o_ref[0] += ob_s[...]


def rwkv_scan(r, v, kk, log_w, k_dir, b_dir):
    b, t, _ = r.shape
    n_tot = t // CHUNK
    blk = pl.BlockSpec((1, t, 128), lambda bi, hi: (bi, 0, hi))
    return pl.pallas_call(
        functools.partial(_rwkv_kernel, n_ctx=CTX_LEN // CHUNK, n_tot=n_tot),
        grid=(b, RWKV_HEADS // 2),
        in_specs=[blk] * 9,
        out_specs=blk,
        out_shape=jax.ShapeDtypeStruct((b, t, D_MODEL), F32),
        scratch_shapes=[pltpu.VMEM((2, n_tot, CHUNK + 2 * RWKV_HEAD, 2 * RWKV_HEAD), BF16),
                        pltpu.VMEM((2, n_tot, CHUNK + 2 * RWKV_HEAD, 2 * RWKV_HEAD), F32),
                        pltpu.VMEM((2, 8 * ((n_tot + 7) // 8), 2 * RWKV_HEAD), F32),
                        pltpu.VMEM((2, 2 * RWKV_HEAD, 2 * RWKV_HEAD), F32),
                        pltpu.VMEM((t, 2 * RWKV_HEAD), F32)],
        compiler_params=_cparams(("parallel", "parallel")),
        name="rwkv_scan",
    )(r, v, kk, *log_w, *k_dir, *b_dir)


def _na_kernel(q_ref, k_ref, v_ref, bias_ref, o_ref, *, rows):
    scale = NA_DH ** -0.5
    slab = NA_WIN_ROWS * GRID_W
    lane = _iota2(1, 2 * NA_DH, 1)
    head_masks = (lane < NA_DH, lane >= NA_DH)
    kc2 = k_ref[0, 0:CTX_LEN, :].astype(BF16)
    vc2 = v_ref[0, 0:CTX_LEN, :].astype(BF16)

    def body(j, carry):
        loaded = []
        for uu in range(NA_ROWS_PER_STEP):
            r = j * NA_ROWS_PER_STEP + uu
            r0 = jnp.clip(r - NA_WIN_ROWS // 2, 0, rows - NA_WIN_ROWS)
            dr0 = r0 - r + NA_WIN_ROWS - 1
            krows = pl.ds(pl.multiple_of(CTX_LEN + r0 * GRID_W, GRID_W), slab)
            loaded.append((r, q_ref[0, pl.ds(pl.multiple_of(CTX_LEN + r * GRID_W, GRID_W), GRID_W), :],
                           k_ref[0, krows, :].astype(BF16), v_ref[0, krows, :].astype(BF16),
                           [bias_ref[hh, pl.ds(dr0, 1)][0] for hh in range(2)]))
        outs = {}

        def chain(key, q2, ks2, vs2, bias, mask):
            qh = jnp.where(mask, q2, 0.0)
            s_lat = _mm_nt(qh, ks2) * scale + bias
            s_ctx = _mm_nt(qh, kc2) * scale
            yield
            m = jnp.maximum(jnp.max(s_lat, axis=1, keepdims=True), jnp.max(s_ctx, axis=1, keepdims=True))
            p_lat = jnp.exp(s_lat - m)
            p_ctx = jnp.exp(s_ctx - m)
            l = jnp.sum(p_lat, axis=1, keepdims=True) + jnp.sum(p_ctx, axis=1, keepdims=True)
            outs[key] = (_mm(p_lat, vs2) + _mm(p_ctx, vc2)) / l
            yield

        _run_interleaved(chain((uu, hh), q2, ks2, vs2, biases[hh], head_masks[hh])
                         for uu, (_, q2, ks2, vs2, biases) in enumerate(loaded) for hh in range(2))
        for uu, (r, _, _, _, _) in enumerate(loaded):
            o_ref[0, pl.ds(pl.multiple_of(r * GRID_W, GRID_W), GRID_W), :] = jnp.where(
                head_masks[0], outs[(uu, 0)], outs[(uu, 1)])
        return carry

    lax.fori_loop(0, rows // NA_ROWS_PER_STEP, body, 0)


def _na_bias_table(rpb):
    cols = np.arange(GRID_W)
    win_c0 = np.clip(cols - NA_WIN_COLS // 2, 0, GRID_W - NA_WIN_COLS)
    kc = np.arange(GRID_W)
    in_win = (kc[None, :] >= win_c0[:, None]) & (kc[None, :] < win_c0[:, None] + NA_WIN_COLS)
    dc = np.clip(kc[None, :] - cols[:, None] + NA_WIN_COLS - 1, 0, 2 * NA_WIN_COLS - 2)
    dr = np.arange(NA_WIN_ROWS)[:, None] + np.arange(NA_WIN_ROWS)[None, :]
    tab = rpb.astype(F32)[:, dr][:, :, :, dc]
    tab = jnp.where(in_win[None, None, None], tab, NEG)
    tab = tab.transpose(0, 1, 3, 2, 4)
    return tab.reshape(NA_HEADS, NA_WIN_ROWS, GRID_W, NA_WIN_ROWS * GRID_W)


def na_attention(z, bias_tab):
    b, t, _ = z.shape
    t_lat = t - CTX_LEN
    n_pairs = NA_HEADS // 2
    return pl.pallas_call(
        functools.partial(_na_kernel, rows=t_lat // GRID_W),
        grid=(n_pairs, b),
        in_specs=[pl.BlockSpec((1, t, 128), lambda hi, bi: (bi, 0, hi)),
                  pl.BlockSpec((1, t, 128), lambda hi, bi: (bi, 0, n_pairs + hi)),
                  pl.BlockSpec((1, t, 128), lambda hi, bi: (bi, 0, 2 * n_pairs + hi)),
                  pl.BlockSpec((2, NA_WIN_ROWS, GRID_W, NA_WIN_ROWS * GRID_W), lambda hi, bi: (hi, 0, 0, 0))],
        out_specs=pl.BlockSpec((1, t_lat, 128), lambda hi, bi: (bi, 0, hi)),
        out_shape=jax.ShapeDtypeStruct((b, t_lat, D_MODEL), F32),
        compiler_params=_cparams(("parallel", "parallel")),
        name="na_attention",
    )(z, z, z, bias_tab)


def _router_kernel(x_ref, modl_ref, modc_ref, rw_ref, rb_ref, hb_ref, idx_ref, wt_ref, *, tile_off):
    is_ctx = (pl.program_id(1) + tile_off) * TOKEN_TILE < CTX_LEN
    sh = _mod_rows(modl_ref, modc_ref, is_ctx, 3)
    sc = _mod_rows(modl_ref, modc_ref, is_ctx, 4)
    h = x_ref[0] * (1.0 + sc) + sh
    hb_ref[0] = h.astype(BF16)
    logits = _mmf(h, rw_ref[...])
    mx = jnp.max(logits, axis=1, keepdims=True)
    ex = jnp.exp(logits - mx)
    probs = ex / jnp.sum(ex, axis=1, keepdims=True)
    sel = probs + rb_ref[...]
    tm = sel.shape[0]
    e_id = _iota2(tm, N_EXPERTS, 1)
    per_group = N_EXPERTS // N_GROUPS
    g_id = e_id // per_group

    def top1(vals):
        m1 = jnp.max(vals, axis=1, keepdims=True)
        i1 = jnp.min(jnp.where(vals == m1, e_id, N_EXPERTS), axis=1, keepdims=True)
        return m1, i1

    best_score, best = None, None
    for g in range(N_GROUPS):
        vals = jnp.where(g_id == g, sel, NEG)
        m1, i1 = top1(vals)
        m2, _ = top1(jnp.where(e_id == i1, NEG, vals))
        score = m1 + m2
        if g == 0:
            best_score, best = score, jnp.zeros_like(i1)
        else:
            better = score > best_score
            best = jnp.where(better, g, best)
            best_score = jnp.where(better, score, best_score)
    vals = jnp.where(g_id == best, sel, NEG)
    _, i1 = top1(vals)
    _, i2 = top1(jnp.where(e_id == i1, NEG, vals))
    p1 = jnp.sum(jnp.where(e_id == i1, probs, 0.0), axis=1, keepdims=True)
    p2 = jnp.sum(jnp.where(e_id == i2, probs, 0.0), axis=1, keepdims=True)
    tot = p1 + p2
    slot = _iota2(tm, 2, 1)
    idx_ref[0] = jnp.where(slot == 0, i1, i2)
    wt_ref[0] = jnp.where(slot == 0, p1 / tot, p2 / tot)


def moe_route(x, mods, router_w, router_b, *, tile_off):
    b, t_out, _ = x.shape
    modl, modc = mods
    return pl.pallas_call(
        functools.partial(_router_kernel, tile_off=tile_off),
        grid=(b, t_out // TOKEN_TILE),
        in_specs=[pl.BlockSpec((1, TOKEN_TILE, D_MODEL), lambda bi, ti: (bi, ti, 0)),
                  pl.BlockSpec((1, 1, 6 * D_MODEL), lambda bi, ti: (bi, 0, 0)),
                  pl.BlockSpec((1, 1, 6 * D_MODEL), lambda bi, ti: (0, 0, 0)),
                  pl.BlockSpec((D_MODEL, N_EXPERTS), lambda bi, ti: (0, 0)),
                  pl.BlockSpec((1, N_EXPERTS), lambda bi, ti: (0, 0))],
        out_specs=[pl.BlockSpec((1, TOKEN_TILE, D_MODEL), lambda bi, ti: (bi, ti, 0)),
                   pl.BlockSpec((1, TOKEN_TILE, 2), lambda bi, ti: (bi, ti, 0)),
                   pl.BlockSpec((1, TOKEN_TILE, 2), lambda bi, ti: (bi, ti, 0))],
        out_shape=[jax.ShapeDtypeStruct((b, t_out, D_MODEL), BF16),
                   jax.ShapeDtypeStruct((b, t_out, 2), jnp.int32),
                   jax.ShapeDtypeStruct((b, t_out, 2), F32)],
        compiler_params=_cparams(("parallel", "parallel")),
        name="moe_route",
    )(x, modl, modc, router_w, router_b.reshape(1, -1))


def _ffn_kernel(te_ref, nt_ref, x_ref, w1_ref, w3_ref, w2_ref, o_ref):
    @pl.when(pl.program_id(0) < nt_ref[0])
    def _():
        xb = x_ref[...]
        h1 = jnp.dot(xb, w1_ref[0, 0].astype(BF16), preferred_element_type=F32)
        h3 = jnp.dot(xb, w3_ref[0, 0].astype(BF16), preferred_element_type=F32)
        hid = (_silu(h1) * h3).astype(BF16)
        o_ref[...] = jnp.dot(hid, w2_ref[0, 0].astype(BF16), preferred_element_type=F32)

    @pl.when(pl.program_id(0) >= nt_ref[0])
    def _():
        o_ref[...] = jnp.zeros_like(o_ref)


def expert_ffn(xs, tile_expert, n_tiles_used, w1, w3, w2, layer):
    p = xs.shape[0]
    grid_spec = pltpu.PrefetchScalarGridSpec(
        num_scalar_prefetch=2,
        grid=(p // FFN_TILE,),
        in_specs=[pl.BlockSpec((FFN_TILE, D_MODEL), lambda i, te, nt: (i, 0)),
                  pl.BlockSpec((1, 1, D_MODEL, D_EXPERT), lambda i, te, nt: (layer, te[i], 0, 0)),
                  pl.BlockSpec((1, 1, D_MODEL, D_EXPERT), lambda i, te, nt: (layer, te[i], 0, 0)),
                  pl.BlockSpec((1, 1, D_EXPERT, D_MODEL), lambda i, te, nt: (layer, te[i], 0, 0))],
        out_specs=pl.BlockSpec((FFN_TILE, D_MODEL), lambda i, te, nt: (i, 0)),
    )
    return pl.pallas_call(
        _ffn_kernel,
        grid_spec=grid_spec,
        out_shape=jax.ShapeDtypeStruct((p, D_MODEL), F32),
        compiler_params=_cparams(("arbitrary",)),
        name="expert_ffn",
    )(tile_expert, n_tiles_used, xs, w1, w3, w2)


def _combine_ln_kernel(x_ref, y0_ref, y1_ref, wt_ref, modl_ref, modc_ref, g_ref, b_ref, o_ref, *, tile_off):
    is_ctx = (pl.program_id(1) + tile_off) * TOKEN_TILE < CTX_LEN
    gate = _mod_rows(modl_ref, modc_ref, is_ctx, 5)
    wt = wt_ref[0]
    f = wt[:, 0:1] * y0_ref[0] + wt[:, 1:2] * y1_ref[0]
    r = ALPHA * x_ref[0] + gate * f
    o_ref[0] = _layer_norm_rows(r, g_ref[...], b_ref[...])


def combine_ln(x, y0, y1, wt, mods, ln_g, ln_b, *, tile_off):
    b, t_out, _ = y0.shape
    modl, modc = mods
    tok = lambda bi, ti: (bi, ti, 0)
    return pl.pallas_call(
        functools.partial(_combine_ln_kernel, tile_off=tile_off),
        grid=(b, t_out // TOKEN_TILE),
        in_specs=[pl.BlockSpec((1, TOKEN_TILE, D_MODEL), tok),
                  pl.BlockSpec((1, TOKEN_TILE, D_MODEL), tok),
                  pl.BlockSpec((1, TOKEN_TILE, D_MODEL), tok),
                  pl.BlockSpec((1, TOKEN_TILE, 2), tok),
                  pl.BlockSpec((1, 1, 6 * D_MODEL), lambda bi, ti: (bi, 0, 0)),
                  pl.BlockSpec((1, 1, 6 * D_MODEL), lambda bi, ti: (0, 0, 0)),
                  pl.BlockSpec((1, D_MODEL), lambda bi, ti: (0, 0)),
                  pl.BlockSpec((1, D_MODEL), lambda bi, ti: (0, 0))],
        out_specs=pl.BlockSpec((1, TOKEN_TILE, D_MODEL), tok),
        out_shape=jax.ShapeDtypeStruct((b, t_out, D_MODEL), F32),
        compiler_params=_cparams(("parallel", "parallel")),
        name="moe_combine_ln",
    )(x, y0, y1, wt, modl, modc, ln_g.reshape(1, -1), ln_b.reshape(1, -1))


def moe_layer(x, mods, router_w, router_b, w1, w3, w2, layer, ln_g, ln_b, *, tile_off):
    hb, idx, wt = moe_route(x, mods, router_w, router_b, tile_off=tile_off)
    b, t, _ = hb.shape
    n_tok = b * t
    n_pair = 2 * n_tok
    e_flat = idx.reshape(n_pair)
    onehot = (e_flat[:, None] == jnp.arange(N_EXPERTS)[None, :]).astype(jnp.int32)
    csum = jnp.cumsum(onehot, axis=0)
    counts = csum[-1]
    rank = jnp.sum((csum - onehot) * onehot, axis=1)
    padded = ((counts + FFN_TILE - 1) // FFN_TILE) * FFN_TILE
    ends = jnp.cumsum(padded)
    offs = ends - padded
    pos = offs[e_flat] + rank
    n_rows = n_pair + N_EXPERTS * FFN_TILE
    n_tiles = n_rows // FFN_TILE
    src = jnp.zeros((n_rows,), jnp.int32).at[pos].set(jnp.arange(n_pair, dtype=jnp.int32) // 2)
    tile_start = jnp.arange(n_tiles, dtype=jnp.int32) * FFN_TILE
    tile_expert = jnp.minimum(jnp.searchsorted(ends, tile_start, side="right"), N_EXPERTS - 1).astype(jnp.int32)
    n_used = (ends[-1] // FFN_TILE).astype(jnp.int32).reshape(1)
    xs = jnp.take(hb.reshape(n_tok, D_MODEL), src, axis=0)
    ys = expert_ffn(xs, tile_expert, n_used, w1, w3, w2, layer)
    pos2 = pos.reshape(n_tok, 2)
    y0 = jnp.take(ys, pos2[:, 0], axis=0).reshape(b, t, D_MODEL)
    y1 = jnp.take(ys, pos2[:, 1], axis=0).reshape(b, t, D_MODEL)
    return combine_ln(x, y0, y1, wt, mods, ln_g, ln_b, tile_off=tile_off)


def _seg_apply(fn, a):
    return jnp.concatenate([fn(a[:, :CTX_LEN]), fn(a[:, CTX_LEN:])], axis=1)


def _rope_tables(t_lat, dh):
    quarter = dh // 4
    pos = jnp.arange(t_lat)
    inv_freq = ROPE_BASE ** (-jnp.arange(quarter, dtype=F32) / quarter)
    ang_r = (pos // GRID_W).astype(F32)[:, None] * inv_freq[None, :]
    ang_c = (pos % GRID_W).astype(F32)[:, None] * inv_freq[None, :]
    return jnp.cos(ang_r), jnp.sin(ang_r), jnp.cos(ang_c), jnp.sin(ang_c)


def _rope_lat(a, n_heads):
    b, t, hd = a.shape
    dh = hd // n_heads
    q4 = dh // 4
    lat = a[:, CTX_LEN:].reshape(b, t - CTX_LEN, n_heads, dh)
    cr, sr, cc, sc = (u[None, :, None, :] for u in _rope_tables(t - CTX_LEN, dh))
    x1, x2, x3, x4 = lat[..., :q4], lat[..., q4:2 * q4], lat[..., 2 * q4:3 * q4], lat[..., 3 * q4:]
    out = jnp.concatenate([x1 * cr - x2 * sr, x1 * sr + x2 * cr, x3 * cc - x4 * sc, x3 * sc + x4 * cc], axis=-1)
    return jnp.concatenate([a[:, :CTX_LEN], out.reshape(b, t - CTX_LEN, hd)], axis=1)


def _head_l2norm(a, n_heads, eps=1e-6):
    b, t, hd = a.shape
    ah = a.reshape(b, t, n_heads, hd // n_heads)
    return (ah * lax.rsqrt(jnp.sum(ah * ah, axis=-1, keepdims=True) + eps)).reshape(b, t, hd)


def _gate_rows(g):
    b, t = g.shape[:2]
    n = t // CHUNK
    ncp = 8 * ((n + 7) // 8)
    g = g.transpose(2, 3, 0, 4, 1).reshape(2, 2, b, g.shape[-1], n, CHUNK)
    return jnp.pad(g, ((0, 0),) * 4 + ((0, ncp - n), (0, 0)))


def _dwconv3(a, w):
    ap = jnp.pad(a, ((0, 0), (1, 1), (0, 0)))
    return ap[:, :-2] * w[0] + ap[:, 1:-1] * w[1] + ap[:, 2:] * w[2]


def gdn_layer(x, mods, w_in, conv_w, a_log, dt_bias, norm_g, w_out):
    wq = GDN_HEADS * GDN_DK
    n_main = 4 * wq
    z, ab = linear(x, w_in[:, :n_main].astype(BF16), mods=mods, modulate=(0, 1), w_small=w_in[:, n_main:])
    qkv = seq_prep(z, 3 * GDN_HEADS, conv_w=conv_w, norm_blocks=2 * GDN_HEADS, rope_blocks=2 * GDN_HEADS,
                   scale_blocks=GDN_HEADS, scale=GDN_DK ** -0.5)
    b, t, _ = x.shape
    ab = ab.reshape(b, t, 2, 2, GDN_HEADS)
    ab = ab.at[:, :, :, 0].add(dt_bias[None, None])
    rows = _gate_rows(ab)
    log_alpha = -jnp.exp(a_log)[:, None, :, None, None] * jax.nn.softplus(rows[:, 0])
    beta = jax.nn.sigmoid(rows[:, 1])
    valid = (jnp.arange(rows.shape[4]) < t // CHUNK)[:, None]
    log_alpha = jnp.where(valid, log_alpha, 0.0)
    o = gdn_scan(qkv, log_alpha, beta)
    return o, w_out, (GDN_DV, False, _silu), (z, 3, jnp.tile(norm_g, GDN_HEADS))


def mlstm_layer(x, mods, w_in, gate_b, norm_g, w_out):
    wq = MLSTM_HEADS * MLSTM_DQK
    wv = MLSTM_HEADS * MLSTM_DV
    n_main = 2 * wq + 2 * wv
    z, gt = linear(x, w_in[:, :n_main].astype(BF16), mods=mods, modulate=(0, 1), w_small=w_in[:, n_main:])
    qk = seq_prep(z, 2 * MLSTM_HEADS, norm_blocks=0, rope_blocks=2 * MLSTM_HEADS, scale_blocks=MLSTM_HEADS,
                  scale=MLSTM_DQK ** -0.5)
    b, t, _ = x.shape
    gt = gt.reshape(b, t, 2, 2, MLSTM_HEADS) + gate_b[None, None]
    rows = _gate_rows(gt)
    valid = (jnp.arange(rows.shape[4]) < t // CHUNK)[:, None]
    i_pre = rows[:, 0]
    log_f = jnp.where(valid, jax.nn.log_sigmoid(rows[:, 1]), 0.0)
    h = mlstm_scan(qk, z, i_pre, log_f)
    return h, w_out, (MLSTM_DV, True, _sigmoid), (z, 2, norm_g)


def _modulated(x, mods, sh_idx, sc_idx):
    modl, modc = mods
    d = D_MODEL
    is_ctx = (jnp.arange(x.shape[1]) < CTX_LEN)[None, :, None]
    sh_c, sh_l = modc[:, :, sh_idx * d:(sh_idx + 1) * d], modl[:, :, sh_idx * d:(sh_idx + 1) * d]
    sc_c, sc_l = modc[:, :, sc_idx * d:(sc_idx + 1) * d], modl[:, :, sc_idx * d:(sc_idx + 1) * d]
    return x * (1.0 + jnp.where(is_ctx, sc_c, sc_l)) + jnp.where(is_ctx, sh_c, sh_l)


def _token_shift(a):
    ap = jnp.pad(a, ((0, 0), (1, 1), (0, 0)))
    return 0.5 * (ap[:, :-2] + ap[:, 2:])


RWKV_IN_COLS = 3584


def _rwkv_in_kernel(x_ref, modl_ref, modc_ref, w_ref, o_ref, hd_s):
    t = x_ref.shape[1]

    @pl.when(pl.program_id(1) == 0)
    def _():
        row = _iota2(t, 1, 0)
        is_ctx = row < CTX_LEN
        sel = lambda i: jnp.where(is_ctx, modc_ref[0, :, i * D_MODEL:(i + 1) * D_MODEL],
                                  modl_ref[0, :, i * D_MODEL:(i + 1) * D_MODEL])
        h = x_ref[0] * (1.0 + sel(1)) + sel(0)
        seg_first = (row == 0) | (row == CTX_LEN)
        seg_last = (row == CTX_LEN - 1) | (row == t - 1)
        h_prev = jnp.where(seg_first, 0.0, pltpu.roll(h, 1, axis=0))
        h_next = jnp.where(seg_last, 0.0, pltpu.roll(h, t - 1, axis=0))
        hd_s[:, :D_MODEL] = h.astype(BF16)
        hd_s[:, D_MODEL:] = (0.5 * (h_prev + h_next) - h).astype(BF16)

    for i in range(t // TOKEN_TILE):
        rows = slice(i * TOKEN_TILE, (i + 1) * TOKEN_TILE)
        o_ref[0, rows, :] = jnp.dot(hd_s[rows, :], w_ref[...], preferred_element_type=F32)


def rwkv_in_proj(x, mods, w_big):
    b, t, _ = x.shape
    modl, modc = mods
    n = w_big.shape[1]
    return pl.pallas_call(
        _rwkv_in_kernel,
        grid=(b, n // N_CHUNK_COLS),
        in_specs=[pl.BlockSpec((1, t, D_MODEL), lambda bi, ji: (bi, 0, 0)),
                  pl.BlockSpec((1, 1, 6 * D_MODEL), lambda bi, ji: (bi, 0, 0)),
                  pl.BlockSpec((1, 1, 6 * D_MODEL), lambda bi, ji: (0, 0, 0)),
                  pl.BlockSpec((2 * D_MODEL, N_CHUNK_COLS), lambda bi, ji: (0, ji))],
        out_specs=pl.BlockSpec((1, t, N_CHUNK_COLS), lambda bi, ji: (bi, 0, ji)),
        out_shape=jax.ShapeDtypeStruct((b, t, n), F32),
        scratch_shapes=[pltpu.VMEM((t, 2 * D_MODEL), BF16)],
        compiler_params=_cparams(("parallel", "arbitrary")),
        name="rwkv_in_proj",
    )(x, modl, modc, w_big)


def _seg64_sums(x):
    left = _iota2(1, 128, 1) < RWKV_HEAD
    parts = []
    for blk_i in range(x.shape[1] // 128):
        blk = x[:, blk_i * 128:(blk_i + 1) * 128]
        s_l = jnp.sum(jnp.where(left, blk, 0.0), axis=1, keepdims=True)
        s_r = jnp.sum(jnp.where(left, 0.0, blk), axis=1, keepdims=True)
        parts.append(jnp.where(left, s_l, s_r))
    return jnp.concatenate(parts, axis=1)


def _softplus(x):
    return jnp.maximum(x, 0.0) + jnp.log(1.0 + jnp.exp(-jnp.abs(x)))


def _rwkv_mid_kernel(z_ref, w2_ref, a2_ref, g2_ref, vec_ref, r_ref, v_ref, kk_ref, lw0_ref, lw1_ref,
                     kd0_ref, kd1_ref, bv0_ref, bv1_ref, g_ref):
    d = D_MODEL
    z = z_ref[0]
    r, k, v = z[:, :d], z[:, d:2 * d], z[:, 2 * d:3 * d]
    zg, zw, za = z[:, 3 * d:3 * d + 128], z[:, 3 * d + 128:3 * d + 256], z[:, 3 * d + 256:3 * d + 384]
    vec = vec_ref[...]
    k_k, k_a = vec[4:5], vec[5:6]
    r_ref[0] = r
    v_ref[0] = v
    g_ref[0] = _mm(_sigmoid(zg), g2_ref[...])
    kx = k * k_k
    kk = kx * lax.rsqrt(_seg64_sums(kx * kx) + 1e-6)
    kk_ref[0] = kk
    tw = jnp.tanh(zw)
    for dr, (lw_ref, kd_ref, bv_ref) in enumerate(((lw0_ref, kd0_ref, bv0_ref), (lw1_ref, kd1_ref, bv1_ref))):
        w_raw = -_softplus(-(vec[dr:dr + 1] + _mm(tw, w2_ref[dr]))) - 0.5
        a = _sigmoid(vec[2 + dr:3 + dr] + _mm(za, a2_ref[dr]))
        lw_ref[0] = -jnp.exp(w_raw)
        kd_ref[0] = k * (1.0 + (a - 1.0) * k_a)
        bv_ref[0] = kk * a


def rwkv_mid(z1, w2p, a2p, g2, vec):
    b, t, n = z1.shape
    tok = pl.BlockSpec((1, TOKEN_TILE, D_MODEL), lambda bi, ti: (bi, ti, 0))
    full = lambda a: pl.BlockSpec(a.shape, lambda bi, ti: (0,) * a.ndim)
    return pl.pallas_call(
        _rwkv_mid_kernel,
        grid=(b, t // TOKEN_TILE),
        in_specs=[pl.BlockSpec((1, TOKEN_TILE, n), lambda bi, ti: (bi, ti, 0)),
                  full(w2p), full(a2p), full(g2), full(vec)],
        out_specs=[tok] * 10,
        out_shape=[jax.ShapeDtypeStruct((b, t, D_MODEL), F32)] * 10,
        compiler_params=_cparams(("parallel", "parallel")),
        name="rwkv_mid",
    )(z1, w2p, a2p, g2, vec)


def _rwkv_post(y, r, v, kd0, kd1, g, params):
    inv = 1.0 / RWKV_HEAD
    yc = y - _seg64_sums(y) * inv
    yn = yc * lax.rsqrt(_seg64_sums(yc * yc) * inv + RWKV_GN_EPS)
    bonus = _seg64_sums(r * (kd0 + kd1) * params[2:3]) * v
    return (yn * params[0:1] + params[1:2] + bonus) * g


def rwkv_layer(x, mods, mu, w_rkv, w0, w1, w2, a0, a1, a2, g1, g2, k_k, k_a, r_k, lnx_g, lnx_b, w_out):
    d = D_MODEL
    cols = [(w_rkv[0], 0), (w_rkv[1], 2), (w_rkv[2], 3), (g1, 5), (w1[0], 1), (w1[1], 1), (a1[0], 4), (a1[1], 4)]
    top = jnp.concatenate([w for w, _ in cols], axis=1)
    bot = jnp.concatenate([mu[j][:, None] * w for w, j in cols], axis=1)
    w_big = jnp.pad(jnp.concatenate([top, bot], axis=0), ((0, 0), (0, RWKV_IN_COLS - top.shape[1]))).astype(BF16)
    z1 = rwkv_in_proj(x, mods, w_big)
    pad_dir = lambda w: jnp.stack([jnp.pad(w[0], ((0, w.shape[1]), (0, 0))), jnp.pad(w[1], ((w.shape[1], 0), (0, 0)))])
    vec = jnp.concatenate([w0, a0, k_k[None], k_a[None], jnp.zeros((2, d), F32)], axis=0)
    r, v, kk, lw0, lw1, kd0, kd1, bv0, bv1, g = rwkv_mid(z1, pad_dir(w2), pad_dir(a2), g2, vec)
    y = rwkv_scan(r, v, kk, (lw0, lw1), (kd0, kd1), (bv0, bv1))
    params = jnp.concatenate([lnx_g[None], lnx_b[None], r_k.reshape(1, d), jnp.zeros((5, d), F32)], axis=0)
    return y, w_out, "rwkv", (r, v, kd0, kd1, g, params)


def kernel(x, c, ctx, c_ctx, ada_w, ada_b, ln_g, ln_b, router_w, router_b, moe_w1, moe_w3, moe_w2, gdn_w_in, gdn_conv, gdn_a_log, gdn_dt_bias, gdn_norm_g, gdn_w_out, mlstm_w_in, mlstm_gate_b, mlstm_norm_g, mlstm_w_out, rwkv_mu, rwkv_w_rkv, rwkv_w0, rwkv_w1, rwkv_w2, rwkv_a0, rwkv_a1, rwkv_a2, rwkv_g1, rwkv_g2, rwkv_k_k, rwkv_k_a, rwkv_r_k, rwkv_lnx_g, rwkv_lnx_b, rwkv_w_out, na_w_in, na_rpb, na_w_out):
    b = x.shape[0]
    ctx_tiles = CTX_LEN // TOKEN_TILE
    mod_all = modulation_all(c, c_ctx, ada_w, ada_b)
    xs = jnp.concatenate([ctx, x], axis=1)
    assert DEPTH == 4
    for i in range(DEPTH):
        mods = (mod_all[i, :b, None, :], mod_all[i, b:b + 1, None, :])
        if i % 4 == 0:
            y, w_out, post, post_args = gdn_layer(xs, mods, gdn_w_in, gdn_conv, gdn_a_log, gdn_dt_bias,
                                                  gdn_norm_g, gdn_w_out)
        elif i % 4 == 1:
            y, w_out, post, post_args = mlstm_layer(xs, mods, mlstm_w_in, mlstm_gate_b, mlstm_norm_g, mlstm_w_out)
        elif i % 4 == 2:
            y, w_out, post, post_args = rwkv_layer(xs, mods, rwkv_mu, rwkv_w_rkv, rwkv_w0, rwkv_w1, rwkv_w2, rwkv_a0, rwkv_a1,
                                  rwkv_a2, rwkv_g1, rwkv_g2, rwkv_k_k, rwkv_k_a, rwkv_r_k, rwkv_lnx_g,
                                  rwkv_lnx_b, rwkv_w_out)
        else:
            z = linear(xs, na_w_in.astype(BF16), mods=mods, modulate=(0, 1))
            y, w_out, post, post_args = na_attention(z, _na_bias_table(na_rpb)), na_w_out, None, None
        off = ctx_tiles if (i % 4 == 3) else 0
        xs1 = out_proj_ln(y, w_out.astype(BF16), xs, mods, ln_g[i, 0], ln_b[i, 0], gate_idx=2, tile_off=off,
                          post=post, post_args=post_args)
        xs = moe_layer(xs1, mods, router_w, router_b, moe_w1, moe_w3, moe_w2, i, ln_g[i, 1], ln_b[i, 1],
                       tile_off=off)
    return xs
```

```python
import functools

import numpy as np
import jax
import jax.numpy as jnp
from jax import lax
from jax.experimental import pallas as pl
from jax.experimental.pallas import tpu as pltpu

F32 = jnp.float32
BF16 = jnp.bfloat16

D_MODEL = 1024
DEPTH = 4
GRID_W = 64
CTX_LEN = 256
ALPHA = (2 * DEPTH) ** 0.25
LN_EPS = 1e-5
ROPE_BASE = 10000.0

GDN_HEADS = 8
GDN_DK = 128
GDN_DV = 128
MLSTM_HEADS = 4
MLSTM_DQK = 128
MLSTM_DV = 256
RWKV_HEAD = 64
RWKV_HEADS = 16
RWKV_GN_EPS = 64e-5
NA_HEADS = 16
NA_DH = 64
NA_WIN_ROWS = 8
NA_WIN_COLS = 16
N_EXPERTS = 16
N_GROUPS = 4
D_EXPERT = 512

CHUNK = 64
GDN_UNROLL = 6
RWKV_UNROLL = 3
NA_ROWS_PER_STEP = 4
MLSTM_HEADS_PER_STEP = 2
TOKEN_TILE = 256
N_CHUNK_COLS = 512
FFN_TILE = 512
NEG = -1e30
VMEM_LIMIT = 56 * 1024 * 1024

_HI = lax.Precision.HIGHEST


def _cparams(sem):
    return pltpu.CompilerParams(dimension_semantics=sem, vmem_limit_bytes=VMEM_LIMIT)


def _mm(a, b):
    return jnp.dot(a.astype(BF16), b.astype(BF16), preferred_element_type=F32)


def _mm_nt(a, b):
    return lax.dot_general(a.astype(BF16), b.astype(BF16), (((1,), (1,)), ((), ())),
                           preferred_element_type=F32)


def _mm_tn(a, b):
    return lax.dot_general(a.astype(BF16), b.astype(BF16), (((0,), (0,)), ((), ())),
                           preferred_element_type=F32)


def _mmf(a, b):
    return jnp.dot(a, b, preferred_element_type=F32, precision=_HI)


def _silu(x):
    return x * (1.0 / (1.0 + jnp.exp(-x)))


def _sigmoid(x):
    return 1.0 / (1.0 + jnp.exp(-x))


def _mod_kernel(s_ref, w_ref, b_ref, o_ref):
    o_ref[0] = _mmf(_silu(s_ref[...]), w_ref[0]) + b_ref[0]


def modulation_all(c, c_ctx, ada_w, ada_b):
    b = c.shape[0]
    rows = 8 * ((b + 1 + 7) // 8)
    s = jnp.zeros((rows, D_MODEL), F32).at[:b].set(c).at[b].set(c_ctx)
    tn = 1536
    n = ada_w.shape[-1]
    return pl.pallas_call(
        _mod_kernel,
        grid=(DEPTH, n // tn),
        in_specs=[pl.BlockSpec((rows, D_MODEL), lambda i, j: (0, 0)),
                  pl.BlockSpec((1, D_MODEL, tn), lambda i, j: (i, 0, j)),
                  pl.BlockSpec((1, 1, tn), lambda i, j: (i, 0, j))],
        out_specs=pl.BlockSpec((1, rows, tn), lambda i, j: (i, 0, j)),
        out_shape=jax.ShapeDtypeStruct((DEPTH, rows, n), F32),
        compiler_params=_cparams(("arbitrary", "arbitrary")),
        name="adaln_modulation",
    )(s, ada_w, ada_b.reshape(DEPTH, 1, n))


def _mod_rows(modl_ref, modc_ref, is_ctx, idx):
    sl = slice(idx * D_MODEL, (idx + 1) * D_MODEL)
    return jnp.where(is_ctx, modc_ref[0, :, sl], modl_ref[0, :, sl])


def _linear_kernel(*refs, glob_off, modulate, act, n_main, has_small):
    it = iter(refs)
    x_ref = next(it)
    if modulate is not None:
        modl_ref, modc_ref = next(it), next(it)
    w_ref = next(it)
    ws_ref = next(it) if has_small else None
    o_ref = next(it)
    os_ref = next(it) if has_small else None

    h = x_ref[0]
    if modulate is not None:
        is_ctx = (pl.program_id(1) + glob_off) * TOKEN_TILE < CTX_LEN
        sh = _mod_rows(modl_ref, modc_ref, is_ctx, modulate[0])
        sc = _mod_rows(modl_ref, modc_ref, is_ctx, modulate[1])
        h = h * (1.0 + sc) + sh
    if act == "tanh":
        h = jnp.tanh(h)
    elif act == "sigmoid":
        h = _sigmoid(h)
    hb = h.astype(BF16)
    step = min(N_CHUNK_COLS, n_main)
    for j in range(n_main // step):
        o_ref[0, :, j * step:(j + 1) * step] = jnp.dot(
            hb, w_ref[:, j * step:(j + 1) * step], preferred_element_type=F32)
    if has_small:
        os_ref[0] = _mmf(h, ws_ref[...])


def linear(x, w_bf16, *, mods=None, modulate=None, act=None, w_small=None):
    b, t_out, k = x.shape
    n_main = w_bf16.shape[1]
    has_small = w_small is not None
    in_specs = [pl.BlockSpec((1, TOKEN_TILE, k), lambda bi, ti: (bi, ti, 0))]
    args = [x]
    if modulate is not None:
        modl, modc = mods
        in_specs += [pl.BlockSpec((1, 1, 6 * D_MODEL), lambda bi, ti: (bi, 0, 0)),
                     pl.BlockSpec((1, 1, 6 * D_MODEL), lambda bi, ti: (0, 0, 0))]
        args += [modl, modc]
    in_specs.append(pl.BlockSpec((k, n_main), lambda bi, ti: (0, 0)))
    args.append(w_bf16)
    out_specs = [pl.BlockSpec((1, TOKEN_TILE, n_main), lambda bi, ti: (bi, ti, 0))]
    out_shape = [jax.ShapeDtypeStruct((b, t_out, n_main), F32)]
    if has_small:
        ns = w_small.shape[1]
        in_specs.append(pl.BlockSpec((k, ns), lambda bi, ti: (0, 0)))
        args.append(w_small)
        out_specs.append(pl.BlockSpec((1, TOKEN_TILE, ns), lambda bi, ti: (bi, ti, 0)))
        out_shape.append(jax.ShapeDtypeStruct((b, t_out, ns), F32))
    res = pl.pallas_call(
        functools.partial(_linear_kernel, glob_off=0, modulate=modulate, act=act,
                          n_main=n_main, has_small=has_small),
        grid=(b, t_out // TOKEN_TILE),
        in_specs=in_specs, out_specs=out_specs, out_shape=out_shape,
        compiler_params=_cparams(("parallel", "parallel")),
        name="linear",
    )(*args)
    return res if has_small else res[0]


def _layer_norm_rows(r, g, b):
    mu = jnp.mean(r, axis=-1, keepdims=True)
    rc = r - mu
    var = jnp.mean(rc * rc, axis=-1, keepdims=True)
    return rc * lax.rsqrt(var + LN_EPS) * g + b


def _head_post(y, gate, norm_g, post):
    head_w, centre, act = post
    parts = []
    for h in range(y.shape[1] // head_w):
        seg = y[:, h * head_w:(h + 1) * head_w]
        if centre:
            seg = seg - jnp.mean(seg, axis=1, keepdims=True)
        parts.append(seg * lax.rsqrt(jnp.mean(seg * seg, axis=1, keepdims=True) + 1e-6))
    return jnp.concatenate(parts, axis=1) * norm_g * act(gate)


def _out_ln_kernel(*refs, tile_off, gate_idx, post):
    if post is None:
        y_ref, w_ref, x_ref, modl_ref, modc_ref, g_ref, b_ref, o_ref = refs
        y = y_ref[0]
    elif post == "rwkv":
        y_ref, r_ref, v_ref, kd0_ref, kd1_ref, gg_ref, pr_ref = refs[:7]
        w_ref, x_ref, modl_ref, modc_ref, g_ref, b_ref, o_ref = refs[7:]
        y = _rwkv_post(y_ref[0], r_ref[0], v_ref[0], kd0_ref[0], kd1_ref[0], gg_ref[0], pr_ref[...])
    else:
        y_ref, gate_ref, ng_ref, w_ref, x_ref, modl_ref, modc_ref, g_ref, b_ref, o_ref = refs
        y = _head_post(y_ref[0], gate_ref[0], ng_ref[...], post)
    is_ctx = (pl.program_id(1) + tile_off) * TOKEN_TILE < CTX_LEN
    gate = _mod_rows(modl_ref, modc_ref, is_ctx, gate_idx)
    f = jnp.dot(y.astype(BF16), w_ref[...], preferred_element_type=F32)
    r = ALPHA * x_ref[0] + gate * f
    o_ref[0] = _layer_norm_rows(r, g_ref[...], b_ref[...])


def out_proj_ln(y, w_bf16, x, mods, ln_g, ln_b, *, gate_idx, tile_off=0, post=None, post_args=None):
    b, t_y, k = y.shape
    modl, modc = mods
    tok = pl.BlockSpec((1, TOKEN_TILE, k), lambda bi, ti: (bi, ti, 0))
    in_specs, args = [tok], [y]
    if post == "rwkv":
        *tiles, params = post_args
        in_specs += [tok] * len(tiles) + [pl.BlockSpec(params.shape, lambda bi, ti: (0, 0))]
        args += [*tiles, params]
    elif post is not None:
        z, gate_block, norm_g = post_args
        in_specs += [pl.BlockSpec((1, TOKEN_TILE, k), lambda bi, ti: (bi, ti, gate_block)),
                     pl.BlockSpec((1, k), lambda bi, ti: (0, 0))]
        args += [z, norm_g.reshape(1, k)]
    in_specs += [pl.BlockSpec((k, D_MODEL), lambda bi, ti: (0, 0)),
                 pl.BlockSpec((1, TOKEN_TILE, D_MODEL), lambda bi, ti: (bi, ti + tile_off, 0)),
                 pl.BlockSpec((1, 1, 6 * D_MODEL), lambda bi, ti: (bi, 0, 0)),
                 pl.BlockSpec((1, 1, 6 * D_MODEL), lambda bi, ti: (0, 0, 0)),
                 pl.BlockSpec((1, D_MODEL), lambda bi, ti: (0, 0)),
                 pl.BlockSpec((1, D_MODEL), lambda bi, ti: (0, 0))]
    args += [w_bf16, x, modl, modc, ln_g.reshape(1, -1), ln_b.reshape(1, -1)]
    return pl.pallas_call(
        functools.partial(_out_ln_kernel, tile_off=tile_off, gate_idx=gate_idx, post=post),
        grid=(b, t_y // TOKEN_TILE),
        in_specs=in_specs,
        out_specs=pl.BlockSpec((1, TOKEN_TILE, D_MODEL), lambda bi, ti: (bi, ti, 0)),
        out_shape=jax.ShapeDtypeStruct((b, t_y, D_MODEL), F32),
        compiler_params=_cparams(("parallel", "parallel")),
        name="out_proj_ln",
    )(*args)


def _seq_prep_kernel(z_ref, cos_ref, sin_ref, cw_ref, o_ref, *, conv, norm_blocks, rope_blocks, scale_blocks, scale):
    x = z_ref[0]
    t = x.shape[0]
    j = pl.program_id(1)
    if conv:
        row = _iota2(t, 1, 0)
        seg_first = (row == 0) | (row == CTX_LEN)
        seg_last = (row == CTX_LEN - 1) | (row == t - 1)
        x_prev = jnp.where(seg_first, 0.0, pltpu.roll(x, 1, axis=0))
        x_next = jnp.where(seg_last, 0.0, pltpu.roll(x, t - 1, axis=0))
        w = cw_ref[...]
        x = _silu(x_prev * w[0:1] + x * w[1:2] + x_next * w[2:3])
    if norm_blocks:
        normed = x * lax.rsqrt(jnp.sum(x * x, axis=1, keepdims=True) + 1e-6)
        x = jnp.where(j < norm_blocks, normed, x)
    lane = _iota2(1, x.shape[1], 1)
    partner = jnp.where((lane % 64) < 32, pltpu.roll(x, 96, axis=1), pltpu.roll(x, 32, axis=1))
    roped = x * cos_ref[...] + partner * sin_ref[...]
    x = jnp.where(j < rope_blocks, roped, x)
    o_ref[0] = jnp.where(j < scale_blocks, x * scale, x)


def _rope_lane_tables(t):
    pos = jnp.arange(t - CTX_LEN)
    inv_freq = ROPE_BASE ** (-jnp.arange(32, dtype=F32) / 32)
    ang_r = (pos // GRID_W).astype(F32)[:, None] * inv_freq[None, :]
    ang_c = (pos % GRID_W).astype(F32)[:, None] * inv_freq[None, :]
    cos = jnp.concatenate([jnp.cos(ang_r), jnp.cos(ang_r), jnp.cos(ang_c), jnp.cos(ang_c)], axis=1)
    sin = jnp.concatenate([-jnp.sin(ang_r), jnp.sin(ang_r), -jnp.sin(ang_c), jnp.sin(ang_c)], axis=1)
    pad = lambda a, v: jnp.concatenate([jnp.full((CTX_LEN, 128), v, F32), a], axis=0)
    return pad(cos, 1.0), pad(sin, 0.0)


def seq_prep(z, n_blocks, *, conv_w=None, norm_blocks, rope_blocks, scale_blocks, scale):
    b, t, _ = z.shape
    cos, sin = _rope_lane_tables(t)
    conv = conv_w is not None
    cw = conv_w if conv else jnp.zeros((3, n_blocks * 128), F32)
    return pl.pallas_call(
        functools.partial(_seq_prep_kernel, conv=conv, norm_blocks=norm_blocks, rope_blocks=rope_blocks,
                          scale_blocks=scale_blocks, scale=scale),
        grid=(b, n_blocks),
        in_specs=[pl.BlockSpec((1, t, 128), lambda bi, ji: (bi, 0, ji)),
                  pl.BlockSpec((t, 128), lambda bi, ji: (0, 0)),
                  pl.BlockSpec((t, 128), lambda bi, ji: (0, 0)),
                  pl.BlockSpec((3, 128), lambda bi, ji: (0, ji))],
        out_specs=pl.BlockSpec((1, t, 128), lambda bi, ji: (bi, 0, ji)),
        out_shape=jax.ShapeDtypeStruct((b, t, n_blocks * 128), F32),
        compiler_params=_cparams(("parallel", "parallel")),
        name="seq_prep",
    )(z, cos, sin, cw)


def _iota2(n, m, axis):
    return lax.broadcasted_iota(jnp.int32, (n, m), axis)


def _row_to_col(row, eye):
    return jnp.sum(jnp.where(eye, row, 0.0), axis=1, keepdims=True)


def _tri_solve_steps(n_mat, rhs_list, eye_f, blockdiag):
    nd = jnp.where(blockdiag, n_mat, 0.0)
    ne = n_mat - nd
    p = eye_f + nd
    n2 = _mm(nd, nd)
    yield
    p = p + _mm(p, n2)
    n4 = _mm(n2, n2)
    yield
    p = p + _mm(p, n4)
    n8 = _mm(n4, n4)
    yield
    dinv = p + _mm(p, n8)
    yield
    m = _mm(dinv, ne)
    xs = [_mm(dinv, r) for r in rhs_list]
    yield
    m2 = _mm(m, m)
    xs = [x + _mm(m, x) for x in xs]
    yield
    xs = [x + _mm(m2, x) for x in xs]
    yield
    return xs


def _run_interleaved(gens):
    gens = list(gens)
    while gens:
        alive = []
        for g in gens:
            try:
                next(g)
                alive.append(g)
            except StopIteration:
                pass
        gens = alive


def _split2(x):
    hi = x.astype(BF16)
    return hi, (x - hi.astype(F32)).astype(BF16)


def _cumsum_rows(x, cum):
    hi, lo = _split2(x)
    cb = cum.astype(BF16)
    return jnp.dot(hi, cb, preferred_element_type=F32) + jnp.dot(lo, cb, preferred_element_type=F32)


def _cumsum_cols(cum, x):
    hi, lo = _split2(x)
    cb = cum.astype(BF16)
    return jnp.dot(cb, hi, preferred_element_type=F32) + jnp.dot(cb, lo, preferred_element_type=F32)


def _chunk_order(direction, n_ctx, n_tot):
    def order(j):
        if direction == 0:
            return j
        return jnp.where(j < n_ctx, n_ctx - 1 - j, n_tot - 1 - (j - n_ctx))
    return order


def _masks(direction):
    ii = _iota2(CHUNK, CHUNK, 0)
    jj = _iota2(CHUNK, CHUNK, 1)
    if direction == 0:
        incl, strict = jj <= ii, jj < ii
    else:
        incl, strict = jj >= ii, jj > ii
    return ii, jj, incl, strict


def _gdn_kernel(q_ref, k_ref, v_ref, la_ref, be_ref, o_ref, g_s, lhs_s, add_s, s_s, ob_s, *, n_ctx, n_tot):
    ii, jj, _, _ = _masks(0)
    eye = ii == jj
    eye_f = eye.astype(F32)
    blockdiag = (ii // 16) == (jj // 16)
    dirs = []
    for d in (0, 1):
        _, _, incl, strict = _masks(d)
        cum = (ii <= jj).astype(F32) if d == 0 else (ii >= jj).astype(F32)
        g_s[d] = _cumsum_rows(la_ref[d, 0, 0], cum)
        dirs.append((incl, strict, CHUNK - 1 if d == 0 else 0))

    def prep(j, carry):
        loaded = []
        for uu in range(GDN_UNROLL):
            n = j * GDN_UNROLL + uu
            rows = pl.ds(pl.multiple_of(n * CHUNK, CHUNK), CHUNK)
            loaded.append((n, q_ref[0, rows, :], k_ref[0, rows, :], v_ref[0, rows, :],
                           [g_s[d, pl.ds(n, 1), :] for d in (0, 1)],
                           [be_ref[d, 0, 0, pl.ds(n, 1), :] for d in (0, 1)]))
        stores = []
        grams = [(_mm_nt(kc, kc), _mm_nt(qc, kc)) for _, qc, kc, _, _, _ in loaded]

        def chain(n, qc, kc, vc, g_row, be_row, kk, qk, d, incl, strict, last):
            g_col = _row_to_col(g_row, eye)
            be_col = _row_to_col(be_row, eye)
            g_last = g_row[:, last:last + 1]
            gamma = jnp.exp(jnp.where(incl, g_col - g_row, NEG))
            n_mat = jnp.where(strict, -(be_col * kk * gamma), 0.0)
            eg = jnp.exp(g_col)
            rhs = jnp.concatenate([be_col * vc, (be_col * eg) * kc], axis=1)
            (uw,) = yield from _tri_solve_steps(n_mat, [rhs], eye_f, blockdiag)
            p_mat = jnp.where(incl, qk * gamma, 0.0)
            p_uw = _mm(p_mat, uw)
            k_dec = kc * jnp.exp(g_last - g_col)
            k_uw = _mm_tn(k_dec, uw)
            q_t = qc * eg - p_uw[:, GDN_DV:]
            stores.append((d, n, jnp.concatenate([q_t, -k_uw[:, GDN_DV:]], axis=0).astype(BF16),
                           jnp.concatenate([p_uw[:, :GDN_DV], k_uw[:, :GDN_DV]], axis=0)))

        _run_interleaved(
            chain(n, qc, kc, vc, g_rows[d], be_rows[d], kk, qk, d, *dirs[d])
            for (n, qc, kc, vc, g_rows, be_rows), (kk, qk) in zip(loaded, grams) for d in (0, 1))
        for d, n, lhs, add in stores:
            lhs_s[d, n] = lhs
            add_s[d, n] = add
        return carry

    lax.fori_loop(0, n_tot // GDN_UNROLL, prep, 0)

    s_s[...] = jnp.zeros_like(s_s)
    orders = [_chunk_order(d, n_ctx, n_tot) for d in (0, 1)]
    o_refs = (o_ref.at[0], ob_s)

    def step(j, carry):
        loaded = []
        for d, (_, _, last) in enumerate(dirs):
            n = orders[d](j)
            loaded.append((n, s_s[d], lhs_s[d, n], add_s[d, n], g_s[d, pl.ds(n, 1), last:last + 1]))
        results = []
        for n, s, lhs, add, g_last in loaded:
            z = jnp.dot(lhs, s.astype(BF16), preferred_element_type=F32) + add
            results.append((n, jnp.exp(g_last) * s + z[CHUNK:], z[:CHUNK]))
        for d, (n, s_new, o) in enumerate(results):
            s_s[d] = s_new
            o_refs[d][pl.ds(pl.multiple_of(n * CHUNK, CHUNK), CHUNK), :] = o
        return carry

    lax.fori_loop(0, n_tot, step, 0)
    o_ref[0] += ob_s[...]


def gdn_scan(qkv, log_alpha, beta):
    b, t, _ = qkv.shape
    n_tot = t // CHUNK
    ncp = log_alpha.shape[3]
    blk = pl.BlockSpec((1, t, GDN_DK), lambda bi, hi: (bi, 0, hi))
    k_blk = pl.BlockSpec((1, t, GDN_DK), lambda bi, hi: (bi, 0, GDN_HEADS + hi))
    v_blk = pl.BlockSpec((1, t, GDN_DV), lambda bi, hi: (bi, 0, 2 * GDN_HEADS + hi))
    gate_blk = pl.BlockSpec((2, 1, 1, ncp, CHUNK), lambda bi, hi: (0, bi, hi, 0, 0))
    return pl.pallas_call(
        functools.partial(_gdn_kernel, n_ctx=CTX_LEN // CHUNK, n_tot=n_tot),
        grid=(b, GDN_HEADS),
        in_specs=[blk, k_blk, v_blk, gate_blk, gate_blk],
        out_specs=blk,
        out_shape=jax.ShapeDtypeStruct((b, t, GDN_HEADS * GDN_DV), F32),
        scratch_shapes=[pltpu.VMEM((2, ncp, CHUNK), F32),
                        pltpu.VMEM((2, n_tot, CHUNK + GDN_DK, GDN_DV), BF16),
                        pltpu.VMEM((2, n_tot, CHUNK + GDN_DK, GDN_DV), F32),
                        pltpu.VMEM((2, GDN_DK, GDN_DV), F32),
                        pltpu.VMEM((t, GDN_DV), F32)],
        compiler_params=_cparams(("parallel", "parallel")),
        name="gdn_scan",
    )(qkv, qkv, qkv, log_alpha, beta)


def _mlstm_kernel(q_ref, k_ref, v_ref, ip_ref, lf_ref, o_ref, b_s, c_s, n_s, m_s, ob_s, *, n_ctx, n_tot):
    ii, jj, _, _ = _masks(0)
    eye = ii == jj
    hps = MLSTM_HEADS_PER_STEP
    dirs = []
    for d in (0, 1):
        _, _, incl, _ = _masks(d)
        cum = (ii <= jj).astype(F32) if d == 0 else (ii >= jj).astype(F32)
        for hh in range(hps):
            b_s[d, hh] = _cumsum_rows(lf_ref[d, 0, hh], cum)
        dirs.append((incl, CHUNK - 1 if d == 0 else 0))
    c_s[...] = jnp.zeros_like(c_s)
    n_s[...] = jnp.zeros_like(n_s)
    m_s[...] = jnp.zeros_like(m_s)
    orders = [_chunk_order(d, n_ctx, n_tot) for d in (0, 1)]
    o_refs = (o_ref.at[0], ob_s)

    def step(j, carry):
        loaded = {}
        for d in (0, 1):
            n = orders[d](j)
            rows = pl.ds(pl.multiple_of(n * CHUNK, CHUNK), CHUNK)
            for hh in range(hps):
                qsl = slice(hh * MLSTM_DQK, (hh + 1) * MLSTM_DQK)
                vsl = slice(hh * MLSTM_DV, (hh + 1) * MLSTM_DV)
                loaded[(d, hh)] = (q_ref[0, rows, qsl], k_ref[0, rows, qsl], v_ref[0, rows, vsl],
                                   b_s[d, hh, pl.ds(n, 1), :], ip_ref[d, 0, hh, pl.ds(n, 1), :],
                                   c_s[d, hh], n_s[d, hh], m_s[d, hh], rows, vsl)
        results = {}

        def chain(key, qc, kc, vc, b_row, ip_row, c_st, n_st, m_st, incl, last):
            b_col = _row_to_col(b_row, eye)
            b_last = b_row[:, last:last + 1]
            log_d = jnp.where(incl, b_col - b_row + ip_row, NEG)
            m_intra = jnp.max(log_d, axis=1, keepdims=True)
            qk = _mm_nt(qc, kc)
            qc_st = _mm(qc, c_st)
            log_end = b_last - b_row + ip_row
            m_end = jnp.max(log_end, axis=1, keepdims=True)
            m_row = jnp.maximum(b_col + m_st, m_intra)
            w_state = jnp.exp(b_col + m_st - m_row)
            m_new = jnp.maximum(b_last + m_st, m_end)
            decay = jnp.exp(b_last + m_st - m_new)
            k_w = kc * _row_to_col(jnp.exp(log_end - m_new), eye)
            c_new = decay * c_st + _mm_tn(k_w, vc)
            n_new = decay * n_st + jnp.sum(k_w, axis=0, keepdims=True)
            yield
            w_intra = jnp.exp(log_d - m_row) * qk
            num = w_state * qc_st + _mm(w_intra, vc)
            den = (w_state * jnp.sum(qc * n_st, axis=1, keepdims=True)
                   + jnp.sum(w_intra, axis=1, keepdims=True))
            yield
            results[key] = (num / jnp.maximum(jnp.abs(den), jnp.exp(-m_row)), c_new, n_new, m_new)

        _run_interleaved(chain(key, *vals[:8], *dirs[key[0]]) for key, vals in loaded.items())
        for (d, hh), (h, c_new, n_new, m_new) in results.items():
            rows, vsl = loaded[(d, hh)][8:]
            c_s[d, hh] = c_new
            n_s[d, hh] = n_new
            m_s[d, hh] = m_new
            o_refs[d][rows, vsl] = h
        return carry

    lax.fori_loop(0, n_tot, step, 0)
    o_ref[0] += ob_s[...]


def mlstm_scan(qk, z, i_pre, log_f):
    b, t, _ = qk.shape
    n_tot = t // CHUNK
    ncp = i_pre.shape[3]
    hps = MLSTM_HEADS_PER_STEP
    n_steps = MLSTM_HEADS // hps
    v_off = 2 * MLSTM_HEADS * MLSTM_DQK // (hps * MLSTM_DV)
    q_blk = pl.BlockSpec((1, t, hps * MLSTM_DQK), lambda bi, hi: (bi, 0, hi))
    k_blk = pl.BlockSpec((1, t, hps * MLSTM_DQK), lambda bi, hi: (bi, 0, n_steps + hi))
    v_blk = pl.BlockSpec((1, t, hps * MLSTM_DV), lambda bi, hi: (bi, 0, v_off + hi))
    o_blk = pl.BlockSpec((1, t, hps * MLSTM_DV), lambda bi, hi: (bi, 0, hi))
    gate_blk = pl.BlockSpec((2, 1, hps, ncp, CHUNK), lambda bi, hi: (0, bi, hi, 0, 0))
    return pl.pallas_call(
        functools.partial(_mlstm_kernel, n_ctx=CTX_LEN // CHUNK, n_tot=n_tot),
        grid=(b, n_steps),
        in_specs=[q_blk, k_blk, v_blk, gate_blk, gate_blk],
        out_specs=o_blk,
        out_shape=jax.ShapeDtypeStruct((b, t, MLSTM_HEADS * MLSTM_DV), F32),
        scratch_shapes=[pltpu.VMEM((2, hps, ncp, CHUNK), F32),
                        pltpu.VMEM((2, hps, MLSTM_DQK, MLSTM_DV), F32),
                        pltpu.VMEM((2, hps, 1, MLSTM_DQK), F32),
                        pltpu.VMEM((2, hps, 1, 1), F32),
                        pltpu.VMEM((t, hps * MLSTM_DV), F32)],
        compiler_params=_cparams(("parallel", "parallel")),
        name="mlstm_scan",
    )(qk, qk, z, i_pre, log_f)


def _rwkv_kernel(r_ref, v_ref, kk_ref, lw0_ref, lw1_ref, kd0_ref, kd1_ref, bv0_ref, bv1_ref, o_ref,
                 lhs_s, add_s, gl_s, s_s, ob_s, *, n_ctx, n_tot):
    dir_refs = ((lw0_ref, kd0_ref, bv0_ref), (lw1_ref, kd1_ref, bv1_ref))
    ii, jj, _, _ = _masks(0)
    eye = ii == jj
    eye_f = eye.astype(F32)
    blockdiag = (ii // 16) == (jj // 16)
    hd = RWKV_HEAD
    dirs = []
    for d in (0, 1):
        _, _, incl, strict = _masks(d)
        dirs.append((incl, strict, CHUNK - 1 if d == 0 else 0))
    zero = jnp.zeros((hd, hd), F32)

    def prep(j, carry):
        loaded = []
        for uu in range(RWKV_UNROLL):
            n = j * RWKV_UNROLL + uu
            rows = pl.ds(pl.multiple_of(n * CHUNK, CHUNK), CHUNK)
            loaded.append((n, r_ref[0, rows, :], v_ref[0, rows, :], kk_ref[0, rows, :],
                           [tuple(ref[0, rows, :] for ref in dir_refs[d]) for d in (0, 1)]))
        parts = {}

        def chain(key, r, v, kk, lw, kd, bv, gcs, e_end, incl, strict):
            e_neg = jnp.exp(-gcs)
            a_h = -kk * jnp.exp(gcs - lw)
            r_h = r * jnp.exp(gcs)
            b_h = bv * e_neg
            k_h = kd * e_neg
            a_ab = jnp.where(strict, _mm_nt(a_h, b_h), 0.0)
            a_ak = jnp.where(strict, _mm_nt(a_h, k_h), 0.0)
            a_rb = jnp.where(incl, _mm_nt(r_h, b_h), 0.0)
            a_rk = jnp.where(incl, _mm_nt(r_h, k_h), 0.0)
            yield
            av = _mm(a_ak, v)
            ta, tav = yield from _tri_solve_steps(a_ab, [a_h, av], eye_f, blockdiag)
            b_g = bv * e_end
            parts[key] = (r_h + _mm(a_rb, ta), _mm_tn(b_g, ta),
                          _mm(a_rb, tav) + _mm(a_rk, v), _mm_tn(b_g, tav) + _mm_tn(kd * e_end, v))

        gens, gls = [], {}
        for uu, (_, r2, v2, kk2, per_dir) in enumerate(loaded):
            for d, (incl, strict, last) in enumerate(dirs):
                lw2, kd2, bv2 = per_dir[d]
                gcs2 = _cumsum_cols(incl.astype(F32), lw2)
                gl_row2 = gcs2[last:last + 1, :]
                e_end2 = jnp.exp(gl_row2 - gcs2)
                gls[(uu, d)] = jnp.exp(gl_row2)
                for hh in range(2):
                    sl = slice(hh * hd, (hh + 1) * hd)
                    gens.append(chain((uu, d, hh), *(a[:, sl] for a in (r2, v2, kk2, lw2, kd2, bv2, gcs2, e_end2)),
                                      incl, strict))
        _run_interleaved(gens)
        for uu, (n, _, _, _, _) in enumerate(loaded):
            for d in (0, 1):
                (rt0, mx0, yc0, kv0), (rt1, mx1, yc1, kv1) = parts[(uu, d, 0)], parts[(uu, d, 1)]
                stack = lambda t0, t1, b0, b1: jnp.concatenate(
                    [jnp.concatenate([t0, t1], axis=1), jnp.concatenate([b0, zero], axis=1),
                     jnp.concatenate([zero, b1], axis=1)], axis=0)
                lhs_s[d, n] = stack(rt0, rt1, mx0, mx1).astype(BF16)
                add_s[d, n] = stack(yc0, yc1, kv0, kv1)
                gl_s[d, pl.ds(n, 1), :] = gls[(uu, d)]
        return carry

    lax.fori_loop(0, n_tot // RWKV_UNROLL, prep, 0)

    s_s[...] = jnp.zeros_like(s_s)
    orders = [_chunk_order(d, n_ctx, n_tot) for d in (0, 1)]
    eye2 = _iota2(2 * hd, 2 * hd, 0) == _iota2(2 * hd, 2 * hd, 1)
    o_refs = (o_ref.at[0], ob_s)

    def step(j, carry):
        loaded = []
        for d in (0, 1):
            n = orders[d](j)
            loaded.append((n, s_s[d], lhs_s[d, n], add_s[d, n], gl_s[d, pl.ds(n, 1), :]))
        results = []
        for n, s, lhs, add, gl in loaded:
            z = jnp.dot(lhs, s.astype(BF16), preferred_element_type=F32) + add
            results.append((n, _row_to_col(gl, eye2) * s + z[CHUNK:], z[:CHUNK]))
        for d, (n, s_new, y) in enumerate(results):
            s_s[d] = s_new
            o_refs[d][pl.ds(pl.multiple_of(n * CHUNK, CHUNK), CHUNK), :] = y
        return carry

    lax.fori_loop(0, n_tot, step, 0)
    o_ref[0] += ob_s[...]


def rwkv_scan(r, v, kk, log_w, k_dir, b_dir):
    b, t, _ = r.shape
    n_tot = t // CHUNK
    blk = pl.BlockSpec((1, t, 128), lambda bi, hi: (bi, 0, hi))
    return pl.pallas_call(
        functools.partial(_rwkv_kernel, n_ctx=CTX_LEN // CHUNK, n_tot=n_tot),
        grid=(b, RWKV_HEADS // 2),
        in_specs=[blk] * 9,
        out_specs=blk,
        out_shape=jax.ShapeDtypeStruct((b, t, D_MODEL), F32),
        scratch_shapes=[pltpu.VMEM((2, n_tot, CHUNK + 2 * RWKV_HEAD, 2 * RWKV_HEAD), BF16),
                        pltpu.VMEM((2, n_tot, CHUNK + 2 * RWKV_HEAD, 2 * RWKV_HEAD), F32),
                        pltpu.VMEM((2, 8 * ((n_tot + 7) // 8), 2 * RWKV_HEAD), F32),
                        pltpu.VMEM((2, 2 * RWKV_HEAD, 2 * RWKV_HEAD), F32),
                        pltpu.VMEM((t, 2 * RWKV_HEAD), F32)],
        compiler_params=_cparams(("parallel", "parallel")),
        name="rwkv_scan",
    )(r, v, kk, *log_w, *k_dir, *b_dir)


def _na_kernel(q_ref, k_ref, v_ref, bias_ref, o_ref, *, rows):
    scale = NA_DH ** -0.5
    slab = NA_WIN_ROWS * GRID_W
    lane = _iota2(1, 2 * NA_DH, 1)
    head_masks = (lane < NA_DH, lane >= NA_DH)
    kc2 = k_ref[0, 0:CTX_LEN, :].astype(BF16)
    vc2 = v_ref[0, 0:CTX_LEN, :].astype(BF16)

    def body(j, carry):
        loaded = []
        for uu in range(NA_ROWS_PER_STEP):
            r = j * NA_ROWS_PER_STEP + uu
            r0 = jnp.clip(r - NA_WIN_ROWS // 2, 0, rows - NA_WIN_ROWS)
            dr0 = r0 - r + NA_WIN_ROWS - 1
            krows = pl.ds(pl.multiple_of(CTX_LEN + r0 * GRID_W, GRID_W), slab)
            loaded.append((r, q_ref[0, pl.ds(pl.multiple_of(CTX_LEN + r * GRID_W, GRID_W), GRID_W), :],
                           k_ref[0, krows, :].astype(BF16), v_ref[0, krows, :].astype(BF16),
                           [bias_ref[hh, pl.ds(dr0, 1)][0] for hh in range(2)]))
        outs = {}

        def chain(key, q2, ks2, vs2, bias, mask):
            qh = jnp.where(mask, q2, 0.0)
            s_lat = _mm_nt(qh, ks2) * scale + bias
            s_ctx = _mm_nt(qh, kc2) * scale
            yield
            m = jnp.maximum(jnp.max(s_lat, axis=1, keepdims=True), jnp.max(s_ctx, axis=1, keepdims=True))
            p_lat = jnp.exp(s_lat - m)
            p_ctx = jnp.exp(s_ctx - m)
            l = jnp.sum(p_lat, axis=1, keepdims=True) + jnp.sum(p_ctx, axis=1, keepdims=True)
            outs[key] = (_mm(p_lat, vs2) + _mm(p_ctx, vc2)) / l
            yield

        _run_interleaved(chain((uu, hh), q2, ks2, vs2, biases[hh], head_masks[hh])
                         for uu, (_, q2, ks2, vs2, biases) in enumerate(loaded) for hh in range(2))
        for uu, (r, _, _, _, _) in enumerate(loaded):
            o_ref[0, pl.ds(pl.multiple_of(r * GRID_W, GRID_W), GRID_W), :] = jnp.where(
                head_masks[0], outs[(uu, 0)], outs[(uu, 1)])
        return carry

    lax.fori_loop(0, rows // NA_ROWS_PER_STEP, body, 0)


def _na_bias_table(rpb):
    cols = np.arange(GRID_W)
    win_c0 = np.clip(cols - NA_WIN_COLS // 2, 0, GRID_W - NA_WIN_COLS)
    kc = np.arange(GRID_W)
    in_win = (kc[None, :] >= win_c0[:, None]) & (kc[None, :] < win_c0[:, None] + NA_WIN_COLS)
    dc = np.clip(kc[None, :] - cols[:, None] + NA_WIN_COLS - 1, 0, 2 * NA_WIN_COLS - 2)
    dr = np.arange(NA_WIN_ROWS)[:, None] + np.arange(NA_WIN_ROWS)[None, :]
    tab = rpb.astype(F32)[:, dr][:, :, :, dc]
    tab = jnp.where(in_win[None, None, None], tab, NEG)
    tab = tab.transpose(0, 1, 3, 2, 4)
    return tab.reshape(NA_HEADS, NA_WIN_ROWS, GRID_W, NA_WIN_ROWS * GRID_W)


def na_attention(z, bias_tab):
    b, t, _ = z.shape
    t_lat = t - CTX_LEN
    n_pairs = NA_HEADS // 2
    return pl.pallas_call(
        functools.partial(_na_kernel, rows=t_lat // GRID_W),
        grid=(n_pairs, b),
        in_specs=[pl.BlockSpec((1, t, 128), lambda hi, bi: (bi, 0, hi)),
                  pl.BlockSpec((1, t, 128), lambda hi, bi: (bi, 0, n_pairs + hi)),
                  pl.BlockSpec((1, t, 128), lambda hi, bi: (bi, 0, 2 * n_pairs + hi)),
                  pl.BlockSpec((2, NA_WIN_ROWS, GRID_W, NA_WIN_ROWS * GRID_W), lambda hi, bi: (hi, 0, 0, 0))],
        out_specs=pl.BlockSpec((1, t_lat, 128), lambda hi, bi: (bi, 0, hi)),
        out_shape=jax.ShapeDtypeStruct((b, t_lat, D_MODEL), F32),
        compiler_params=_cparams(("parallel", "parallel")),
        name="na_attention",
    )(z, z, z, bias_tab)


def _router_kernel(x_ref, modl_ref, modc_ref, rw_ref, rb_ref, hb_ref, idx_ref, wt_ref, *, tile_off):
    is_ctx = (pl.program_id(1) + tile_off) * TOKEN_TILE < CTX_LEN
    sh = _mod_rows(modl_ref, modc_ref, is_ctx, 3)
    sc = _mod_rows(modl_ref, modc_ref, is_ctx, 4)
    h = x_ref[0] * (1.0 + sc) + sh
    hb_ref[0] = h.astype(BF16)
    logits = _mmf(h, rw_ref[...])
    mx = jnp.max(logits, axis=1, keepdims=True)
    ex = jnp.exp(logits - mx)
    probs = ex / jnp.sum(ex, axis=1, keepdims=True)
    sel = probs + rb_ref[...]
    tm = sel.shape[0]
    e_id = _iota2(tm, N_EXPERTS, 1)
    per_group = N_EXPERTS // N_GROUPS
    g_id = e_id // per_group

    def top1(vals):
        m1 = jnp.max(vals, axis=1, keepdims=True)
        i1 = jnp.min(jnp.where(vals == m1, e_id, N_EXPERTS), axis=1, keepdims=True)
        return m1, i1

    best_score, best = None, None
    for g in range(N_GROUPS):
        vals = jnp.where(g_id == g, sel, NEG)
        m1, i1 = top1(vals)
        m2, _ = top1(jnp.where(e_id == i1, NEG, vals))
        score = m1 + m2
        if g == 0:
            best_score, best = score, jnp.zeros_like(i1)
        else:
            better = score > best_score
            best = jnp.where(better, g, best)
            best_score = jnp.where(better, score, best_score)
    vals = jnp.where(g_id == best, sel, NEG)
    _, i1 = top1(vals)
    _, i2 = top1(jnp.where(e_id == i1, NEG, vals))
    p1 = jnp.sum(jnp.where(e_id == i1, probs, 0.0), axis=1, keepdims=True)
    p2 = jnp.sum(jnp.where(e_id == i2, probs, 0.0), axis=1, keepdims=True)
    tot = p1 + p2
    slot = _iota2(tm, 2, 1)
    idx_ref[0] = jnp.where(slot == 0, i1, i2)
    wt_ref[0] = jnp.where(slot == 0, p1 / tot, p2 / tot)


def moe_route(x, mods, router_w, router_b, *, tile_off):
    b, t_out, _ = x.shape
    modl, modc = mods
    return pl.pallas_call(
        functools.partial(_router_kernel, tile_off=tile_off),
        grid=(b, t_out // TOKEN_TILE),
        in_specs=[pl.BlockSpec((1, TOKEN_TILE, D_MODEL), lambda bi, ti: (bi, ti, 0)),
                  pl.BlockSpec((1, 1, 6 * D_MODEL), lambda bi, ti: (bi, 0, 0)),
                  pl.BlockSpec((1, 1, 6 * D_MODEL), lambda bi, ti: (0, 0, 0)),
                  pl.BlockSpec((D_MODEL, N_EXPERTS), lambda bi, ti: (0, 0)),
                  pl.BlockSpec((1, N_EXPERTS), lambda bi, ti: (0, 0))],
        out_specs=[pl.BlockSpec((1, TOKEN_TILE, D_MODEL), lambda bi, ti: (bi, ti, 0)),
                   pl.BlockSpec((1, TOKEN_TILE, 2), lambda bi, ti: (bi, ti, 0)),
                   pl.BlockSpec((1, TOKEN_TILE, 2), lambda bi, ti: (bi, ti, 0))],
        out_shape=[jax.ShapeDtypeStruct((b, t_out, D_MODEL), BF16),
                   jax.ShapeDtypeStruct((b, t_out, 2), jnp.int32),
                   jax.ShapeDtypeStruct((b, t_out, 2), F32)],
        compiler_params=_cparams(("parallel", "parallel")),
        name="moe_route",
    )(x, modl, modc, router_w, router_b.reshape(1, -1))


def _ffn_kernel(te_ref, nt_ref, x_ref, w1_ref, w3_ref, w2_ref, o_ref):
    @pl.when(pl.program_id(0) < nt_ref[0])
    def _():
        xb = x_ref[...]
        h1 = jnp.dot(xb, w1_ref[0, 0].astype(BF16), preferred_element_type=F32)
        h3 = jnp.dot(xb, w3_ref[0, 0].astype(BF16), preferred_element_type=F32)
        hid = (_silu(h1) * h3).astype(BF16)
        o_ref[...] = jnp.dot(hid, w2_ref[0, 0].astype(BF16), preferred_element_type=F32).astype(o_ref.dtype)

    @pl.when(pl.program_id(0) >= nt_ref[0])
    def _():
        o_ref[...] = jnp.zeros_like(o_ref)


def expert_ffn(xs, tile_expert, n_tiles_used, w1, w3, w2, layer):
    p = xs.shape[0]
    grid_spec = pltpu.PrefetchScalarGridSpec(
        num_scalar_prefetch=2,
        grid=(p // FFN_TILE,),
        in_specs=[pl.BlockSpec((FFN_TILE, D_MODEL), lambda i, te, nt: (i, 0)),
                  pl.BlockSpec((1, 1, D_MODEL, D_EXPERT), lambda i, te, nt: (layer, te[i], 0, 0)),
                  pl.BlockSpec((1, 1, D_MODEL, D_EXPERT), lambda i, te, nt: (layer, te[i], 0, 0)),
                  pl.BlockSpec((1, 1, D_EXPERT, D_MODEL), lambda i, te, nt: (layer, te[i], 0, 0))],
        out_specs=pl.BlockSpec((FFN_TILE, D_MODEL), lambda i, te, nt: (i, 0)),
    )
    return pl.pallas_call(
        _ffn_kernel,
        grid_spec=grid_spec,
        out_shape=jax.ShapeDtypeStruct((p, D_MODEL), BF16),
        compiler_params=_cparams(("arbitrary",)),
        name="expert_ffn",
    )(tile_expert, n_tiles_used, xs, w1, w3, w2)


def _combine_ln_kernel(x_ref, y_ref, wt_ref, modl_ref, modc_ref, g_ref, b_ref, o_ref, *, tile_off):
    is_ctx = (pl.program_id(1) + tile_off) * TOKEN_TILE < CTX_LEN
    gate = _mod_rows(modl_ref, modc_ref, is_ctx, 5)
    wt = wt_ref[0]
    y = y_ref[0].astype(F32)
    f = wt[:, 0:1] * y[:, :D_MODEL] + wt[:, 1:2] * y[:, D_MODEL:]
    r = ALPHA * x_ref[0] + gate * f
    o_ref[0] = _layer_norm_rows(r, g_ref[...], b_ref[...])


def combine_ln(x, y01, wt, mods, ln_g, ln_b, *, tile_off):
    b, t_out, _ = y01.shape
    modl, modc = mods
    tok = lambda bi, ti: (bi, ti, 0)
    return pl.pallas_call(
        functools.partial(_combine_ln_kernel, tile_off=tile_off),
        grid=(b, t_out // TOKEN_TILE),
        in_specs=[pl.BlockSpec((1, TOKEN_TILE, D_MODEL), tok),
                  pl.BlockSpec((1, TOKEN_TILE, 2 * D_MODEL), tok),
                  pl.BlockSpec((1, TOKEN_TILE, 2), tok),
                  pl.BlockSpec((1, 1, 6 * D_MODEL), lambda bi, ti: (bi, 0, 0)),
                  pl.BlockSpec((1, 1, 6 * D_MODEL), lambda bi, ti: (0, 0, 0)),
                  pl.BlockSpec((1, D_MODEL), lambda bi, ti: (0, 0)),
                  pl.BlockSpec((1, D_MODEL), lambda bi, ti: (0, 0))],
        out_specs=pl.BlockSpec((1, TOKEN_TILE, D_MODEL), tok),
        out_shape=jax.ShapeDtypeStruct((b, t_out, D_MODEL), F32),
        compiler_params=_cparams(("parallel", "parallel")),
        name="moe_combine_ln",
    )(x, y01, wt, modl, modc, ln_g.reshape(1, -1), ln_b.reshape(1, -1))


def moe_layer(x, mods, router_w, router_b, w1, w3, w2, layer, ln_g, ln_b, *, tile_off):
    hb, idx, wt = moe_route(x, mods, router_w, router_b, tile_off=tile_off)
    b, t, _ = hb.shape
    n_tok = b * t
    n_pair = 2 * n_tok
    e_flat = idx.reshape(n_pair)
    onehot = (e_flat[:, None] == jnp.arange(N_EXPERTS)[None, :]).astype(jnp.int32)
    csum = jnp.cumsum(onehot, axis=0)
    counts = csum[-1]
    rank = jnp.sum((csum - onehot) * onehot, axis=1)
    padded = ((counts + FFN_TILE - 1) // FFN_TILE) * FFN_TILE
    ends = jnp.cumsum(padded)
    offs = ends - padded
    pos = offs[e_flat] + rank
    n_rows = n_pair + N_EXPERTS * FFN_TILE
    n_tiles = n_rows // FFN_TILE
    src = jnp.zeros((n_rows,), jnp.int32).at[pos].set(
        jnp.arange(n_pair, dtype=jnp.int32) // 2, mode="promise_in_bounds", unique_indices=True)
    tile_start = jnp.arange(n_tiles, dtype=jnp.int32) * FFN_TILE
    tile_expert = jnp.minimum(jnp.searchsorted(ends, tile_start, side="right"), N_EXPERTS - 1).astype(jnp.int32)
    n_used = (ends[-1] // FFN_TILE).astype(jnp.int32).reshape(1)
    xs = hb.reshape(n_tok, D_MODEL).at[src].get(mode="promise_in_bounds")
    ys = expert_ffn(xs, tile_expert, n_used, w1, w3, w2, layer)
    y01 = ys.at[pos].get(mode="promise_in_bounds", unique_indices=True).reshape(b, t, 2 * D_MODEL)
    return combine_ln(x, y01, wt, mods, ln_g, ln_b, tile_off=tile_off)


def _gate_rows(g):
    b, t = g.shape[:2]
    n = t // CHUNK
    ncp = 8 * ((n + 7) // 8)
    g = g.transpose(2, 3, 0, 4, 1).reshape(2, 2, b, g.shape[-1], n, CHUNK)
    return jnp.pad(g, ((0, 0),) * 4 + ((0, ncp - n), (0, 0)))


def gdn_layer(x, mods, w_in, conv_w, a_log, dt_bias, norm_g, w_out):
    wq = GDN_HEADS * GDN_DK
    n_main = 4 * wq
    z, ab = linear(x, w_in[:, :n_main].astype(BF16), mods=mods, modulate=(0, 1), w_small=w_in[:, n_main:])
    qkv = seq_prep(z, 3 * GDN_HEADS, conv_w=conv_w, norm_blocks=2 * GDN_HEADS, rope_blocks=2 * GDN_HEADS,
                   scale_blocks=GDN_HEADS, scale=GDN_DK ** -0.5)
    b, t, _ = x.shape
    ab = ab.reshape(b, t, 2, 2, GDN_HEADS)
    ab = ab.at[:, :, :, 0].add(dt_bias[None, None])
    rows = _gate_rows(ab)
    log_alpha = -jnp.exp(a_log)[:, None, :, None, None] * jax.nn.softplus(rows[:, 0])
    beta = jax.nn.sigmoid(rows[:, 1])
    valid = (jnp.arange(rows.shape[4]) < t // CHUNK)[:, None]
    log_alpha = jnp.where(valid, log_alpha, 0.0)
    o = gdn_scan(qkv, log_alpha, beta)
    return o, w_out, (GDN_DV, False, _silu), (z, 3, jnp.tile(norm_g, GDN_HEADS))


def mlstm_layer(x, mods, w_in, gate_b, norm_g, w_out):
    wq = MLSTM_HEADS * MLSTM_DQK
    wv = MLSTM_HEADS * MLSTM_DV
    n_main = 2 * wq + 2 * wv
    z, gt = linear(x, w_in[:, :n_main].astype(BF16), mods=mods, modulate=(0, 1), w_small=w_in[:, n_main:])
    qk = seq_prep(z, 2 * MLSTM_HEADS, norm_blocks=0, rope_blocks=2 * MLSTM_HEADS, scale_blocks=MLSTM_HEADS,
                  scale=MLSTM_DQK ** -0.5)
    b, t, _ = x.shape
    gt = gt.reshape(b, t, 2, 2, MLSTM_HEADS) + gate_b[None, None]
    rows = _gate_rows(gt)
    valid = (jnp.arange(rows.shape[4]) < t // CHUNK)[:, None]
    i_pre = rows[:, 0]
    log_f = jnp.where(valid, jax.nn.log_sigmoid(rows[:, 1]), 0.0)
    h = mlstm_scan(qk, z, i_pre, log_f)
    return h, w_out, (MLSTM_DV, True, _sigmoid), (z, 2, norm_g)


RWKV_IN_COLS = 3584


def _rwkv_in_kernel(x_ref, modl_ref, modc_ref, w_ref, o_ref, hd_s):
    t = x_ref.shape[1]

    @pl.when(pl.program_id(1) == 0)
    def _():
        row = _iota2(t, 1, 0)
        is_ctx = row < CTX_LEN
        sel = lambda i: jnp.where(is_ctx, modc_ref[0, :, i * D_MODEL:(i + 1) * D_MODEL],
                                  modl_ref[0, :, i * D_MODEL:(i + 1) * D_MODEL])
        h = x_ref[0] * (1.0 + sel(1)) + sel(0)
        seg_first = (row == 0) | (row == CTX_LEN)
        seg_last = (row == CTX_LEN - 1) | (row == t - 1)
        h_prev = jnp.where(seg_first, 0.0, pltpu.roll(h, 1, axis=0))
        h_next = jnp.where(seg_last, 0.0, pltpu.roll(h, t - 1, axis=0))
        hd_s[:, :D_MODEL] = h.astype(BF16)
        hd_s[:, D_MODEL:] = (0.5 * (h_prev + h_next) - h).astype(BF16)

    for i in range(t // TOKEN_TILE):
        rows = slice(i * TOKEN_TILE, (i + 1) * TOKEN_TILE)
        o_ref[0, rows, :] = jnp.dot(hd_s[rows, :], w_ref[...], preferred_element_type=F32)


def rwkv_in_proj(x, mods, w_big):
    b, t, _ = x.shape
    modl, modc = mods
    n = w_big.shape[1]
    return pl.pallas_call(
        _rwkv_in_kernel,
        grid=(b, n // N_CHUNK_COLS),
        in_specs=[pl.BlockSpec((1, t, D_MODEL), lambda bi, ji: (bi, 0, 0)),
                  pl.BlockSpec((1, 1, 6 * D_MODEL), lambda bi, ji: (bi, 0, 0)),
                  pl.BlockSpec((1, 1, 6 * D_MODEL), lambda bi, ji: (0, 0, 0)),
                  pl.BlockSpec((2 * D_MODEL, N_CHUNK_COLS), lambda bi, ji: (0, ji))],
        out_specs=pl.BlockSpec((1, t, N_CHUNK_COLS), lambda bi, ji: (bi, 0, ji)),
        out_shape=jax.ShapeDtypeStruct((b, t, n), F32),
        scratch_shapes=[pltpu.VMEM((t, 2 * D_MODEL), BF16)],
        compiler_params=_cparams(("parallel", "arbitrary")),
        name="rwkv_in_proj",
    )(x, modl, modc, w_big)


def _seg64_sums(x):
    left = _iota2(1, 128, 1) < RWKV_HEAD
    parts = []
    for blk_i in range(x.shape[1] // 128):
        blk = x[:, blk_i * 128:(blk_i + 1) * 128]
        s_l = jnp.sum(jnp.where(left, blk, 0.0), axis=1, keepdims=True)
        s_r = jnp.sum(jnp.where(left, 0.0, blk), axis=1, keepdims=True)
        parts.append(jnp.where(left, s_l, s_r))
    return jnp.concatenate(parts, axis=1)


def _softplus(x):
    return jnp.maximum(x, 0.0) + jnp.log(1.0 + jnp.exp(-jnp.abs(x)))


def _rwkv_mid_kernel(z_ref, w2_ref, a2_ref, g2_ref, vec_ref, r_ref, v_ref, kk_ref, lw0_ref, lw1_ref,
                     kd0_ref, kd1_ref, bv0_ref, bv1_ref, g_ref):
    d = D_MODEL
    z = z_ref[0]
    r, k, v = z[:, :d], z[:, d:2 * d], z[:, 2 * d:3 * d]
    zg, zw, za = z[:, 3 * d:3 * d + 128], z[:, 3 * d + 128:3 * d + 256], z[:, 3 * d + 256:3 * d + 384]
    vec = vec_ref[...]
    k_k, k_a = vec[4:5], vec[5:6]
    r_ref[0] = r
    v_ref[0] = v
    g_ref[0] = _mm(_sigmoid(zg), g2_ref[...])
    kx = k * k_k
    kk = kx * lax.rsqrt(_seg64_sums(kx * kx) + 1e-6)
    kk_ref[0] = kk
    tw = jnp.tanh(zw)
    for dr, (lw_ref, kd_ref, bv_ref) in enumerate(((lw0_ref, kd0_ref, bv0_ref), (lw1_ref, kd1_ref, bv1_ref))):
        w_raw = -_softplus(-(vec[dr:dr + 1] + _mm(tw, w2_ref[dr]))) - 0.5
        a = _sigmoid(vec[2 + dr:3 + dr] + _mm(za, a2_ref[dr]))
        lw_ref[0] = -jnp.exp(w_raw)
        kd_ref[0] = k * (1.0 + (a - 1.0) * k_a)
        bv_ref[0] = kk * a


def rwkv_mid(z1, w2p, a2p, g2, vec):
    b, t, n = z1.shape
    tok = pl.BlockSpec((1, TOKEN_TILE, D_MODEL), lambda bi, ti: (bi, ti, 0))
    full = lambda a: pl.BlockSpec(a.shape, lambda bi, ti: (0,) * a.ndim)
    return pl.pallas_call(
        _rwkv_mid_kernel,
        grid=(b, t // TOKEN_TILE),
        in_specs=[pl.BlockSpec((1, TOKEN_TILE, n), lambda bi, ti: (bi, ti, 0)),
                  full(w2p), full(a2p), full(g2), full(vec)],
        out_specs=[tok] * 10,
        out_shape=[jax.ShapeDtypeStruct((b, t, D_MODEL), F32)] * 10,
        compiler_params=_cparams(("parallel", "parallel")),
        name="rwkv_mid",
    )(z1, w2p, a2p, g2, vec)


def _rwkv_post(y, r, v, kd0, kd1, g, params):
    inv = 1.0 / RWKV_HEAD
    yc = y - _seg64_sums(y) * inv
    yn = yc * lax.rsqrt(_seg64_sums(yc * yc) * inv + RWKV_GN_EPS)
    bonus = _seg64_sums(r * (kd0 + kd1) * params[2:3]) * v
    return (yn * params[0:1] + params[1:2] + bonus) * g


def rwkv_layer(x, mods, mu, w_rkv, w0, w1, w2, a0, a1, a2, g1, g2, k_k, k_a, r_k, lnx_g, lnx_b, w_out):
    d = D_MODEL
    cols = [(w_rkv[0], 0), (w_rkv[1], 2), (w_rkv[2], 3), (g1, 5), (w1[0], 1), (w1[1], 1), (a1[0], 4), (a1[1], 4)]
    top = jnp.concatenate([w for w, _ in cols], axis=1)
    bot = jnp.concatenate([mu[j][:, None] * w for w, j in cols], axis=1)
    w_big = jnp.pad(jnp.concatenate([top, bot], axis=0), ((0, 0), (0, RWKV_IN_COLS - top.shape[1]))).astype(BF16)
    z1 = rwkv_in_proj(x, mods, w_big)
    pad_dir = lambda w: jnp.stack([jnp.pad(w[0], ((0, w.shape[1]), (0, 0))), jnp.pad(w[1], ((w.shape[1], 0), (0, 0)))])
    vec = jnp.concatenate([w0, a0, k_k[None], k_a[None], jnp.zeros((2, d), F32)], axis=0)
    r, v, kk, lw0, lw1, kd0, kd1, bv0, bv1, g = rwkv_mid(z1, pad_dir(w2), pad_dir(a2), g2, vec)
    y = rwkv_scan(r, v, kk, (lw0, lw1), (kd0, kd1), (bv0, bv1))
    params = jnp.concatenate([lnx_g[None], lnx_b[None], r_k.reshape(1, d), jnp.zeros((5, d), F32)], axis=0)
    return y, w_out, "rwkv", (r, v, kd0, kd1, g, params)


def kernel(x, c, ctx, c_ctx, ada_w, ada_b, ln_g, ln_b, router_w, router_b, moe_w1, moe_w3, moe_w2, gdn_w_in, gdn_conv, gdn_a_log, gdn_dt_bias, gdn_norm_g, gdn_w_out, mlstm_w_in, mlstm_gate_b, mlstm_norm_g, mlstm_w_out, rwkv_mu, rwkv_w_rkv, rwkv_w0, rwkv_w1, rwkv_w2, rwkv_a0, rwkv_a1, rwkv_a2, rwkv_g1, rwkv_g2, rwkv_k_k, rwkv_k_a, rwkv_r_k, rwkv_lnx_g, rwkv_lnx_b, rwkv_w_out, na_w_in, na_rpb, na_w_out):
    b = x.shape[0]
    ctx_tiles = CTX_LEN // TOKEN_TILE
    mod_all = modulation_all(c, c_ctx, ada_w, ada_b)
    xs = jnp.concatenate([ctx, x], axis=1)
    assert DEPTH == 4
    for i in range(DEPTH):
        mods = (mod_all[i, :b, None, :], mod_all[i, b:b + 1, None, :])
        if i % 4 == 0:
            y, w_out, post, post_args = gdn_layer(xs, mods, gdn_w_in, gdn_conv, gdn_a_log, gdn_dt_bias,
                                                  gdn_norm_g, gdn_w_out)
        elif i % 4 == 1:
            y, w_out, post, post_args = mlstm_layer(xs, mods, mlstm_w_in, mlstm_gate_b, mlstm_norm_g, mlstm_w_out)
        elif i % 4 == 2:
            y, w_out, post, post_args = rwkv_layer(xs, mods, rwkv_mu, rwkv_w_rkv, rwkv_w0, rwkv_w1, rwkv_w2, rwkv_a0, rwkv_a1,
                                  rwkv_a2, rwkv_g1, rwkv_g2, rwkv_k_k, rwkv_k_a, rwkv_r_k, rwkv_lnx_g,
                                  rwkv_lnx_b, rwkv_w_out)
        else:
            z = linear(xs, na_w_in.astype(BF16), mods=mods, modulate=(0, 1))
            y, w_out, post, post_args = na_attention(z, _na_bias_table(na_rpb)), na_w_out, None, None
        off = ctx_tiles if (i % 4 == 3) else 0
        xs1 = out_proj_ln(y, w_out.astype(BF16), xs, mods, ln_g[i, 0], ln_b[i, 0], gate_idx=2, tile_off=off,
                          post=post, post_args=post_args)
        xs = moe_layer(xs1, mods, router_w, router_b, moe_w1, moe_w3, moe_w2, i, ln_g[i, 1], ln_b[i, 1],
                       tile_off=off)
    return xs
```

```python
import functools

import numpy as np
import jax
import jax.numpy as jnp
from jax import lax
from jax.experimental import pallas as pl
from jax.experimental.pallas import tpu as pltpu

F32 = jnp.float32
BF16 = jnp.bfloat16

D_MODEL = 1024
DEPTH = 4
GRID_W = 64
CTX_LEN = 256
ALPHA = (2 * DEPTH) ** 0.25
LN_EPS = 1e-5
ROPE_BASE = 10000.0

GDN_HEADS = 8
GDN_DK = 128
GDN_DV = 128
MLSTM_HEADS = 4
MLSTM_DQK = 128
MLSTM_DV = 256
RWKV_HEAD = 64
RWKV_HEADS = 16
RWKV_GN_EPS = 64e-5
NA_HEADS = 16
NA_DH = 64
NA_WIN_ROWS = 8
NA_WIN_COLS = 16
N_EXPERTS = 16
N_GROUPS = 4
D_EXPERT = 512

CHUNK = 64
GDN_UNROLL = 6
RWKV_UNROLL = 3
NA_ROWS_PER_STEP = 4
MLSTM_HEADS_PER_STEP = 2
TOKEN_TILE = 256
N_CHUNK_COLS = 512
FFN_TILE = 512
NEG = -1e30
VMEM_LIMIT = 56 * 1024 * 1024

_HI = lax.Precision.HIGHEST


def _cparams(sem):
    return pltpu.CompilerParams(dimension_semantics=sem, vmem_limit_bytes=VMEM_LIMIT)


def _mm(a, b):
    return jnp.dot(a.astype(BF16), b.astype(BF16), preferred_element_type=F32)


def _mm_nt(a, b):
    return lax.dot_general(a.astype(BF16), b.astype(BF16), (((1,), (1,)), ((), ())),
                           preferred_element_type=F32)


def _mm_tn(a, b):
    return lax.dot_general(a.astype(BF16), b.astype(BF16), (((0,), (0,)), ((), ())),
                           preferred_element_type=F32)


def _mmf(a, b):
    return jnp.dot(a, b, preferred_element_type=F32, precision=_HI)


def _silu(x):
    return x * (1.0 / (1.0 + jnp.exp(-x)))


def _sigmoid(x):
    return 1.0 / (1.0 + jnp.exp(-x))


def _mod_kernel(s_ref, w_ref, b_ref, o_ref):
    o_ref[0] = _mmf(_silu(s_ref[...]), w_ref[0]) + b_ref[0]


def modulation_all(c, c_ctx, ada_w, ada_b):
    b = c.shape[0]
    rows = 8 * ((b + 1 + 7) // 8)
    s = jnp.zeros((rows, D_MODEL), F32).at[:b].set(c).at[b].set(c_ctx)
    tn = 1536
    n = ada_w.shape[-1]
    return pl.pallas_call(
        _mod_kernel,
        grid=(DEPTH, n // tn),
        in_specs=[pl.BlockSpec((rows, D_MODEL), lambda i, j: (0, 0)),
                  pl.BlockSpec((1, D_MODEL, tn), lambda i, j: (i, 0, j)),
                  pl.BlockSpec((1, 1, tn), lambda i, j: (i, 0, j))],
        out_specs=pl.BlockSpec((1, rows, tn), lambda i, j: (i, 0, j)),
        out_shape=jax.ShapeDtypeStruct((DEPTH, rows, n), F32),
        compiler_params=_cparams(("arbitrary", "arbitrary")),
        name="adaln_modulation",
    )(s, ada_w, ada_b.reshape(DEPTH, 1, n))


def _mod_rows(modl_ref, modc_ref, is_ctx, idx):
    sl = slice(idx * D_MODEL, (idx + 1) * D_MODEL)
    return jnp.where(is_ctx, modc_ref[0, :, sl], modl_ref[0, :, sl])


def _linear_kernel(*refs, glob_off, modulate, act, n_main, has_small):
    it = iter(refs)
    x_ref = next(it)
    if modulate is not None:
        modl_ref, modc_ref = next(it), next(it)
    w_ref = next(it)
    ws_ref = next(it) if has_small else None
    o_ref = next(it)
    os_ref = next(it) if has_small else None

    h = x_ref[0]
    if modulate is not None:
        is_ctx = (pl.program_id(1) + glob_off) * TOKEN_TILE < CTX_LEN
        sh = _mod_rows(modl_ref, modc_ref, is_ctx, modulate[0])
        sc = _mod_rows(modl_ref, modc_ref, is_ctx, modulate[1])
        h = h * (1.0 + sc) + sh
    if act == "tanh":
        h = jnp.tanh(h)
    elif act == "sigmoid":
        h = _sigmoid(h)
    hb = h.astype(BF16)
    step = min(N_CHUNK_COLS, n_main)
    for j in range(n_main // step):
        o_ref[0, :, j * step:(j + 1) * step] = jnp.dot(
            hb, w_ref[:, j * step:(j + 1) * step], preferred_element_type=F32)
    if has_small:
        os_ref[0] = _mmf(h, ws_ref[...])


def linear(x, w_bf16, *, mods=None, modulate=None, act=None, w_small=None):
    b, t_out, k = x.shape
    n_main = w_bf16.shape[1]
    has_small = w_small is not None
    in_specs = [pl.BlockSpec((1, TOKEN_TILE, k), lambda bi, ti: (bi, ti, 0))]
    args = [x]
    if modulate is not None:
        modl, modc = mods
        in_specs += [pl.BlockSpec((1, 1, 6 * D_MODEL), lambda bi, ti: (bi, 0, 0)),
                     pl.BlockSpec((1, 1, 6 * D_MODEL), lambda bi, ti: (0, 0, 0))]
        args += [modl, modc]
    in_specs.append(pl.BlockSpec((k, n_main), lambda bi, ti: (0, 0)))
    args.append(w_bf16)
    out_specs = [pl.BlockSpec((1, TOKEN_TILE, n_main), lambda bi, ti: (bi, ti, 0))]
    out_shape = [jax.ShapeDtypeStruct((b, t_out, n_main), F32)]
    if has_small:
        ns = w_small.shape[1]
        in_specs.append(pl.BlockSpec((k, ns), lambda bi, ti: (0, 0)))
        args.append(w_small)
        out_specs.append(pl.BlockSpec((1, TOKEN_TILE, ns), lambda bi, ti: (bi, ti, 0)))
        out_shape.append(jax.ShapeDtypeStruct((b, t_out, ns), F32))
    res = pl.pallas_call(
        functools.partial(_linear_kernel, glob_off=0, modulate=modulate, act=act,
                          n_main=n_main, has_small=has_small),
        grid=(b, t_out // TOKEN_TILE),
        in_specs=in_specs, out_specs=out_specs, out_shape=out_shape,
        compiler_params=_cparams(("parallel", "parallel")),
        name="linear",
    )(*args)
    return res if has_small else res[0]


def _layer_norm_rows(r, g, b):
    mu = jnp.mean(r, axis=-1, keepdims=True)
    rc = r - mu
    var = jnp.mean(rc * rc, axis=-1, keepdims=True)
    return rc * lax.rsqrt(var + LN_EPS) * g + b


def _head_post(y, gate, norm_g, post):
    head_w, centre, act = post
    parts = []
    for h in range(y.shape[1] // head_w):
        seg = y[:, h * head_w:(h + 1) * head_w]
        if centre:
            seg = seg - jnp.mean(seg, axis=1, keepdims=True)
        parts.append(seg * lax.rsqrt(jnp.mean(seg * seg, axis=1, keepdims=True) + 1e-6))
    return jnp.concatenate(parts, axis=1) * norm_g * act(gate)


def _out_ln_kernel(*refs, tile_off, gate_idx, post):
    if post is None:
        y_ref, w_ref, x_ref, modl_ref, modc_ref, g_ref, b_ref, o_ref = refs
        y = y_ref[0]
    elif post == "rwkv":
        y_ref, r_ref, v_ref, kd0_ref, kd1_ref, gg_ref, pr_ref = refs[:7]
        w_ref, x_ref, modl_ref, modc_ref, g_ref, b_ref, o_ref = refs[7:]
        y = _rwkv_post(y_ref[0], r_ref[0], v_ref[0], kd0_ref[0], kd1_ref[0], gg_ref[0], pr_ref[...])
    else:
        y_ref, gate_ref, ng_ref, w_ref, x_ref, modl_ref, modc_ref, g_ref, b_ref, o_ref = refs
        y = _head_post(y_ref[0], gate_ref[0], ng_ref[...], post)
    is_ctx = (pl.program_id(1) + tile_off) * TOKEN_TILE < CTX_LEN
    gate = _mod_rows(modl_ref, modc_ref, is_ctx, gate_idx)
    f = jnp.dot(y.astype(BF16), w_ref[...], preferred_element_type=F32)
    r = ALPHA * x_ref[0] + gate * f
    o_ref[0] = _layer_norm_rows(r, g_ref[...], b_ref[...])


def out_proj_ln(y, w_bf16, x, mods, ln_g, ln_b, *, gate_idx, tile_off=0, post=None, post_args=None):
    b, t_y, k = y.shape
    modl, modc = mods
    tok = pl.BlockSpec((1, TOKEN_TILE, k), lambda bi, ti: (bi, ti, 0))
    in_specs, args = [tok], [y]
    if post == "rwkv":
        *tiles, params = post_args
        in_specs += [tok] * len(tiles) + [pl.BlockSpec(params.shape, lambda bi, ti: (0, 0))]
        args += [*tiles, params]
    elif post is not None:
        z, gate_block, norm_g = post_args
        in_specs += [pl.BlockSpec((1, TOKEN_TILE, k), lambda bi, ti: (bi, ti, gate_block)),
                     pl.BlockSpec((1, k), lambda bi, ti: (0, 0))]
        args += [z, norm_g.reshape(1, k)]
    in_specs += [pl.BlockSpec((k, D_MODEL), lambda bi, ti: (0, 0)),
                 pl.BlockSpec((1, TOKEN_TILE, D_MODEL), lambda bi, ti: (bi, ti + tile_off, 0)),
                 pl.BlockSpec((1, 1, 6 * D_MODEL), lambda bi, ti: (bi, 0, 0)),
                 pl.BlockSpec((1, 1, 6 * D_MODEL), lambda bi, ti: (0, 0, 0)),
                 pl.BlockSpec((1, D_MODEL), lambda bi, ti: (0, 0)),
                 pl.BlockSpec((1, D_MODEL), lambda bi, ti: (0, 0))]
    args += [w_bf16, x, modl, modc, ln_g.reshape(1, -1), ln_b.reshape(1, -1)]
    return pl.pallas_call(
        functools.partial(_out_ln_kernel, tile_off=tile_off, gate_idx=gate_idx, post=post),
        grid=(b, t_y // TOKEN_TILE),
        in_specs=in_specs,
        out_specs=pl.BlockSpec((1, TOKEN_TILE, D_MODEL), lambda bi, ti: (bi, ti, 0)),
        out_shape=jax.ShapeDtypeStruct((b, t_y, D_MODEL), F32),
        compiler_params=_cparams(("parallel", "parallel")),
        name="out_proj_ln",
    )(*args)


def _seq_prep_kernel(z_ref, cos_ref, sin_ref, cw_ref, o_ref, *, conv, norm_blocks, rope_blocks, scale_blocks, scale):
    x = z_ref[0]
    t = x.shape[0]
    j = pl.program_id(1)
    if conv:
        row = _iota2(t, 1, 0)
        seg_first = (row == 0) | (row == CTX_LEN)
        seg_last = (row == CTX_LEN - 1) | (row == t - 1)
        x_prev = jnp.where(seg_first, 0.0, pltpu.roll(x, 1, axis=0))
        x_next = jnp.where(seg_last, 0.0, pltpu.roll(x, t - 1, axis=0))
        w = cw_ref[...]
        x = _silu(x_prev * w[0:1] + x * w[1:2] + x_next * w[2:3])
    if norm_blocks:
        normed = x * lax.rsqrt(jnp.sum(x * x, axis=1, keepdims=True) + 1e-6)
        x = jnp.where(j < norm_blocks, normed, x)
    lane = _iota2(1, x.shape[1], 1)
    partner = jnp.where((lane % 64) < 32, pltpu.roll(x, 96, axis=1), pltpu.roll(x, 32, axis=1))
    roped = x * cos_ref[...] + partner * sin_ref[...]
    x = jnp.where(j < rope_blocks, roped, x)
    o_ref[0] = jnp.where(j < scale_blocks, x * scale, x)


def _rope_lane_tables(t):
    pos = jnp.arange(t - CTX_LEN)
    inv_freq = ROPE_BASE ** (-jnp.arange(32, dtype=F32) / 32)
    ang_r = (pos // GRID_W).astype(F32)[:, None] * inv_freq[None, :]
    ang_c = (pos % GRID_W).astype(F32)[:, None] * inv_freq[None, :]
    cos = jnp.concatenate([jnp.cos(ang_r), jnp.cos(ang_r), jnp.cos(ang_c), jnp.cos(ang_c)], axis=1)
    sin = jnp.concatenate([-jnp.sin(ang_r), jnp.sin(ang_r), -jnp.sin(ang_c), jnp.sin(ang_c)], axis=1)
    pad = lambda a, v: jnp.concatenate([jnp.full((CTX_LEN, 128), v, F32), a], axis=0)
    return pad(cos, 1.0), pad(sin, 0.0)


def seq_prep(z, n_blocks, *, conv_w=None, norm_blocks, rope_blocks, scale_blocks, scale):
    b, t, _ = z.shape
    cos, sin = _rope_lane_tables(t)
    conv = conv_w is not None
    cw = conv_w if conv else jnp.zeros((3, n_blocks * 128), F32)
    return pl.pallas_call(
        functools.partial(_seq_prep_kernel, conv=conv, norm_blocks=norm_blocks, rope_blocks=rope_blocks,
                          scale_blocks=scale_blocks, scale=scale),
        grid=(b, n_blocks),
        in_specs=[pl.BlockSpec((1, t, 128), lambda bi, ji: (bi, 0, ji)),
                  pl.BlockSpec((t, 128), lambda bi, ji: (0, 0)),
                  pl.BlockSpec((t, 128), lambda bi, ji: (0, 0)),
                  pl.BlockSpec((3, 128), lambda bi, ji: (0, ji))],
        out_specs=pl.BlockSpec((1, t, 128), lambda bi, ji: (bi, 0, ji)),
        out_shape=jax.ShapeDtypeStruct((b, t, n_blocks * 128), F32),
        compiler_params=_cparams(("parallel", "parallel")),
        name="seq_prep",
    )(z, cos, sin, cw)


def _iota2(n, m, axis):
    return lax.broadcasted_iota(jnp.int32, (n, m), axis)


def _row_to_col(row, eye):
    return jnp.sum(jnp.where(eye, row, 0.0), axis=1, keepdims=True)


def _tri_solve_steps(n_mat, rhs_list, eye_f, blockdiag):
    nd = jnp.where(blockdiag, n_mat, 0.0)
    ne = n_mat - nd
    p = eye_f + nd
    n2 = _mm(nd, nd)
    yield
    p = p + _mm(p, n2)
    n4 = _mm(n2, n2)
    yield
    p = p + _mm(p, n4)
    n8 = _mm(n4, n4)
    yield
    dinv = p + _mm(p, n8)
    yield
    m = _mm(dinv, ne)
    xs = [_mm(dinv, r) for r in rhs_list]
    yield
    m2 = _mm(m, m)
    xs = [x + _mm(m, x) for x in xs]
    yield
    xs = [x + _mm(m2, x) for x in xs]
    yield
    return xs


def _run_interleaved(gens):
    gens = list(gens)
    while gens:
        alive = []
        for g in gens:
            try:
                next(g)
                alive.append(g)
            except StopIteration:
                pass
        gens = alive


def _split2(x):
    hi = x.astype(BF16)
    return hi, (x - hi.astype(F32)).astype(BF16)


def _cumsum_rows(x, cum):
    hi, lo = _split2(x)
    cb = cum.astype(BF16)
    return jnp.dot(hi, cb, preferred_element_type=F32) + jnp.dot(lo, cb, preferred_element_type=F32)


def _cumsum_cols(cum, x):
    hi, lo = _split2(x)
    cb = cum.astype(BF16)
    return jnp.dot(cb, hi, preferred_element_type=F32) + jnp.dot(cb, lo, preferred_element_type=F32)


def _chunk_order(direction, n_ctx, n_tot):
    def order(j):
        if direction == 0:
            return j
        return jnp.where(j < n_ctx, n_ctx - 1 - j, n_tot - 1 - (j - n_ctx))
    return order


def _masks(direction):
    ii = _iota2(CHUNK, CHUNK, 0)
    jj = _iota2(CHUNK, CHUNK, 1)
    if direction == 0:
        incl, strict = jj <= ii, jj < ii
    else:
        incl, strict = jj >= ii, jj > ii
    return ii, jj, incl, strict


def _gdn_kernel(q_ref, k_ref, v_ref, la_ref, be_ref, o_ref, g_s, lhs_s, add_s, s_s, ob_s, *, n_ctx, n_tot):
    ii, jj, _, _ = _masks(0)
    eye = ii == jj
    eye_f = eye.astype(F32)
    blockdiag = (ii // 16) == (jj // 16)
    dirs = []
    for d in (0, 1):
        _, _, incl, strict = _masks(d)
        cum = (ii <= jj).astype(F32) if d == 0 else (ii >= jj).astype(F32)
        g_s[d] = _cumsum_rows(la_ref[d, 0, 0], cum)
        dirs.append((incl, strict, CHUNK - 1 if d == 0 else 0))

    def prep(j, carry):
        loaded = []
        for uu in range(GDN_UNROLL):
            n = j * GDN_UNROLL + uu
            rows = pl.ds(pl.multiple_of(n * CHUNK, CHUNK), CHUNK)
            loaded.append((n, q_ref[0, rows, :], k_ref[0, rows, :], v_ref[0, rows, :],
                           [g_s[d, pl.ds(n, 1), :] for d in (0, 1)],
                           [be_ref[d, 0, 0, pl.ds(n, 1), :] for d in (0, 1)]))
        stores = []
        grams = [(_mm_nt(kc, kc), _mm_nt(qc, kc)) for _, qc, kc, _, _, _ in loaded]

        def chain(n, qc, kc, vc, g_row, be_row, kk, qk, d, incl, strict, last):
            g_col = _row_to_col(g_row, eye)
            be_col = _row_to_col(be_row, eye)
            g_last = g_row[:, last:last + 1]
            gamma = jnp.exp(jnp.where(incl, g_col - g_row, NEG))
            n_mat = jnp.where(strict, -(be_col * kk * gamma), 0.0)
            eg = jnp.exp(g_col)
            rhs = jnp.concatenate([be_col * vc, (be_col * eg) * kc], axis=1)
            (uw,) = yield from _tri_solve_steps(n_mat, [rhs], eye_f, blockdiag)
            p_mat = jnp.where(incl, qk * gamma, 0.0)
            p_uw = _mm(p_mat, uw)
            k_dec = kc * jnp.exp(g_last - g_col)
            k_uw = _mm_tn(k_dec, uw)
            q_t = qc * eg - p_uw[:, GDN_DV:]
            stores.append((d, n, jnp.concatenate([q_t, -k_uw[:, GDN_DV:]], axis=0).astype(BF16),
                           jnp.concatenate([p_uw[:, :GDN_DV], k_uw[:, :GDN_DV]], axis=0)))

        _run_interleaved(
            chain(n, qc, kc, vc, g_rows[d], be_rows[d], kk, qk, d, *dirs[d])
            for (n, qc, kc, vc, g_rows, be_rows), (kk, qk) in zip(loaded, grams) for d in (0, 1))
        for d, n, lhs, add in stores:
            lhs_s[d, n] = lhs
            add_s[d, n] = add
        return carry

    lax.fori_loop(0, n_tot // GDN_UNROLL, prep, 0)

    s_s[...] = jnp.zeros_like(s_s)
    orders = [_chunk_order(d, n_ctx, n_tot) for d in (0, 1)]
    o_refs = (o_ref.at[0], ob_s)

    def step(j, carry):
        loaded = []
        for d, (_, _, last) in enumerate(dirs):
            n = orders[d](j)
            loaded.append((n, s_s[d], lhs_s[d, n], add_s[d, n], g_s[d, pl.ds(n, 1), last:last + 1]))
        results = []
        for n, s, lhs, add, g_last in loaded:
            z = jnp.dot(lhs, s.astype(BF16), preferred_element_type=F32) + add
            results.append((n, jnp.exp(g_last) * s + z[CHUNK:], z[:CHUNK]))
        for d, (n, s_new, o) in enumerate(results):
            s_s[d] = s_new
            o_refs[d][pl.ds(pl.multiple_of(n * CHUNK, CHUNK), CHUNK), :] = o
        return carry

    lax.fori_loop(0, n_tot, step, 0)
    o_ref[0] += ob_s[...]


def gdn_scan(qkv, log_alpha, beta):
    b, t, _ = qkv.shape
    n_tot = t // CHUNK
    ncp = log_alpha.shape[3]
    blk = pl.BlockSpec((1, t, GDN_DK), lambda bi, hi: (bi, 0, hi))
    k_blk = pl.BlockSpec((1, t, GDN_DK), lambda bi, hi: (bi, 0, GDN_HEADS + hi))
    v_blk = pl.BlockSpec((1, t, GDN_DV), lambda bi, hi: (bi, 0, 2 * GDN_HEADS + hi))
    gate_blk = pl.BlockSpec((2, 1, 1, ncp, CHUNK), lambda bi, hi: (0, bi, hi, 0, 0))
    return pl.pallas_call(
        functools.partial(_gdn_kernel, n_ctx=CTX_LEN // CHUNK, n_tot=n_tot),
        grid=(b, GDN_HEADS),
        in_specs=[blk, k_blk, v_blk, gate_blk, gate_blk],
        out_specs=blk,
        out_shape=jax.ShapeDtypeStruct((b, t, GDN_HEADS * GDN_DV), F32),
        scratch_shapes=[pltpu.VMEM((2, ncp, CHUNK), F32),
                        pltpu.VMEM((2, n_tot, CHUNK + GDN_DK, GDN_DV), BF16),
                        pltpu.VMEM((2, n_tot, CHUNK + GDN_DK, GDN_DV), F32),
                        pltpu.VMEM((2, GDN_DK, GDN_DV), F32),
                        pltpu.VMEM((t, GDN_DV), F32)],
        compiler_params=_cparams(("parallel", "parallel")),
        name="gdn_scan",
    )(qkv, qkv, qkv, log_alpha, beta)


def _mlstm_kernel(q_ref, k_ref, v_ref, ip_ref, lf_ref, o_ref, b_s, c_s, n_s, m_s, ob_s, *, n_ctx, n_tot):
    ii, jj, _, _ = _masks(0)
    eye = ii == jj
    hps = MLSTM_HEADS_PER_STEP
    dirs = []
    for d in (0, 1):
        _, _, incl, _ = _masks(d)
        cum = (ii <= jj).astype(F32) if d == 0 else (ii >= jj).astype(F32)
        for hh in range(hps):
            b_s[d, hh] = _cumsum_rows(lf_ref[d, 0, hh], cum)
        dirs.append((incl, CHUNK - 1 if d == 0 else 0))
    c_s[...] = jnp.zeros_like(c_s)
    n_s[...] = jnp.zeros_like(n_s)
    m_s[...] = jnp.zeros_like(m_s)
    orders = [_chunk_order(d, n_ctx, n_tot) for d in (0, 1)]
    o_refs = (o_ref.at[0], ob_s)

    def step(j, carry):
        loaded = {}
        for d in (0, 1):
            n = orders[d](j)
            rows = pl.ds(pl.multiple_of(n * CHUNK, CHUNK), CHUNK)
            for hh in range(hps):
                qsl = slice(hh * MLSTM_DQK, (hh + 1) * MLSTM_DQK)
                vsl = slice(hh * MLSTM_DV, (hh + 1) * MLSTM_DV)
                loaded[(d, hh)] = (q_ref[0, rows, qsl], k_ref[0, rows, qsl], v_ref[0, rows, vsl],
                                   b_s[d, hh, pl.ds(n, 1), :], ip_ref[d, 0, hh, pl.ds(n, 1), :],
                                   c_s[d, hh], n_s[d, hh], m_s[d, hh], rows, vsl)
        results = {}

        def chain(key, qc, kc, vc, b_row, ip_row, c_st, n_st, m_st, incl, last):
            b_col = _row_to_col(b_row, eye)
            b_last = b_row[:, last:last + 1]
            qk = _mm_nt(qc, kc)
            qc_st = _mm(qc, c_st)
            qn = jnp.sum(qc * n_st, axis=1, keepdims=True)
            log_end = b_last - b_row + ip_row
            m_end = jnp.max(log_end, axis=1, keepdims=True)
            yield
            log_d = jnp.where(incl, b_col - b_row + ip_row, NEG)
            m_intra = jnp.max(log_d, axis=1, keepdims=True)
            m_new = jnp.maximum(b_last + m_st, m_end)
            decay = jnp.exp(b_last + m_st - m_new)
            kw_col = _row_to_col(jnp.exp(log_end - m_new), eye)
            yield
            m_row = jnp.maximum(b_col + m_st, m_intra)
            w_state = jnp.exp(b_col + m_st - m_row)
            k_w = kc * kw_col
            c_new = decay * c_st + _mm_tn(k_w, vc)
            n_new = decay * n_st + jnp.sum(k_w, axis=0, keepdims=True)
            w_intra = jnp.exp(log_d - m_row) * qk
            num = w_state * qc_st + _mm(w_intra, vc)
            den = w_state * qn + jnp.sum(w_intra, axis=1, keepdims=True)
            yield
            results[key] = (num / jnp.maximum(jnp.abs(den), jnp.exp(-m_row)), c_new, n_new, m_new)

        _run_interleaved(chain(key, *vals[:8], *dirs[key[0]]) for key, vals in loaded.items())
        for (d, hh), (h, c_new, n_new, m_new) in results.items():
            rows, vsl = loaded[(d, hh)][8:]
            c_s[d, hh] = c_new
            n_s[d, hh] = n_new
            m_s[d, hh] = m_new
            o_refs[d][rows, vsl] = h
        return carry

    lax.fori_loop(0, n_tot, step, 0)
    o_ref[0] += ob_s[...]


def mlstm_scan(qk, z, i_pre, log_f):
    b, t, _ = qk.shape
    n_tot = t // CHUNK
    ncp = i_pre.shape[3]
    hps = MLSTM_HEADS_PER_STEP
    n_steps = MLSTM_HEADS // hps
    v_off = 2 * MLSTM_HEADS * MLSTM_DQK // (hps * MLSTM_DV)
    q_blk = pl.BlockSpec((1, t, hps * MLSTM_DQK), lambda bi, hi: (bi, 0, hi))
    k_blk = pl.BlockSpec((1, t, hps * MLSTM_DQK), lambda bi, hi: (bi, 0, n_steps + hi))
    v_blk = pl.BlockSpec((1, t, hps * MLSTM_DV), lambda bi, hi: (bi, 0, v_off + hi))
    o_blk = pl.BlockSpec((1, t, hps * MLSTM_DV), lambda bi, hi: (bi, 0, hi))
    gate_blk = pl.BlockSpec((2, 1, hps, ncp, CHUNK), lambda bi, hi: (0, bi, hi, 0, 0))
    return pl.pallas_call(
        functools.partial(_mlstm_kernel, n_ctx=CTX_LEN // CHUNK, n_tot=n_tot),
        grid=(b, n_steps),
        in_specs=[q_blk, k_blk, v_blk, gate_blk, gate_blk],
        out_specs=o_blk,
        out_shape=jax.ShapeDtypeStruct((b, t, MLSTM_HEADS * MLSTM_DV), F32),
        scratch_shapes=[pltpu.VMEM((2, hps, ncp, CHUNK), F32),
                        pltpu.VMEM((2, hps, MLSTM_DQK, MLSTM_DV), F32),
                        pltpu.VMEM((2, hps, 1, MLSTM_DQK), F32),
                        pltpu.VMEM((2, hps, 1, 1), F32),
                        pltpu.VMEM((t, hps * MLSTM_DV), F32)],
        compiler_params=_cparams(("parallel", "parallel")),
        name="mlstm_scan",
    )(qk, qk, z, i_pre, log_f)


def _rwkv_kernel(r_ref, v_ref, kk_ref, lw0_ref, lw1_ref, kd0_ref, kd1_ref, bv0_ref, bv1_ref, o_ref,
                 lhs_s, add_s, gl_s, s_s, ob_s, *, n_ctx, n_tot):
    dir_refs = ((lw0_ref, kd0_ref, bv0_ref), (lw1_ref, kd1_ref, bv1_ref))
    ii, jj, _, _ = _masks(0)
    eye = ii == jj
    eye_f = eye.astype(F32)
    blockdiag = (ii // 16) == (jj // 16)
    hd = RWKV_HEAD
    dirs = []
    for d in (0, 1):
        _, _, incl, strict = _masks(d)
        dirs.append((incl, strict, CHUNK - 1 if d == 0 else 0))
    zero = jnp.zeros((hd, hd), F32)

    def prep(j, carry):
        loaded = []
        for uu in range(RWKV_UNROLL):
            n = j * RWKV_UNROLL + uu
            rows = pl.ds(pl.multiple_of(n * CHUNK, CHUNK), CHUNK)
            loaded.append((n, r_ref[0, rows, :], v_ref[0, rows, :], kk_ref[0, rows, :],
                           [tuple(ref[0, rows, :] for ref in dir_refs[d]) for d in (0, 1)]))
        parts = {}

        def chain(key, r, v, kk, lw, kd, bv, gcs, e_end, incl, strict):
            e_neg = jnp.exp(-gcs)
            a_h = -kk * jnp.exp(gcs - lw)
            r_h = r * jnp.exp(gcs)
            b_h = bv * e_neg
            k_h = kd * e_neg
            a_ab = jnp.where(strict, _mm_nt(a_h, b_h), 0.0)
            a_ak = jnp.where(strict, _mm_nt(a_h, k_h), 0.0)
            a_rb = jnp.where(incl, _mm_nt(r_h, b_h), 0.0)
            a_rk = jnp.where(incl, _mm_nt(r_h, k_h), 0.0)
            yield
            av = _mm(a_ak, v)
            ta, tav = yield from _tri_solve_steps(a_ab, [a_h, av], eye_f, blockdiag)
            b_g = bv * e_end
            parts[key] = (r_h + _mm(a_rb, ta), _mm_tn(b_g, ta),
                          _mm(a_rb, tav) + _mm(a_rk, v), _mm_tn(b_g, tav) + _mm_tn(kd * e_end, v))

        gens, gls = [], {}
        for uu, (_, r2, v2, kk2, per_dir) in enumerate(loaded):
            for d, (incl, strict, last) in enumerate(dirs):
                lw2, kd2, bv2 = per_dir[d]
                gcs2 = _cumsum_cols(incl.astype(F32), lw2)
                gl_row2 = gcs2[last:last + 1, :]
                e_end2 = jnp.exp(gl_row2 - gcs2)
                gls[(uu, d)] = jnp.exp(gl_row2)
                for hh in range(2):
                    sl = slice(hh * hd, (hh + 1) * hd)
                    gens.append(chain((uu, d, hh), *(a[:, sl] for a in (r2, v2, kk2, lw2, kd2, bv2, gcs2, e_end2)),
                                      incl, strict))
        _run_interleaved(gens)
        for uu, (n, _, _, _, _) in enumerate(loaded):
            for d in (0, 1):
                (rt0, mx0, yc0, kv0), (rt1, mx1, yc1, kv1) = parts[(uu, d, 0)], parts[(uu, d, 1)]
                stack = lambda t0, t1, b0, b1: jnp.concatenate(
                    [jnp.concatenate([t0, t1], axis=1), jnp.concatenate([b0, zero], axis=1),
                     jnp.concatenate([zero, b1], axis=1)], axis=0)
                lhs_s[d, n] = stack(rt0, rt1, mx0, mx1).astype(BF16)
                add_s[d, n] = stack(yc0, yc1, kv0, kv1)
                gl_s[d, pl.ds(n, 1), :] = gls[(uu, d)]
        return carry

    lax.fori_loop(0, n_tot // RWKV_UNROLL, prep, 0)

    s_s[...] = jnp.zeros_like(s_s)
    orders = [_chunk_order(d, n_ctx, n_tot) for d in (0, 1)]
    eye2 = _iota2(2 * hd, 2 * hd, 0) == _iota2(2 * hd, 2 * hd, 1)
    o_refs = (o_ref.at[0], ob_s)

    def step(j, carry):
        loaded = []
        for d in (0, 1):
            n = orders[d](j)
            loaded.append((n, s_s[d], lhs_s[d, n], add_s[d, n], gl_s[d, pl.ds(n, 1), :]))
        results = []
        for n, s, lhs, add, gl in loaded:
            z = jnp.dot(lhs, s.astype(BF16), preferred_element_type=F32) + add
            results.append((n, _row_to_col(gl, eye2) * s + z[CHUNK:], z[:CHUNK]))
        for d, (n, s_new, y) in enumerate(results):
            s_s[d] = s_new
            o_refs[d][pl.ds(pl.multiple_of(n * CHUNK, CHUNK), CHUNK), :] = y
        return carry

    lax.fori_loop(0, n_tot, step, 0)
    o_ref[0] += ob_s[...]


def rwkv_scan(r, v, kk, log_w, k_dir, b_dir):
    b, t, _ = r.shape
    n_tot = t // CHUNK
    blk = pl.BlockSpec((1, t, 128), lambda bi, hi: (bi, 0, hi))
    return pl.pallas_call(
        functools.partial(_rwkv_kernel, n_ctx=CTX_LEN // CHUNK, n_tot=n_tot),
        grid=(b, RWKV_HEADS // 2),
        in_specs=[blk] * 9,
        out_specs=blk,
        out_shape=jax.ShapeDtypeStruct((b, t, D_MODEL), F32),
        scratch_shapes=[pltpu.VMEM((2, n_tot, CHUNK + 2 * RWKV_HEAD, 2 * RWKV_HEAD), BF16),
                        pltpu.VMEM((2, n_tot, CHUNK + 2 * RWKV_HEAD, 2 * RWKV_HEAD), F32),
                        pltpu.VMEM((2, 8 * ((n_tot + 7) // 8), 2 * RWKV_HEAD), F32),
                        pltpu.VMEM((2, 2 * RWKV_HEAD, 2 * RWKV_HEAD), F32),
                        pltpu.VMEM((t, 2 * RWKV_HEAD), F32)],
        compiler_params=_cparams(("parallel", "parallel")),
        name="rwkv_scan",
    )(r, v, kk, *log_w, *k_dir, *b_dir)


def _na_kernel(q_ref, k_ref, v_ref, bias_ref, o_ref, *, rows):
    scale = NA_DH ** -0.5
    slab = NA_WIN_ROWS * GRID_W
    lane = _iota2(1, 2 * NA_DH, 1)
    head_masks = (lane < NA_DH, lane >= NA_DH)
    kc2 = k_ref[0, 0:CTX_LEN, :].astype(BF16)
    vc2 = v_ref[0, 0:CTX_LEN, :].astype(BF16)

    def body(j, carry):
        loaded = []
        for uu in range(NA_ROWS_PER_STEP):
            r = j * NA_ROWS_PER_STEP + uu
            r0 = jnp.clip(r - NA_WIN_ROWS // 2, 0, rows - NA_WIN_ROWS)
            dr0 = r0 - r + NA_WIN_ROWS - 1
            krows = pl.ds(pl.multiple_of(CTX_LEN + r0 * GRID_W, GRID_W), slab)
            loaded.append((r, q_ref[0, pl.ds(pl.multiple_of(CTX_LEN + r * GRID_W, GRID_W), GRID_W), :],
                           k_ref[0, krows, :].astype(BF16), v_ref[0, krows, :].astype(BF16),
                           [bias_ref[hh, pl.ds(dr0, 1)][0] for hh in range(2)]))
        outs = {}

        def chain(key, q2, ks2, vs2, bias, mask):
            qh = jnp.where(mask, q2, 0.0)
            s_lat = _mm_nt(qh, ks2) * scale + bias
            s_ctx = _mm_nt(qh, kc2) * scale
            yield
            m = jnp.maximum(jnp.max(s_lat, axis=1, keepdims=True), jnp.max(s_ctx, axis=1, keepdims=True))
            yield
            p_lat = jnp.exp(s_lat - m)
            p_ctx = jnp.exp(s_ctx - m)
            l = jnp.sum(p_lat, axis=1, keepdims=True) + jnp.sum(p_ctx, axis=1, keepdims=True)
            outs[key] = (_mm(p_lat, vs2) + _mm(p_ctx, vc2)) / l
            yield

        _run_interleaved(chain((uu, hh), q2, ks2, vs2, biases[hh], head_masks[hh])
                         for uu, (_, q2, ks2, vs2, biases) in enumerate(loaded) for hh in range(2))
        for uu, (r, _, _, _, _) in enumerate(loaded):
            o_ref[0, pl.ds(pl.multiple_of(r * GRID_W, GRID_W), GRID_W), :] = jnp.where(
                head_masks[0], outs[(uu, 0)], outs[(uu, 1)])
        return carry

    lax.fori_loop(0, rows // NA_ROWS_PER_STEP, body, 0)


def _na_bias_table(rpb):
    cols = np.arange(GRID_W)
    win_c0 = np.clip(cols - NA_WIN_COLS // 2, 0, GRID_W - NA_WIN_COLS)
    kc = np.arange(GRID_W)
    in_win = (kc[None, :] >= win_c0[:, None]) & (kc[None, :] < win_c0[:, None] + NA_WIN_COLS)
    dc = np.clip(kc[None, :] - cols[:, None] + NA_WIN_COLS - 1, 0, 2 * NA_WIN_COLS - 2)
    dr = np.arange(NA_WIN_ROWS)[:, None] + np.arange(NA_WIN_ROWS)[None, :]
    tab = rpb.astype(F32)[:, dr][:, :, :, dc]
    tab = jnp.where(in_win[None, None, None], tab, NEG)
    tab = tab.transpose(0, 1, 3, 2, 4)
    return tab.reshape(NA_HEADS, NA_WIN_ROWS, GRID_W, NA_WIN_ROWS * GRID_W)


def na_attention(z, bias_tab):
    b, t, _ = z.shape
    t_lat = t - CTX_LEN
    n_pairs = NA_HEADS // 2
    return pl.pallas_call(
        functools.partial(_na_kernel, rows=t_lat // GRID_W),
        grid=(n_pairs, b),
        in_specs=[pl.BlockSpec((1, t, 128), lambda hi, bi: (bi, 0, hi)),
                  pl.BlockSpec((1, t, 128), lambda hi, bi: (bi, 0, n_pairs + hi)),
                  pl.BlockSpec((1, t, 128), lambda hi, bi: (bi, 0, 2 * n_pairs + hi)),
                  pl.BlockSpec((2, NA_WIN_ROWS, GRID_W, NA_WIN_ROWS * GRID_W), lambda hi, bi: (hi, 0, 0, 0))],
        out_specs=pl.BlockSpec((1, t_lat, 128), lambda hi, bi: (bi, 0, hi)),
        out_shape=jax.ShapeDtypeStruct((b, t_lat, D_MODEL), F32),
        compiler_params=_cparams(("parallel", "parallel")),
        name="na_attention",
    )(z, z, z, bias_tab)


def _router_kernel(x_ref, modl_ref, modc_ref, rw_ref, rb_ref, hb_ref, idx_ref, wt_ref, *, tile_off):
    is_ctx = (pl.program_id(1) + tile_off) * TOKEN_TILE < CTX_LEN
    sh = _mod_rows(modl_ref, modc_ref, is_ctx, 3)
    sc = _mod_rows(modl_ref, modc_ref, is_ctx, 4)
    h = x_ref[0] * (1.0 + sc) + sh
    hb_ref[0] = h.astype(BF16)
    logits = lax.dot_general(rw_ref[...], h, (((1,), (1,)), ((), ())), preferred_element_type=F32,
                             precision=_HI)
    mx = jnp.max(logits, axis=0, keepdims=True)
    ex = jnp.exp(logits - mx)
    probs = ex / jnp.sum(ex, axis=0, keepdims=True)
    sel = probs + rb_ref[...]
    tm = sel.shape[1]
    e_id = _iota2(N_EXPERTS, tm, 0)
    per_group = N_EXPERTS // N_GROUPS
    g_id = e_id // per_group

    def top1(vals):
        m1 = jnp.max(vals, axis=0, keepdims=True)
        i1 = jnp.min(jnp.where(vals == m1, e_id, N_EXPERTS), axis=0, keepdims=True)
        return m1, i1

    best_score, best = None, None
    for g in range(N_GROUPS):
        vals = jnp.where(g_id == g, sel, NEG)
        m1, i1 = top1(vals)
        m2, _ = top1(jnp.where(e_id == i1, NEG, vals))
        score = m1 + m2
        if g == 0:
            best_score, best = score, jnp.zeros_like(i1)
        else:
            better = score > best_score
            best = jnp.where(better, g, best)
            best_score = jnp.where(better, score, best_score)
    vals = jnp.where(g_id == best, sel, NEG)
    _, i1 = top1(vals)
    _, i2 = top1(jnp.where(e_id == i1, NEG, vals))
    p1 = jnp.sum(jnp.where(e_id == i1, probs, 0.0), axis=0, keepdims=True)
    p2 = jnp.sum(jnp.where(e_id == i2, probs, 0.0), axis=0, keepdims=True)
    tot = p1 + p2
    idx_ref[0] = jnp.where(_iota2(2, tm, 0) == 0, i1, i2)
    eye = _iota2(tm, tm, 0) == _iota2(tm, tm, 1)
    wt_ref[0] = jnp.where(_iota2(tm, 2, 1) == 0, _row_to_col(p1 / tot, eye), _row_to_col(p2 / tot, eye))


def moe_route(x, mods, router_w, router_b, *, tile_off):
    b, t_out, _ = x.shape
    modl, modc = mods
    return pl.pallas_call(
        functools.partial(_router_kernel, tile_off=tile_off),
        grid=(b, t_out // TOKEN_TILE),
        in_specs=[pl.BlockSpec((1, TOKEN_TILE, D_MODEL), lambda bi, ti: (bi, ti, 0)),
                  pl.BlockSpec((1, 1, 6 * D_MODEL), lambda bi, ti: (bi, 0, 0)),
                  pl.BlockSpec((1, 1, 6 * D_MODEL), lambda bi, ti: (0, 0, 0)),
                  pl.BlockSpec((N_EXPERTS, D_MODEL), lambda bi, ti: (0, 0)),
                  pl.BlockSpec((N_EXPERTS, 1), lambda bi, ti: (0, 0))],
        out_specs=[pl.BlockSpec((1, TOKEN_TILE, D_MODEL), lambda bi, ti: (bi, ti, 0)),
                   pl.BlockSpec((1, 2, TOKEN_TILE), lambda bi, ti: (bi, 0, ti)),
                   pl.BlockSpec((1, TOKEN_TILE, 2), lambda bi, ti: (bi, ti, 0))],
        out_shape=[jax.ShapeDtypeStruct((b, t_out, D_MODEL), BF16),
                   jax.ShapeDtypeStruct((b, 2, t_out), jnp.int32),
                   jax.ShapeDtypeStruct((b, t_out, 2), F32)],
        compiler_params=_cparams(("parallel", "parallel")),
        name="moe_route",
    )(x, modl, modc, router_w.T, router_b.reshape(-1, 1))


def _ffn_kernel(te_ref, nt_ref, x_ref, w1_ref, w3_ref, w2_ref, o_ref):
    @pl.when(pl.program_id(0) < nt_ref[0])
    def _():
        xb = x_ref[...]
        h1 = jnp.dot(xb, w1_ref[0, 0].astype(BF16), preferred_element_type=F32)
        h3 = jnp.dot(xb, w3_ref[0, 0].astype(BF16), preferred_element_type=F32)
        hid = (_silu(h1) * h3).astype(BF16)
        o_ref[...] = jnp.dot(hid, w2_ref[0, 0].astype(BF16), preferred_element_type=F32).astype(o_ref.dtype)

    @pl.when(pl.program_id(0) >= nt_ref[0])
    def _():
        o_ref[...] = jnp.zeros_like(o_ref)


def expert_ffn(xs, tile_expert, n_tiles_used, w1, w3, w2, layer):
    p = xs.shape[0]
    grid_spec = pltpu.PrefetchScalarGridSpec(
        num_scalar_prefetch=2,
        grid=(p // FFN_TILE,),
        in_specs=[pl.BlockSpec((FFN_TILE, D_MODEL), lambda i, te, nt: (i, 0)),
                  pl.BlockSpec((1, 1, D_MODEL, D_EXPERT), lambda i, te, nt: (layer, te[i], 0, 0)),
                  pl.BlockSpec((1, 1, D_MODEL, D_EXPERT), lambda i, te, nt: (layer, te[i], 0, 0)),
                  pl.BlockSpec((1, 1, D_EXPERT, D_MODEL), lambda i, te, nt: (layer, te[i], 0, 0))],
        out_specs=pl.BlockSpec((FFN_TILE, D_MODEL), lambda i, te, nt: (i, 0)),
    )
    return pl.pallas_call(
        _ffn_kernel,
        grid_spec=grid_spec,
        out_shape=jax.ShapeDtypeStruct((p, D_MODEL), BF16),
        compiler_params=_cparams(("arbitrary",)),
        name="expert_ffn",
    )(tile_expert, n_tiles_used, xs, w1, w3, w2)


def _combine_ln_kernel(x_ref, y0_ref, y1_ref, wt_ref, modl_ref, modc_ref, g_ref, b_ref, o_ref, *, tile_off):
    is_ctx = (pl.program_id(1) + tile_off) * TOKEN_TILE < CTX_LEN
    gate = _mod_rows(modl_ref, modc_ref, is_ctx, 5)
    wt = wt_ref[0]
    f = wt[:, 0:1] * y0_ref[0, 0].astype(F32) + wt[:, 1:2] * y1_ref[0, 0].astype(F32)
    r = ALPHA * x_ref[0] + gate * f
    o_ref[0] = _layer_norm_rows(r, g_ref[...], b_ref[...])


def combine_ln(x, y01, wt, mods, ln_g, ln_b, *, tile_off):
    _, b, t_out, _ = y01.shape
    modl, modc = mods
    tok = lambda bi, ti: (bi, ti, 0)
    return pl.pallas_call(
        functools.partial(_combine_ln_kernel, tile_off=tile_off),
        grid=(b, t_out // TOKEN_TILE),
        in_specs=[pl.BlockSpec((1, TOKEN_TILE, D_MODEL), tok),
                  pl.BlockSpec((1, 1, TOKEN_TILE, D_MODEL), lambda bi, ti: (0, bi, ti, 0)),
                  pl.BlockSpec((1, 1, TOKEN_TILE, D_MODEL), lambda bi, ti: (1, bi, ti, 0)),
                  pl.BlockSpec((1, TOKEN_TILE, 2), tok),
                  pl.BlockSpec((1, 1, 6 * D_MODEL), lambda bi, ti: (bi, 0, 0)),
                  pl.BlockSpec((1, 1, 6 * D_MODEL), lambda bi, ti: (0, 0, 0)),
                  pl.BlockSpec((1, D_MODEL), lambda bi, ti: (0, 0)),
                  pl.BlockSpec((1, D_MODEL), lambda bi, ti: (0, 0))],
        out_specs=pl.BlockSpec((1, TOKEN_TILE, D_MODEL), tok),
        out_shape=jax.ShapeDtypeStruct((b, t_out, D_MODEL), F32),
        compiler_params=_cparams(("parallel", "parallel")),
        name="moe_combine_ln",
    )(x, y01, y01, wt, modl, modc, ln_g.reshape(1, -1), ln_b.reshape(1, -1))


def moe_layer(x, mods, router_w, router_b, w1, w3, w2, layer, ln_g, ln_b, *, tile_off):
    hb, idx, wt = moe_route(x, mods, router_w, router_b, tile_off=tile_off)
    b, t, _ = hb.shape
    n_tok = b * t
    n_pair = 2 * n_tok
    e_flat = idx.transpose(1, 0, 2).reshape(n_pair)
    onehot = (e_flat[:, None] == jnp.arange(N_EXPERTS)[None, :]).astype(jnp.int32)
    csum = jnp.cumsum(onehot, axis=0)
    counts = csum[-1]
    rank = jnp.sum((csum - onehot) * onehot, axis=1)
    padded = ((counts + FFN_TILE - 1) // FFN_TILE) * FFN_TILE
    ends = jnp.cumsum(padded)
    offs = ends - padded
    pos = offs[e_flat] + rank
    n_rows = n_pair + N_EXPERTS * FFN_TILE
    n_tiles = n_rows // FFN_TILE
    src = jnp.zeros((n_rows,), jnp.int32).at[pos].set(
        jnp.arange(n_pair, dtype=jnp.int32) % n_tok, mode="promise_in_bounds", unique_indices=True)
    tile_start = jnp.arange(n_tiles, dtype=jnp.int32) * FFN_TILE
    tile_expert = jnp.minimum(jnp.searchsorted(ends, tile_start, side="right"), N_EXPERTS - 1).astype(jnp.int32)
    n_used = (ends[-1] // FFN_TILE).astype(jnp.int32).reshape(1)
    xs = hb.reshape(n_tok, D_MODEL).at[src].get(mode="promise_in_bounds")
    ys = expert_ffn(xs, tile_expert, n_used, w1, w3, w2, layer)
    y01 = ys.at[pos].get(mode="promise_in_bounds", unique_indices=True).reshape(2, b, t, D_MODEL)
    return combine_ln(x, y01, wt, mods, ln_g, ln_b, tile_off=tile_off)


def _gate_rows(g):
    b, t = g.shape[:2]
    n = t // CHUNK
    ncp = 8 * ((n + 7) // 8)
    g = g.transpose(2, 3, 0, 4, 1).reshape(2, 2, b, g.shape[-1], n, CHUNK)
    return jnp.pad(g, ((0, 0),) * 4 + ((0, ncp - n), (0, 0)))


def gdn_layer(x, mods, w_in, conv_w, a_log, dt_bias, norm_g, w_out):
    wq = GDN_HEADS * GDN_DK
    n_main = 4 * wq
    z, ab = linear(x, w_in[:, :n_main].astype(BF16), mods=mods, modulate=(0, 1), w_small=w_in[:, n_main:])
    qkv = seq_prep(z, 3 * GDN_HEADS, conv_w=conv_w, norm_blocks=2 * GDN_HEADS, rope_blocks=2 * GDN_HEADS,
                   scale_blocks=GDN_HEADS, scale=GDN_DK ** -0.5)
    b, t, _ = x.shape
    ab = ab.reshape(b, t, 2, 2, GDN_HEADS)
    ab = ab.at[:, :, :, 0].add(dt_bias[None, None])
    rows = _gate_rows(ab)
    log_alpha = -jnp.exp(a_log)[:, None, :, None, None] * jax.nn.softplus(rows[:, 0])
    beta = jax.nn.sigmoid(rows[:, 1])
    valid = (jnp.arange(rows.shape[4]) < t // CHUNK)[:, None]
    log_alpha = jnp.where(valid, log_alpha, 0.0)
    o = gdn_scan(qkv, log_alpha, beta)
    return o, w_out, (GDN_DV, False, _silu), (z, 3, jnp.tile(norm_g, GDN_HEADS))


def mlstm_layer(x, mods, w_in, gate_b, norm_g, w_out):
    wq = MLSTM_HEADS * MLSTM_DQK
    wv = MLSTM_HEADS * MLSTM_DV
    n_main = 2 * wq + 2 * wv
    z, gt = linear(x, w_in[:, :n_main].astype(BF16), mods=mods, modulate=(0, 1), w_small=w_in[:, n_main:])
    qk = seq_prep(z, 2 * MLSTM_HEADS, norm_blocks=0, rope_blocks=2 * MLSTM_HEADS, scale_blocks=MLSTM_HEADS,
                  scale=MLSTM_DQK ** -0.5)
    b, t, _ = x.shape
    gt = gt.reshape(b, t, 2, 2, MLSTM_HEADS) + gate_b[None, None]
    rows = _gate_rows(gt)
    valid = (jnp.arange(rows.shape[4]) < t // CHUNK)[:, None]
    i_pre = rows[:, 0]
    log_f = jnp.where(valid, jax.nn.log_sigmoid(rows[:, 1]), 0.0)
    h = mlstm_scan(qk, z, i_pre, log_f)
    return h, w_out, (MLSTM_DV, True, _sigmoid), (z, 2, norm_g)


RWKV_IN_COLS = 3584


def _rwkv_in_kernel(x_ref, modl_ref, modc_ref, w_ref, o_ref, hd_s):
    t = x_ref.shape[1]

    @pl.when(pl.program_id(1) == 0)
    def _():
        row = _iota2(t, 1, 0)
        is_ctx = row < CTX_LEN
        sel = lambda i: jnp.where(is_ctx, modc_ref[0, :, i * D_MODEL:(i + 1) * D_MODEL],
                                  modl_ref[0, :, i * D_MODEL:(i + 1) * D_MODEL])
        h = x_ref[0] * (1.0 + sel(1)) + sel(0)
        seg_first = (row == 0) | (row == CTX_LEN)
        seg_last = (row == CTX_LEN - 1) | (row == t - 1)
        h_prev = jnp.where(seg_first, 0.0, pltpu.roll(h, 1, axis=0))
        h_next = jnp.where(seg_last, 0.0, pltpu.roll(h, t - 1, axis=0))
        hd_s[:, :D_MODEL] = h.astype(BF16)
        hd_s[:, D_MODEL:] = (0.5 * (h_prev + h_next) - h).astype(BF16)

    for i in range(t // TOKEN_TILE):
        rows = slice(i * TOKEN_TILE, (i + 1) * TOKEN_TILE)
        o_ref[0, rows, :] = jnp.dot(hd_s[rows, :], w_ref[...], preferred_element_type=F32)


def rwkv_in_proj(x, mods, w_big):
    b, t, _ = x.shape
    modl, modc = mods
    n = w_big.shape[1]
    return pl.pallas_call(
        _rwkv_in_kernel,
        grid=(b, n // N_CHUNK_COLS),
        in_specs=[pl.BlockSpec((1, t, D_MODEL), lambda bi, ji: (bi, 0, 0)),
                  pl.BlockSpec((1, 1, 6 * D_MODEL), lambda bi, ji: (bi, 0, 0)),
                  pl.BlockSpec((1, 1, 6 * D_MODEL), lambda bi, ji: (0, 0, 0)),
                  pl.BlockSpec((2 * D_MODEL, N_CHUNK_COLS), lambda bi, ji: (0, ji))],
        out_specs=pl.BlockSpec((1, t, N_CHUNK_COLS), lambda bi, ji: (bi, 0, ji)),
        out_shape=jax.ShapeDtypeStruct((b, t, n), F32),
        scratch_shapes=[pltpu.VMEM((t, 2 * D_MODEL), BF16)],
        compiler_params=_cparams(("parallel", "arbitrary")),
        name="rwkv_in_proj",
    )(x, modl, modc, w_big)


def _seg64_sums(x):
    left = _iota2(1, 128, 1) < RWKV_HEAD
    parts = []
    for blk_i in range(x.shape[1] // 128):
        blk = x[:, blk_i * 128:(blk_i + 1) * 128]
        s_l = jnp.sum(jnp.where(left, blk, 0.0), axis=1, keepdims=True)
        s_r = jnp.sum(jnp.where(left, 0.0, blk), axis=1, keepdims=True)
        parts.append(jnp.where(left, s_l, s_r))
    return jnp.concatenate(parts, axis=1)


def _softplus(x):
    return jnp.maximum(x, 0.0) + jnp.log(1.0 + jnp.exp(-jnp.abs(x)))


def _rwkv_mid_kernel(z_ref, w2_ref, a2_ref, g2_ref, vec_ref, r_ref, v_ref, kk_ref, lw0_ref, lw1_ref,
                     kd0_ref, kd1_ref, bv0_ref, bv1_ref, g_ref):
    d = D_MODEL
    z = z_ref[0]
    r, k, v = z[:, :d], z[:, d:2 * d], z[:, 2 * d:3 * d]
    zg, zw, za = z[:, 3 * d:3 * d + 128], z[:, 3 * d + 128:3 * d + 256], z[:, 3 * d + 256:3 * d + 384]
    vec = vec_ref[...]
    k_k, k_a = vec[4:5], vec[5:6]
    r_ref[0] = r
    v_ref[0] = v
    g_ref[0] = _mm(_sigmoid(zg), g2_ref[...])
    kx = k * k_k
    kk = kx * lax.rsqrt(_seg64_sums(kx * kx) + 1e-6)
    kk_ref[0] = kk
    tw = jnp.tanh(zw)
    for dr, (lw_ref, kd_ref, bv_ref) in enumerate(((lw0_ref, kd0_ref, bv0_ref), (lw1_ref, kd1_ref, bv1_ref))):
        w_raw = -_softplus(-(vec[dr:dr + 1] + _mm(tw, w2_ref[dr]))) - 0.5
        a = _sigmoid(vec[2 + dr:3 + dr] + _mm(za, a2_ref[dr]))
        lw_ref[0] = -jnp.exp(w_raw)
        kd_ref[0] = k * (1.0 + (a - 1.0) * k_a)
        bv_ref[0] = kk * a


def rwkv_mid(z1, w2p, a2p, g2, vec):
    b, t, n = z1.shape
    tok = pl.BlockSpec((1, TOKEN_TILE, D_MODEL), lambda bi, ti: (bi, ti, 0))
    full = lambda a: pl.BlockSpec(a.shape, lambda bi, ti: (0,) * a.ndim)
    return pl.pallas_call(
        _rwkv_mid_kernel,
        grid=(b, t // TOKEN_TILE),
        in_specs=[pl.BlockSpec((1, TOKEN_TILE, n), lambda bi, ti: (bi, ti, 0)),
                  full(w2p), full(a2p), full(g2), full(vec)],
        out_specs=[tok] * 10,
        out_shape=[jax.ShapeDtypeStruct((b, t, D_MODEL), F32)] * 10,
        compiler_params=_cparams(("parallel", "parallel")),
        name="rwkv_mid",
    )(z1, w2p, a2p, g2, vec)


def _rwkv_post(y, r, v, kd0, kd1, g, params):
    inv = 1.0 / RWKV_HEAD
    yc = y - _seg64_sums(y) * inv
    yn = yc * lax.rsqrt(_seg64_sums(yc * yc) * inv + RWKV_GN_EPS)
    bonus = _seg64_sums(r * (kd0 + kd1) * params[2:3]) * v
    return (yn * params[0:1] + params[1:2] + bonus) * g


def rwkv_layer(x, mods, mu, w_rkv, w0, w1, w2, a0, a1, a2, g1, g2, k_k, k_a, r_k, lnx_g, lnx_b, w_out):
    d = D_MODEL
    cols = [(w_rkv[0], 0), (w_rkv[1], 2), (w_rkv[2], 3), (g1, 5), (w1[0], 1), (w1[1], 1), (a1[0], 4), (a1[1], 4)]
    top = jnp.concatenate([w for w, _ in cols], axis=1)
    bot = jnp.concatenate([mu[j][:, None] * w for w, j in cols], axis=1)
    w_big = jnp.pad(jnp.concatenate([top, bot], axis=0), ((0, 0), (0, RWKV_IN_COLS - top.shape[1]))).astype(BF16)
    z1 = rwkv_in_proj(x, mods, w_big)
    pad_dir = lambda w: jnp.stack([jnp.pad(w[0], ((0, w.shape[1]), (0, 0))), jnp.pad(w[1], ((w.shape[1], 0), (0, 0)))])
    vec = jnp.concatenate([w0, a0, k_k[None], k_a[None], jnp.zeros((2, d), F32)], axis=0)
    r, v, kk, lw0, lw1, kd0, kd1, bv0, bv1, g = rwkv_mid(z1, pad_dir(w2), pad_dir(a2), g2, vec)
    y = rwkv_scan(r, v, kk, (lw0, lw1), (kd0, kd1), (bv0, bv1))
    params = jnp.concatenate([lnx_g[None], lnx_b[None], r_k.reshape(1, d), jnp.zeros((5, d), F32)], axis=0)
    return y, w_out, "rwkv", (r, v, kd0, kd1, g, params)


def kernel(x, c, ctx, c_ctx, ada_w, ada_b, ln_g, ln_b, router_w, router_b, moe_w1, moe_w3, moe_w2, gdn_w_in, gdn_conv, gdn_a_log, gdn_dt_bias, gdn_norm_g, gdn_w_out, mlstm_w_in, mlstm_gate_b, mlstm_norm_g, mlstm_w_out, rwkv_mu, rwkv_w_rkv, rwkv_w0, rwkv_w1, rwkv_w2, rwkv_a0, rwkv_a1, rwkv_a2, rwkv_g1, rwkv_g2, rwkv_k_k, rwkv_k_a, rwkv_r_k, rwkv_lnx_g, rwkv_lnx_b, rwkv_w_out, na_w_in, na_rpb, na_w_out):
    b = x.shape[0]
    ctx_tiles = CTX_LEN // TOKEN_TILE
    mod_all = modulation_all(c, c_ctx, ada_w, ada_b)
    xs = jnp.concatenate([ctx, x], axis=1)
    assert DEPTH == 4
    for i in range(DEPTH):
        mods = (mod_all[i, :b, None, :], mod_all[i, b:b + 1, None, :])
        if i % 4 == 0:
            y, w_out, post, post_args = gdn_layer(xs, mods, gdn_w_in, gdn_conv, gdn_a_log, gdn_dt_bias,
                                                  gdn_norm_g, gdn_w_out)
        elif i % 4 == 1:
            y, w_out, post, post_args = mlstm_layer(xs, mods, mlstm_w_in, mlstm_gate_b, mlstm_norm_g, mlstm_w_out)
        elif i % 4 == 2:
            y, w_out, post, post_args = rwkv_layer(xs, mods, rwkv_mu, rwkv_w_rkv, rwkv_w0, rwkv_w1, rwkv_w2, rwkv_a0, rwkv_a1,
                                  rwkv_a2, rwkv_g1, rwkv_g2, rwkv_k_k, rwkv_k_a, rwkv_r_k, rwkv_lnx_g,
                                  rwkv_lnx_b, rwkv_w_out)
        else:
            z = linear(xs, na_w_in.astype(BF16), mods=mods, modulate=(0, 1))
            y, w_out, post, post_args = na_attention(z, _na_bias_table(na_rpb)), na_w_out, None, None
        off = ctx_tiles if (i % 4 == 3) else 0
        xs1 = out_proj_ln(y, w_out.astype(BF16), xs, mods, ln_g[i, 0], ln_b[i, 0], gate_idx=2, tile_off=off,
                          post=post, post_args=post_args)
        xs = moe_layer(xs1, mods, router_w, router_b, moe_w1, moe_w3, moe_w2, i, ln_g[i, 1], ln_b[i, 1],
                       tile_off=off)
    return xs
```

```python
import functools

import numpy as np
import jax
import jax.numpy as jnp
from jax import lax
from jax.experimental import pallas as pl
from jax.experimental.pallas import tpu as pltpu

F32 = jnp.float32
BF16 = jnp.bfloat16

D_MODEL = 1024
DEPTH = 4
GRID_W = 64
CTX_LEN = 256
ALPHA = (2 * DEPTH) ** 0.25
LN_EPS = 1e-5
ROPE_BASE = 10000.0

GDN_HEADS = 8
GDN_DK = 128
GDN_DV = 128
MLSTM_HEADS = 4
MLSTM_DQK = 128
MLSTM_DV = 256
RWKV_HEAD = 64
RWKV_HEADS = 16
RWKV_GN_EPS = 64e-5
NA_HEADS = 16
NA_DH = 64
NA_WIN_ROWS = 8
NA_WIN_COLS = 16
N_EXPERTS = 16
N_GROUPS = 4
D_EXPERT = 512

CHUNK = 64
GDN_UNROLL = 6
RWKV_UNROLL = 3
NA_ROWS_PER_STEP = 4
MLSTM_HEADS_PER_STEP = 2
TOKEN_TILE = 256
N_CHUNK_COLS = 512
FFN_TILE = 512
NEG = -1e30
VMEM_LIMIT = 56 * 1024 * 1024

_HI = lax.Precision.HIGHEST


def _cparams(sem):
    return pltpu.CompilerParams(dimension_semantics=sem, vmem_limit_bytes=VMEM_LIMIT)


def _mm(a, b):
    return jnp.dot(a.astype(BF16), b.astype(BF16), preferred_element_type=F32)


def _mm_nt(a, b):
    return lax.dot_general(a.astype(BF16), b.astype(BF16), (((1,), (1,)), ((), ())),
                           preferred_element_type=F32)


def _mm_tn(a, b):
    return lax.dot_general(a.astype(BF16), b.astype(BF16), (((0,), (0,)), ((), ())),
                           preferred_element_type=F32)


def _mmf(a, b):
    return jnp.dot(a, b, preferred_element_type=F32, precision=_HI)


def _silu(x):
    return x * (1.0 / (1.0 + jnp.exp(-x)))


def _sigmoid(x):
    return 1.0 / (1.0 + jnp.exp(-x))


def _mod_kernel(s_ref, w_ref, b_ref, o_ref):
    o_ref[0] = _mmf(_silu(s_ref[...]), w_ref[0]) + b_ref[0]


def modulation_all(c, c_ctx, ada_w, ada_b):
    b = c.shape[0]
    rows = 8 * ((b + 1 + 7) // 8)
    s = jnp.zeros((rows, D_MODEL), F32).at[:b].set(c).at[b].set(c_ctx)
    tn = 1536
    n = ada_w.shape[-1]
    return pl.pallas_call(
        _mod_kernel,
        grid=(DEPTH, n // tn),
        in_specs=[pl.BlockSpec((rows, D_MODEL), lambda i, j: (0, 0)),
                  pl.BlockSpec((1, D_MODEL, tn), lambda i, j: (i, 0, j)),
                  pl.BlockSpec((1, 1, tn), lambda i, j: (i, 0, j))],
        out_specs=pl.BlockSpec((1, rows, tn), lambda i, j: (i, 0, j)),
        out_shape=jax.ShapeDtypeStruct((DEPTH, rows, n), F32),
        compiler_params=_cparams(("arbitrary", "arbitrary")),
        name="adaln_modulation",
    )(s, ada_w, ada_b.reshape(DEPTH, 1, n))


def _mod_rows(modl_ref, modc_ref, is_ctx, idx):
    sl = slice(idx * D_MODEL, (idx + 1) * D_MODEL)
    return jnp.where(is_ctx, modc_ref[0, :, sl], modl_ref[0, :, sl])


def _linear_kernel(*refs, glob_off, modulate, act, n_main, has_small):
    it = iter(refs)
    x_ref = next(it)
    if modulate is not None:
        modl_ref, modc_ref = next(it), next(it)
    w_ref = next(it)
    ws_ref = next(it) if has_small else None
    o_ref = next(it)
    os_ref = next(it) if has_small else None

    h = x_ref[0]
    if modulate is not None:
        is_ctx = (pl.program_id(1) + glob_off) * TOKEN_TILE < CTX_LEN
        sh = _mod_rows(modl_ref, modc_ref, is_ctx, modulate[0])
        sc = _mod_rows(modl_ref, modc_ref, is_ctx, modulate[1])
        h = h * (1.0 + sc) + sh
    if act == "tanh":
        h = jnp.tanh(h)
    elif act == "sigmoid":
        h = _sigmoid(h)
    hb = h.astype(BF16)
    step = min(N_CHUNK_COLS, n_main)
    for j in range(n_main // step):
        o_ref[0, :, j * step:(j + 1) * step] = jnp.dot(
            hb, w_ref[:, j * step:(j + 1) * step], preferred_element_type=F32)
    if has_small:
        os_ref[0] = _mmf(h, ws_ref[...])


def linear(x, w_bf16, *, mods=None, modulate=None, act=None, w_small=None):
    b, t_out, k = x.shape
    n_main = w_bf16.shape[1]
    has_small = w_small is not None
    in_specs = [pl.BlockSpec((1, TOKEN_TILE, k), lambda bi, ti: (bi, ti, 0))]
    args = [x]
    if modulate is not None:
        modl, modc = mods
        in_specs += [pl.BlockSpec((1, 1, 6 * D_MODEL), lambda bi, ti: (bi, 0, 0)),
                     pl.BlockSpec((1, 1, 6 * D_MODEL), lambda bi, ti: (0, 0, 0))]
        args += [modl, modc]
    in_specs.append(pl.BlockSpec((k, n_main), lambda bi, ti: (0, 0)))
    args.append(w_bf16)
    out_specs = [pl.BlockSpec((1, TOKEN_TILE, n_main), lambda bi, ti: (bi, ti, 0))]
    out_shape = [jax.ShapeDtypeStruct((b, t_out, n_main), F32)]
    if has_small:
        ns = w_small.shape[1]
        in_specs.append(pl.BlockSpec((k, ns), lambda bi, ti: (0, 0)))
        args.append(w_small)
        out_specs.append(pl.BlockSpec((1, TOKEN_TILE, ns), lambda bi, ti: (bi, ti, 0)))
        out_shape.append(jax.ShapeDtypeStruct((b, t_out, ns), F32))
    res = pl.pallas_call(
        functools.partial(_linear_kernel, glob_off=0, modulate=modulate, act=act,
                          n_main=n_main, has_small=has_small),
        grid=(b, t_out // TOKEN_TILE),
        in_specs=in_specs, out_specs=out_specs, out_shape=out_shape,
        compiler_params=_cparams(("parallel", "parallel")),
        name="linear",
    )(*args)
    return res if has_small else res[0]


def _layer_norm_rows(r, g, b):
    mu = jnp.mean(r, axis=-1, keepdims=True)
    rc = r - mu
    var = jnp.mean(rc * rc, axis=-1, keepdims=True)
    return rc * lax.rsqrt(var + LN_EPS) * g + b


def _head_post(y, gate, norm_g, post):
    head_w, centre, act = post
    parts = []
    for h in range(y.shape[1] // head_w):
        seg = y[:, h * head_w:(h + 1) * head_w]
        if centre:
            seg = seg - jnp.mean(seg, axis=1, keepdims=True)
        parts.append(seg * lax.rsqrt(jnp.mean(seg * seg, axis=1, keepdims=True) + 1e-6))
    return jnp.concatenate(parts, axis=1) * norm_g * act(gate)


def _out_ln_kernel(*refs, tile_off, gate_idx, post):
    if post is None:
        y_ref, w_ref, x_ref, modl_ref, modc_ref, g_ref, b_ref, o_ref = refs
        y = y_ref[0]
    elif post == "rwkv":
        y_ref, r_ref, v_ref, kd0_ref, kd1_ref, gg_ref, pr_ref = refs[:7]
        w_ref, x_ref, modl_ref, modc_ref, g_ref, b_ref, o_ref = refs[7:]
        y = _rwkv_post(y_ref[0], r_ref[0], v_ref[0], kd0_ref[0], kd1_ref[0], gg_ref[0], pr_ref[...])
    else:
        y_ref, gate_ref, ng_ref, w_ref, x_ref, modl_ref, modc_ref, g_ref, b_ref, o_ref = refs
        y = _head_post(y_ref[0], gate_ref[0], ng_ref[...], post)
    is_ctx = (pl.program_id(1) + tile_off) * TOKEN_TILE < CTX_LEN
    gate = _mod_rows(modl_ref, modc_ref, is_ctx, gate_idx)
    f = jnp.dot(y.astype(BF16), w_ref[...], preferred_element_type=F32)
    r = ALPHA * x_ref[0] + gate * f
    o_ref[0] = _layer_norm_rows(r, g_ref[...], b_ref[...])


def out_proj_ln(y, w_bf16, x, mods, ln_g, ln_b, *, gate_idx, tile_off=0, post=None, post_args=None):
    b, t_y, k = y.shape
    modl, modc = mods
    tok = pl.BlockSpec((1, TOKEN_TILE, k), lambda bi, ti: (bi, ti, 0))
    in_specs, args = [tok], [y]
    if post == "rwkv":
        *tiles, params = post_args
        in_specs += [tok] * len(tiles) + [pl.BlockSpec(params.shape, lambda bi, ti: (0, 0))]
        args += [*tiles, params]
    elif post is not None:
        z, gate_block, norm_g = post_args
        in_specs += [pl.BlockSpec((1, TOKEN_TILE, k), lambda bi, ti: (bi, ti, gate_block)),
                     pl.BlockSpec((1, k), lambda bi, ti: (0, 0))]
        args += [z, norm_g.reshape(1, k)]
    in_specs += [pl.BlockSpec((k, D_MODEL), lambda bi, ti: (0, 0)),
                 pl.BlockSpec((1, TOKEN_TILE, D_MODEL), lambda bi, ti: (bi, ti + tile_off, 0)),
                 pl.BlockSpec((1, 1, 6 * D_MODEL), lambda bi, ti: (bi, 0, 0)),
                 pl.BlockSpec((1, 1, 6 * D_MODEL), lambda bi, ti: (0, 0, 0)),
                 pl.BlockSpec((1, D_MODEL), lambda bi, ti: (0, 0)),
                 pl.BlockSpec((1, D_MODEL), lambda bi, ti: (0, 0))]
    args += [w_bf16, x, modl, modc, ln_g.reshape(1, -1), ln_b.reshape(1, -1)]
    return pl.pallas_call(
        functools.partial(_out_ln_kernel, tile_off=tile_off, gate_idx=gate_idx, post=post),
        grid=(b, t_y // TOKEN_TILE),
        in_specs=in_specs,
        out_specs=pl.BlockSpec((1, TOKEN_TILE, D_MODEL), lambda bi, ti: (bi, ti, 0)),
        out_shape=jax.ShapeDtypeStruct((b, t_y, D_MODEL), F32),
        compiler_params=_cparams(("parallel", "parallel")),
        name="out_proj_ln",
    )(*args)


def _seq_prep_kernel(z_ref, cos_ref, sin_ref, cw_ref, o_ref, *, conv, norm_blocks, rope_blocks, scale_blocks, scale):
    x = z_ref[0]
    t = x.shape[0]
    j = pl.program_id(1)
    if conv:
        row = _iota2(t, 1, 0)
        seg_first = (row == 0) | (row == CTX_LEN)
        seg_last = (row == CTX_LEN - 1) | (row == t - 1)
        x_prev = jnp.where(seg_first, 0.0, pltpu.roll(x, 1, axis=0))
        x_next = jnp.where(seg_last, 0.0, pltpu.roll(x, t - 1, axis=0))
        w = cw_ref[...]
        x = _silu(x_prev * w[0:1] + x * w[1:2] + x_next * w[2:3])
    if norm_blocks:
        normed = x * lax.rsqrt(jnp.sum(x * x, axis=1, keepdims=True) + 1e-6)
        x = jnp.where(j < norm_blocks, normed, x)
    lane = _iota2(1, x.shape[1], 1)
    partner = jnp.where((lane % 64) < 32, pltpu.roll(x, 96, axis=1), pltpu.roll(x, 32, axis=1))
    roped = x * cos_ref[...] + partner * sin_ref[...]
    x = jnp.where(j < rope_blocks, roped, x)
    o_ref[0] = jnp.where(j < scale_blocks, x * scale, x)


def _rope_lane_tables(t):
    pos = jnp.arange(t - CTX_LEN)
    inv_freq = ROPE_BASE ** (-jnp.arange(32, dtype=F32) / 32)
    ang_r = (pos // GRID_W).astype(F32)[:, None] * inv_freq[None, :]
    ang_c = (pos % GRID_W).astype(F32)[:, None] * inv_freq[None, :]
    cos = jnp.concatenate([jnp.cos(ang_r), jnp.cos(ang_r), jnp.cos(ang_c), jnp.cos(ang_c)], axis=1)
    sin = jnp.concatenate([-jnp.sin(ang_r), jnp.sin(ang_r), -jnp.sin(ang_c), jnp.sin(ang_c)], axis=1)
    pad = lambda a, v: jnp.concatenate([jnp.full((CTX_LEN, 128), v, F32), a], axis=0)
    return pad(cos, 1.0), pad(sin, 0.0)


def seq_prep(z, n_blocks, *, conv_w=None, norm_blocks, rope_blocks, scale_blocks, scale):
    b, t, _ = z.shape
    cos, sin = _rope_lane_tables(t)
    conv = conv_w is not None
    cw = conv_w if conv else jnp.zeros((3, n_blocks * 128), F32)
    return pl.pallas_call(
        functools.partial(_seq_prep_kernel, conv=conv, norm_blocks=norm_blocks, rope_blocks=rope_blocks,
                          scale_blocks=scale_blocks, scale=scale),
        grid=(b, n_blocks),
        in_specs=[pl.BlockSpec((1, t, 128), lambda bi, ji: (bi, 0, ji)),
                  pl.BlockSpec((t, 128), lambda bi, ji: (0, 0)),
                  pl.BlockSpec((t, 128), lambda bi, ji: (0, 0)),
                  pl.BlockSpec((3, 128), lambda bi, ji: (0, ji))],
        out_specs=pl.BlockSpec((1, t, 128), lambda bi, ji: (bi, 0, ji)),
        out_shape=jax.ShapeDtypeStruct((b, t, n_blocks * 128), F32),
        compiler_params=_cparams(("parallel", "parallel")),
        name="seq_prep",
    )(z, cos, sin, cw)


def _iota2(n, m, axis):
    return lax.broadcasted_iota(jnp.int32, (n, m), axis)


def _row_to_col(row, eye):
    return jnp.sum(jnp.where(eye, row, 0.0), axis=1, keepdims=True)


def _tri_solve_steps(n_mat, rhs_list, eye_f, blockdiag):
    nd = jnp.where(blockdiag, n_mat, 0.0)
    ne = n_mat - nd
    p = eye_f + nd
    n2 = _mm(nd, nd)
    yield
    p = p + _mm(p, n2)
    n4 = _mm(n2, n2)
    yield
    p = p + _mm(p, n4)
    n8 = _mm(n4, n4)
    yield
    dinv = p + _mm(p, n8)
    yield
    m = _mm(dinv, ne)
    xs = [_mm(dinv, r) for r in rhs_list]
    yield
    m2 = _mm(m, m)
    xs = [x + _mm(m, x) for x in xs]
    yield
    xs = [x + _mm(m2, x) for x in xs]
    yield
    return xs


def _run_interleaved(gens):
    gens = list(gens)
    while gens:
        alive = []
        for g in gens:
            try:
                next(g)
                alive.append(g)
            except StopIteration:
                pass
        gens = alive


def _split2(x):
    hi = x.astype(BF16)
    return hi, (x - hi.astype(F32)).astype(BF16)


def _cumsum_rows(x, cum):
    hi, lo = _split2(x)
    cb = cum.astype(BF16)
    return jnp.dot(hi, cb, preferred_element_type=F32) + jnp.dot(lo, cb, preferred_element_type=F32)


def _cumsum_cols(cum, x):
    hi, lo = _split2(x)
    cb = cum.astype(BF16)
    return jnp.dot(cb, hi, preferred_element_type=F32) + jnp.dot(cb, lo, preferred_element_type=F32)


def _chunk_order(direction, n_ctx, n_tot):
    def order(j):
        if direction == 0:
            return j
        return jnp.where(j < n_ctx, n_ctx - 1 - j, n_tot - 1 - (j - n_ctx))
    return order


def _masks(direction):
    ii = _iota2(CHUNK, CHUNK, 0)
    jj = _iota2(CHUNK, CHUNK, 1)
    if direction == 0:
        incl, strict = jj <= ii, jj < ii
    else:
        incl, strict = jj >= ii, jj > ii
    return ii, jj, incl, strict


def _gdn_kernel(q_ref, k_ref, v_ref, la_ref, be_ref, o_ref, g_s, lhs_s, add_s, s_s, ob_s, *, n_ctx, n_tot):
    ii, jj, _, _ = _masks(0)
    eye = ii == jj
    eye_f = eye.astype(F32)
    blockdiag = (ii // 16) == (jj // 16)
    dirs = []
    for d in (0, 1):
        _, _, incl, strict = _masks(d)
        cum = (ii <= jj).astype(F32) if d == 0 else (ii >= jj).astype(F32)
        g_s[d] = _cumsum_rows(la_ref[d, 0, 0], cum)
        dirs.append((incl, strict, CHUNK - 1 if d == 0 else 0))

    def prep(j, carry):
        loaded = []
        for uu in range(GDN_UNROLL):
            n = j * GDN_UNROLL + uu
            rows = pl.ds(pl.multiple_of(n * CHUNK, CHUNK), CHUNK)
            loaded.append((n, q_ref[0, rows, :], k_ref[0, rows, :], v_ref[0, rows, :],
                           [g_s[d, pl.ds(n, 1), :] for d in (0, 1)],
                           [be_ref[d, 0, 0, pl.ds(n, 1), :] for d in (0, 1)]))
        stores = []
        grams = [(_mm_nt(kc, kc), _mm_nt(qc, kc)) for _, qc, kc, _, _, _ in loaded]

        def chain(n, qc, kc, vc, g_row, be_row, kk, qk, d, incl, strict, last):
            g_col = _row_to_col(g_row, eye)
            be_col = _row_to_col(be_row, eye)
            g_last = g_row[:, last:last + 1]
            gamma = jnp.exp(jnp.where(incl, g_col - g_row, NEG))
            n_mat = jnp.where(strict, -(be_col * kk * gamma), 0.0)
            eg = jnp.exp(g_col)
            rhs = jnp.concatenate([be_col * vc, (be_col * eg) * kc], axis=1)
            (uw,) = yield from _tri_solve_steps(n_mat, [rhs], eye_f, blockdiag)
            p_mat = jnp.where(incl, qk * gamma, 0.0)
            p_uw = _mm(p_mat, uw)
            k_dec = kc * jnp.exp(g_last - g_col)
            k_uw = _mm_tn(k_dec, uw)
            q_t = qc * eg - p_uw[:, GDN_DV:]
            stores.append((d, n, jnp.concatenate([q_t, -k_uw[:, GDN_DV:]], axis=0).astype(BF16),
                           jnp.concatenate([p_uw[:, :GDN_DV], k_uw[:, :GDN_DV]], axis=0)))

        _run_interleaved(
            chain(n, qc, kc, vc, g_rows[d], be_rows[d], kk, qk, d, *dirs[d])
            for (n, qc, kc, vc, g_rows, be_rows), (kk, qk) in zip(loaded, grams) for d in (0, 1))
        for d, n, lhs, add in stores:
            lhs_s[d, n] = lhs
            add_s[d, n] = add
        return carry

    lax.fori_loop(0, n_tot // GDN_UNROLL, prep, 0)

    s_s[...] = jnp.zeros_like(s_s)
    orders = [_chunk_order(d, n_ctx, n_tot) for d in (0, 1)]
    o_refs = (o_ref.at[0], ob_s)

    def step(j, carry):
        loaded = []
        for d, (_, _, last) in enumerate(dirs):
            n = orders[d](j)
            loaded.append((n, s_s[d], lhs_s[d, n], add_s[d, n], g_s[d, pl.ds(n, 1), last:last + 1]))
        results = []
        for n, s, lhs, add, g_last in loaded:
            z = jnp.dot(lhs, s.astype(BF16), preferred_element_type=F32) + add
            results.append((n, jnp.exp(g_last) * s + z[CHUNK:], z[:CHUNK]))
        for d, (n, s_new, o) in enumerate(results):
            s_s[d] = s_new
            o_refs[d][pl.ds(pl.multiple_of(n * CHUNK, CHUNK), CHUNK), :] = o
        return carry

    lax.fori_loop(0, n_tot, step, 0)
    o_ref[0] += ob_s[...]


def gdn_scan(qkv, log_alpha, beta):
    b, t, _ = qkv.shape
    n_tot = t // CHUNK
    ncp = log_alpha.shape[3]
    blk = pl.BlockSpec((1, t, GDN_DK), lambda bi, hi: (bi, 0, hi))
    k_blk = pl.BlockSpec((1, t, GDN_DK), lambda bi, hi: (bi, 0, GDN_HEADS + hi))
    v_blk = pl.BlockSpec((1, t, GDN_DV), lambda bi, hi: (bi, 0, 2 * GDN_HEADS + hi))
    gate_blk = pl.BlockSpec((2, 1, 1, ncp, CHUNK), lambda bi, hi: (0, bi, hi, 0, 0))
    return pl.pallas_call(
        functools.partial(_gdn_kernel, n_ctx=CTX_LEN // CHUNK, n_tot=n_tot),
        grid=(b, GDN_HEADS),
        in_specs=[blk, k_blk, v_blk, gate_blk, gate_blk],
        out_specs=blk,
        out_shape=jax.ShapeDtypeStruct((b, t, GDN_HEADS * GDN_DV), F32),
        scratch_shapes=[pltpu.VMEM((2, ncp, CHUNK), F32),
                        pltpu.VMEM((2, n_tot, CHUNK + GDN_DK, GDN_DV), BF16),
                        pltpu.VMEM((2, n_tot, CHUNK + GDN_DK, GDN_DV), F32),
                        pltpu.VMEM((2, GDN_DK, GDN_DV), F32),
                        pltpu.VMEM((t, GDN_DV), F32)],
        compiler_params=_cparams(("parallel", "parallel")),
        name="gdn_scan",
    )(qkv, qkv, qkv, log_alpha, beta)


def _mlstm_kernel(q_ref, k_ref, v_ref, ip_ref, lf_ref, o_ref, b_s, c_s, n_s, m_s, ob_s, *, n_ctx, n_tot):
    ii, jj, _, _ = _masks(0)
    eye = ii == jj
    hps = MLSTM_HEADS_PER_STEP
    dirs = []
    for d in (0, 1):
        _, _, incl, _ = _masks(d)
        cum = (ii <= jj).astype(F32) if d == 0 else (ii >= jj).astype(F32)
        for hh in range(hps):
            b_s[d, hh] = _cumsum_rows(lf_ref[d, 0, hh], cum)
        dirs.append((incl, CHUNK - 1 if d == 0 else 0))
    c_s[...] = jnp.zeros_like(c_s)
    n_s[...] = jnp.zeros_like(n_s)
    m_s[...] = jnp.zeros_like(m_s)
    orders = [_chunk_order(d, n_ctx, n_tot) for d in (0, 1)]
    o_refs = (o_ref.at[0], ob_s)

    def step(j, carry):
        loaded = {}
        for d in (0, 1):
            n = orders[d](j)
            rows = pl.ds(pl.multiple_of(n * CHUNK, CHUNK), CHUNK)
            for hh in range(hps):
                qsl = slice(hh * MLSTM_DQK, (hh + 1) * MLSTM_DQK)
                vsl = slice(hh * MLSTM_DV, (hh + 1) * MLSTM_DV)
                loaded[(d, hh)] = (q_ref[0, rows, qsl], k_ref[0, rows, qsl], v_ref[0, rows, vsl],
                                   b_s[d, hh, pl.ds(n, 1), :], ip_ref[d, 0, hh, pl.ds(n, 1), :],
                                   c_s[d, hh], n_s[d, hh], m_s[d, hh], rows, vsl)
        results = {}

        def chain(key, qc, kc, vc, b_row, ip_row, c_st, n_st, m_st, incl, last):
            b_col = _row_to_col(b_row, eye)
            b_last = b_row[:, last:last + 1]
            qk = _mm_nt(qc, kc)
            qc_st = _mm(qc, c_st)
            qn = jnp.sum(qc * n_st, axis=1, keepdims=True)
            log_end = b_last - b_row + ip_row
            m_end = jnp.max(log_end, axis=1, keepdims=True)
            yield
            log_d = jnp.where(incl, b_col - b_row + ip_row, NEG)
            m_intra = jnp.max(log_d, axis=1, keepdims=True)
            m_new = jnp.maximum(b_last + m_st, m_end)
            decay = jnp.exp(b_last + m_st - m_new)
            kw_col = _row_to_col(jnp.exp(log_end - m_new), eye)
            yield
            m_row = jnp.maximum(b_col + m_st, m_intra)
            w_state = jnp.exp(b_col + m_st - m_row)
            k_w = kc * kw_col
            c_new = decay * c_st + _mm_tn(k_w, vc)
            n_new = decay * n_st + jnp.sum(k_w, axis=0, keepdims=True)
            w_intra = jnp.exp(log_d - m_row) * qk
            num = w_state * qc_st + _mm(w_intra, vc)
            den = w_state * qn + jnp.sum(w_intra, axis=1, keepdims=True)
            yield
            results[key] = (num / jnp.maximum(jnp.abs(den), jnp.exp(-m_row)), c_new, n_new, m_new)

        _run_interleaved(chain(key, *vals[:8], *dirs[key[0]]) for key, vals in loaded.items())
        for (d, hh), (h, c_new, n_new, m_new) in results.items():
            rows, vsl = loaded[(d, hh)][8:]
            c_s[d, hh] = c_new
            n_s[d, hh] = n_new
            m_s[d, hh] = m_new
            o_refs[d][rows, vsl] = h
        return carry

    lax.fori_loop(0, n_tot, step, 0)
    o_ref[0] += ob_s[...]


def mlstm_scan(qk, z, i_pre, log_f):
    b, t, _ = qk.shape
    n_tot = t // CHUNK
    ncp = i_pre.shape[3]
    hps = MLSTM_HEADS_PER_STEP
    n_steps = MLSTM_HEADS // hps
    v_off = 2 * MLSTM_HEADS * MLSTM_DQK // (hps * MLSTM_DV)
    q_blk = pl.BlockSpec((1, t, hps * MLSTM_DQK), lambda bi, hi: (bi, 0, hi))
    k_blk = pl.BlockSpec((1, t, hps * MLSTM_DQK), lambda bi, hi: (bi, 0, n_steps + hi))
    v_blk = pl.BlockSpec((1, t, hps * MLSTM_DV), lambda bi, hi: (bi, 0, v_off + hi))
    o_blk = pl.BlockSpec((1, t, hps * MLSTM_DV), lambda bi, hi: (bi, 0, hi))
    gate_blk = pl.BlockSpec((2, 1, hps, ncp, CHUNK), lambda bi, hi: (0, bi, hi, 0, 0))
    return pl.pallas_call(
        functools.partial(_mlstm_kernel, n_ctx=CTX_LEN // CHUNK, n_tot=n_tot),
        grid=(b, n_steps),
        in_specs=[q_blk, k_blk, v_blk, gate_blk, gate_blk],
        out_specs=o_blk,
        out_shape=jax.ShapeDtypeStruct((b, t, MLSTM_HEADS * MLSTM_DV), F32),
        scratch_shapes=[pltpu.VMEM((2, hps, ncp, CHUNK), F32),
                        pltpu.VMEM((2, hps, MLSTM_DQK, MLSTM_DV), F32),
                        pltpu.VMEM((2, hps, 1, MLSTM_DQK), F32),
                        pltpu.VMEM((2, hps, 1, 1), F32),
                        pltpu.VMEM((t, hps * MLSTM_DV), F32)],
        compiler_params=_cparams(("parallel", "parallel")),
        name="mlstm_scan",
    )(qk, qk, z, i_pre, log_f)


def _rwkv_kernel(r_ref, v_ref, kk_ref, lw0_ref, lw1_ref, kd0_ref, kd1_ref, bv0_ref, bv1_ref, o_ref,
                 lhs_s, add_s, gl_s, s_s, ob_s, *, n_ctx, n_tot):
    dir_refs = ((lw0_ref, kd0_ref, bv0_ref), (lw1_ref, kd1_ref, bv1_ref))
    ii, jj, _, _ = _masks(0)
    eye = ii == jj
    eye_f = eye.astype(F32)
    blockdiag = (ii // 16) == (jj // 16)
    hd = RWKV_HEAD
    dirs = []
    for d in (0, 1):
        _, _, incl, strict = _masks(d)
        dirs.append((incl, strict, CHUNK - 1 if d == 0 else 0))
    zero = jnp.zeros((hd, hd), F32)

    def prep(j, carry):
        loaded = []
        for uu in range(RWKV_UNROLL):
            n = j * RWKV_UNROLL + uu
            rows = pl.ds(pl.multiple_of(n * CHUNK, CHUNK), CHUNK)
            loaded.append((n, r_ref[0, rows, :], v_ref[0, rows, :], kk_ref[0, rows, :],
                           [tuple(ref[0, rows, :] for ref in dir_refs[d]) for d in (0, 1)]))
        parts = {}

        def chain(key, r, v, kk, lw, kd, bv, gcs, e_end, incl, strict):
            e_neg = jnp.exp(-gcs)
            a_h = -kk * jnp.exp(gcs - lw)
            r_h = r * jnp.exp(gcs)
            b_h = bv * e_neg
            k_h = kd * e_neg
            a_ab = jnp.where(strict, _mm_nt(a_h, b_h), 0.0)
            a_ak = jnp.where(strict, _mm_nt(a_h, k_h), 0.0)
            a_rb = jnp.where(incl, _mm_nt(r_h, b_h), 0.0)
            a_rk = jnp.where(incl, _mm_nt(r_h, k_h), 0.0)
            yield
            av = _mm(a_ak, v)
            ta, tav = yield from _tri_solve_steps(a_ab, [a_h, av], eye_f, blockdiag)
            b_g = bv * e_end
            parts[key] = (r_h + _mm(a_rb, ta), _mm_tn(b_g, ta),
                          _mm(a_rb, tav) + _mm(a_rk, v), _mm_tn(b_g, tav) + _mm_tn(kd * e_end, v))

        gens, gls = [], {}
        for uu, (_, r2, v2, kk2, per_dir) in enumerate(loaded):
            for d, (incl, strict, last) in enumerate(dirs):
                lw2, kd2, bv2 = per_dir[d]
                gcs2 = _cumsum_cols(incl.astype(F32), lw2)
                gl_row2 = gcs2[last:last + 1, :]
                e_end2 = jnp.exp(gl_row2 - gcs2)
                gls[(uu, d)] = jnp.exp(gl_row2)
                for hh in range(2):
                    sl = slice(hh * hd, (hh + 1) * hd)
                    gens.append(chain((uu, d, hh), *(a[:, sl] for a in (r2, v2, kk2, lw2, kd2, bv2, gcs2, e_end2)),
                                      incl, strict))
        _run_interleaved(gens)
        for uu, (n, _, _, _, _) in enumerate(loaded):
            for d in (0, 1):
                (rt0, mx0, yc0, kv0), (rt1, mx1, yc1, kv1) = parts[(uu, d, 0)], parts[(uu, d, 1)]
                stack = lambda t0, t1, b0, b1: jnp.concatenate(
                    [jnp.concatenate([t0, t1], axis=1), jnp.concatenate([b0, zero], axis=1),
                     jnp.concatenate([zero, b1], axis=1)], axis=0)
                lhs_s[d, n] = stack(rt0, rt1, mx0, mx1).astype(BF16)
                add_s[d, n] = stack(yc0, yc1, kv0, kv1)
                gl_s[d, pl.ds(n, 1), :] = gls[(uu, d)]
        return carry

    lax.fori_loop(0, n_tot // RWKV_UNROLL, prep, 0)

    s_s[...] = jnp.zeros_like(s_s)
    orders = [_chunk_order(d, n_ctx, n_tot) for d in (0, 1)]
    eye2 = _iota2(2 * hd, 2 * hd, 0) == _iota2(2 * hd, 2 * hd, 1)
    o_refs = (o_ref.at[0], ob_s)

    def step(j, carry):
        loaded = []
        for d in (0, 1):
            n = orders[d](j)
            loaded.append((n, s_s[d], lhs_s[d, n], add_s[d, n], gl_s[d, pl.ds(n, 1), :]))
        results = []
        for n, s, lhs, add, gl in loaded:
            z = jnp.dot(lhs, s.astype(BF16), preferred_element_type=F32) + add
            results.append((n, _row_to_col(gl, eye2) * s + z[CHUNK:], z[:CHUNK]))
        for d, (n, s_new, y) in enumerate(results):
            s_s[d] = s_new
            o_refs[d][pl.ds(pl.multiple_of(n * CHUNK, CHUNK), CHUNK), :] = y
        return carry

    lax.fori_loop(0, n_tot, step, 0)
    o_ref[0] += ob_s[...]


def rwkv_scan(r, v, kk, log_w, k_dir, b_dir):
    b, t, _ = r.shape
    n_tot = t // CHUNK
    blk = pl.BlockSpec((1, t, 128), lambda bi, hi: (bi, 0, hi))
    return pl.pallas_call(
        functools.partial(_rwkv_kernel, n_ctx=CTX_LEN // CHUNK, n_tot=n_tot),
        grid=(b, RWKV_HEADS // 2),
        in_specs=[blk] * 9,
        out_specs=blk,
        out_shape=jax.ShapeDtypeStruct((b, t, D_MODEL), F32),
        scratch_shapes=[pltpu.VMEM((2, n_tot, CHUNK + 2 * RWKV_HEAD, 2 * RWKV_HEAD), BF16),
                        pltpu.VMEM((2, n_tot, CHUNK + 2 * RWKV_HEAD, 2 * RWKV_HEAD), F32),
                        pltpu.VMEM((2, 8 * ((n_tot + 7) // 8), 2 * RWKV_HEAD), F32),
                        pltpu.VMEM((2, 2 * RWKV_HEAD, 2 * RWKV_HEAD), F32),
                        pltpu.VMEM((t, 2 * RWKV_HEAD), F32)],
        compiler_params=_cparams(("parallel", "parallel")),
        name="rwkv_scan",
    )(r, v, kk, *log_w, *k_dir, *b_dir)


def _na_kernel(q_ref, k_ref, v_ref, bias_ref, o_ref, *, rows):
    scale = NA_DH ** -0.5
    slab = NA_WIN_ROWS * GRID_W
    lane = _iota2(1, 2 * NA_DH, 1)
    head_masks = (lane < NA_DH, lane >= NA_DH)
    kc2 = k_ref[0, 0:CTX_LEN, :].astype(BF16)
    vc2 = v_ref[0, 0:CTX_LEN, :].astype(BF16)

    def body(j, carry):
        loaded = []
        for uu in range(NA_ROWS_PER_STEP):
            r = j * NA_ROWS_PER_STEP + uu
            r0 = jnp.clip(r - NA_WIN_ROWS // 2, 0, rows - NA_WIN_ROWS)
            dr0 = r0 - r + NA_WIN_ROWS - 1
            krows = pl.ds(pl.multiple_of(CTX_LEN + r0 * GRID_W, GRID_W), slab)
            loaded.append((r, q_ref[0, pl.ds(pl.multiple_of(CTX_LEN + r * GRID_W, GRID_W), GRID_W), :],
                           k_ref[0, krows, :].astype(BF16), v_ref[0, krows, :].astype(BF16),
                           [bias_ref[hh, pl.ds(dr0, 1)][0] for hh in range(2)]))
        outs = {}

        def chain(key, q2, ks2, vs2, bias, mask):
            qh = jnp.where(mask, q2, 0.0)
            s_lat = _mm_nt(qh, ks2) * scale + bias
            s_ctx = _mm_nt(qh, kc2) * scale
            yield
            m = jnp.maximum(jnp.max(s_lat, axis=1, keepdims=True), jnp.max(s_ctx, axis=1, keepdims=True))
            yield
            p_lat = jnp.exp(s_lat - m)
            p_ctx = jnp.exp(s_ctx - m)
            l = jnp.sum(p_lat, axis=1, keepdims=True) + jnp.sum(p_ctx, axis=1, keepdims=True)
            outs[key] = (_mm(p_lat, vs2) + _mm(p_ctx, vc2)) / l
            yield

        _run_interleaved(chain((uu, hh), q2, ks2, vs2, biases[hh], head_masks[hh])
                         for uu, (_, q2, ks2, vs2, biases) in enumerate(loaded) for hh in range(2))
        for uu, (r, _, _, _, _) in enumerate(loaded):
            o_ref[0, pl.ds(pl.multiple_of(r * GRID_W, GRID_W), GRID_W), :] = jnp.where(
                head_masks[0], outs[(uu, 0)], outs[(uu, 1)])
        return carry

    lax.fori_loop(0, rows // NA_ROWS_PER_STEP, body, 0)


def _na_bias_table(rpb):
    cols = np.arange(GRID_W)
    win_c0 = np.clip(cols - NA_WIN_COLS // 2, 0, GRID_W - NA_WIN_COLS)
    kc = np.arange(GRID_W)
    in_win = (kc[None, :] >= win_c0[:, None]) & (kc[None, :] < win_c0[:, None] + NA_WIN_COLS)
    dc = np.clip(kc[None, :] - cols[:, None] + NA_WIN_COLS - 1, 0, 2 * NA_WIN_COLS - 2)
    dr = np.arange(NA_WIN_ROWS)[:, None] + np.arange(NA_WIN_ROWS)[None, :]
    tab = rpb.astype(F32)[:, dr][:, :, :, dc]
    tab = jnp.where(in_win[None, None, None], tab, NEG)
    tab = tab.transpose(0, 1, 3, 2, 4)
    return tab.reshape(NA_HEADS, NA_WIN_ROWS, GRID_W, NA_WIN_ROWS * GRID_W)


def na_attention(z, bias_tab):
    b, t, _ = z.shape
    t_lat = t - CTX_LEN
    n_pairs = NA_HEADS // 2
    return pl.pallas_call(
        functools.partial(_na_kernel, rows=t_lat // GRID_W),
        grid=(n_pairs, b),
        in_specs=[pl.BlockSpec((1, t, 128), lambda hi, bi: (bi, 0, hi)),
                  pl.BlockSpec((1, t, 128), lambda hi, bi: (bi, 0, n_pairs + hi)),
                  pl.BlockSpec((1, t, 128), lambda hi, bi: (bi, 0, 2 * n_pairs + hi)),
                  pl.BlockSpec((2, NA_WIN_ROWS, GRID_W, NA_WIN_ROWS * GRID_W), lambda hi, bi: (hi, 0, 0, 0))],
        out_specs=pl.BlockSpec((1, t_lat, 128), lambda hi, bi: (bi, 0, hi)),
        out_shape=jax.ShapeDtypeStruct((b, t_lat, D_MODEL), F32),
        compiler_params=_cparams(("parallel", "parallel")),
        name="na_attention",
    )(z, z, z, bias_tab)


def _router_kernel(x_ref, modl_ref, modc_ref, rw_ref, rb_ref, hb_ref, idx_ref, wt_ref, *, tile_off):
    is_ctx = (pl.program_id(1) + tile_off) * TOKEN_TILE < CTX_LEN
    sh = _mod_rows(modl_ref, modc_ref, is_ctx, 3)
    sc = _mod_rows(modl_ref, modc_ref, is_ctx, 4)
    h = x_ref[0] * (1.0 + sc) + sh
    hb_ref[0] = h.astype(BF16)
    logits = lax.dot_general(rw_ref[...], h, (((1,), (1,)), ((), ())), preferred_element_type=F32,
                             precision=_HI)
    mx = jnp.max(logits, axis=0, keepdims=True)
    ex = jnp.exp(logits - mx)
    probs = ex / jnp.sum(ex, axis=0, keepdims=True)
    sel = probs + rb_ref[...]
    tm = sel.shape[1]
    e_id = _iota2(N_EXPERTS, tm, 0)
    per_group = N_EXPERTS // N_GROUPS
    g_id = e_id // per_group

    def top1(vals):
        m1 = jnp.max(vals, axis=0, keepdims=True)
        i1 = jnp.min(jnp.where(vals == m1, e_id, N_EXPERTS), axis=0, keepdims=True)
        return m1, i1

    best_score, best = None, None
    for g in range(N_GROUPS):
        vals = jnp.where(g_id == g, sel, NEG)
        m1, i1 = top1(vals)
        m2, _ = top1(jnp.where(e_id == i1, NEG, vals))
        score = m1 + m2
        if g == 0:
            best_score, best = score, jnp.zeros_like(i1)
        else:
            better = score > best_score
            best = jnp.where(better, g, best)
            best_score = jnp.where(better, score, best_score)
    vals = jnp.where(g_id == best, sel, NEG)
    _, i1 = top1(vals)
    _, i2 = top1(jnp.where(e_id == i1, NEG, vals))
    p1 = jnp.sum(jnp.where(e_id == i1, probs, 0.0), axis=0, keepdims=True)
    p2 = jnp.sum(jnp.where(e_id == i2, probs, 0.0), axis=0, keepdims=True)
    tot = p1 + p2
    idx_ref[0] = jnp.where(_iota2(2, tm, 0) == 0, i1, i2)
    eye = _iota2(tm, tm, 0) == _iota2(tm, tm, 1)
    wt_ref[0] = jnp.where(_iota2(tm, 2, 1) == 0, _row_to_col(p1 / tot, eye), _row_to_col(p2 / tot, eye))


def moe_route(x, mods, router_w, router_b, *, tile_off):
    b, t_out, _ = x.shape
    modl, modc = mods
    return pl.pallas_call(
        functools.partial(_router_kernel, tile_off=tile_off),
        grid=(b, t_out // TOKEN_TILE),
        in_specs=[pl.BlockSpec((1, TOKEN_TILE, D_MODEL), lambda bi, ti: (bi, ti, 0)),
                  pl.BlockSpec((1, 1, 6 * D_MODEL), lambda bi, ti: (bi, 0, 0)),
                  pl.BlockSpec((1, 1, 6 * D_MODEL), lambda bi, ti: (0, 0, 0)),
                  pl.BlockSpec((N_EXPERTS, D_MODEL), lambda bi, ti: (0, 0)),
                  pl.BlockSpec((N_EXPERTS, 1), lambda bi, ti: (0, 0))],
        out_specs=[pl.BlockSpec((1, TOKEN_TILE, D_MODEL), lambda bi, ti: (bi, ti, 0)),
                   pl.BlockSpec((1, 2, TOKEN_TILE), lambda bi, ti: (bi, 0, ti)),
                   pl.BlockSpec((1, TOKEN_TILE, 2), lambda bi, ti: (bi, ti, 0))],
        out_shape=[jax.ShapeDtypeStruct((b, t_out, D_MODEL), BF16),
                   jax.ShapeDtypeStruct((b, 2, t_out), jnp.int32),
                   jax.ShapeDtypeStruct((b, t_out, 2), F32)],
        compiler_params=_cparams(("parallel", "parallel")),
        name="moe_route",
    )(x, modl, modc, router_w.T, router_b.reshape(-1, 1))


def _ffn_kernel(te_ref, nt_ref, x_ref, w1_ref, w3_ref, w2_ref, o_ref):
    @pl.when(pl.program_id(0) < nt_ref[0])
    def _():
        xb = x_ref[...]
        h1 = jnp.dot(xb, w1_ref[0, 0].astype(BF16), preferred_element_type=F32)
        h3 = jnp.dot(xb, w3_ref[0, 0].astype(BF16), preferred_element_type=F32)
        hid = (_silu(h1) * h3).astype(BF16)
        o_ref[...] = jnp.dot(hid, w2_ref[0, 0].astype(BF16), preferred_element_type=F32).astype(o_ref.dtype)

    @pl.when(pl.program_id(0) >= nt_ref[0])
    def _():
        o_ref[...] = jnp.zeros_like(o_ref)


def expert_ffn(xs, tile_expert, n_tiles_used, w1, w3, w2, layer):
    p = xs.shape[0]
    grid_spec = pltpu.PrefetchScalarGridSpec(
        num_scalar_prefetch=2,
        grid=(p // FFN_TILE,),
        in_specs=[pl.BlockSpec((FFN_TILE, D_MODEL), lambda i, te, nt: (i, 0)),
                  pl.BlockSpec((1, 1, D_MODEL, D_EXPERT), lambda i, te, nt: (layer, te[i], 0, 0)),
                  pl.BlockSpec((1, 1, D_MODEL, D_EXPERT), lambda i, te, nt: (layer, te[i], 0, 0)),
                  pl.BlockSpec((1, 1, D_EXPERT, D_MODEL), lambda i, te, nt: (layer, te[i], 0, 0))],
        out_specs=pl.BlockSpec((FFN_TILE, D_MODEL), lambda i, te, nt: (i, 0)),
    )
    return pl.pallas_call(
        _ffn_kernel,
        grid_spec=grid_spec,
        out_shape=jax.ShapeDtypeStruct((p, D_MODEL), BF16),
        compiler_params=_cparams(("arbitrary",)),
        name="expert_ffn",
    )(tile_expert, n_tiles_used, xs, w1, w3, w2)


def _combine_ln_kernel(x_ref, ya0_ref, ya1_ref, yb0_ref, yb1_ref, wt_ref, modl_ref, modc_ref, g_ref, b_ref, o_ref,
                       *, tile_off, b_half):
    is_ctx = (pl.program_id(1) + tile_off) * TOKEN_TILE < CTX_LEN
    gate = _mod_rows(modl_ref, modc_ref, is_ctx, 5)
    wt = wt_ref[0]
    first = pl.program_id(0) < b_half
    y0 = jnp.where(first, ya0_ref[0, 0], yb0_ref[0, 0]).astype(F32)
    y1 = jnp.where(first, ya1_ref[0, 0], yb1_ref[0, 0]).astype(F32)
    f = wt[:, 0:1] * y0 + wt[:, 1:2] * y1
    r = ALPHA * x_ref[0] + gate * f
    o_ref[0] = _layer_norm_rows(r, g_ref[...], b_ref[...])


def combine_ln(x, y01_halves, wt, mods, ln_g, ln_b, *, tile_off):
    ya, yb = y01_halves
    _, bh, t_out, _ = ya.shape
    b = 2 * bh
    modl, modc = mods
    tok = lambda bi, ti: (bi, ti, 0)
    half_blk = lambda slot, lo: pl.BlockSpec(
        (1, 1, TOKEN_TILE, D_MODEL), lambda bi, ti: (slot, jnp.clip(bi - lo, 0, bh - 1), ti, 0))
    return pl.pallas_call(
        functools.partial(_combine_ln_kernel, tile_off=tile_off, b_half=bh),
        grid=(b, t_out // TOKEN_TILE),
        in_specs=[pl.BlockSpec((1, TOKEN_TILE, D_MODEL), tok),
                  half_blk(0, 0), half_blk(1, 0), half_blk(0, bh), half_blk(1, bh),
                  pl.BlockSpec((1, TOKEN_TILE, 2), tok),
                  pl.BlockSpec((1, 1, 6 * D_MODEL), lambda bi, ti: (bi, 0, 0)),
                  pl.BlockSpec((1, 1, 6 * D_MODEL), lambda bi, ti: (0, 0, 0)),
                  pl.BlockSpec((1, D_MODEL), lambda bi, ti: (0, 0)),
                  pl.BlockSpec((1, D_MODEL), lambda bi, ti: (0, 0))],
        out_specs=pl.BlockSpec((1, TOKEN_TILE, D_MODEL), tok),
        out_shape=jax.ShapeDtypeStruct((b, t_out, D_MODEL), F32),
        compiler_params=_cparams(("parallel", "parallel")),
        name="moe_combine_ln",
    )(x, ya, ya, yb, yb, wt, modl, modc, ln_g.reshape(1, -1), ln_b.reshape(1, -1))


def _expert_outputs(hb_flat, idx, tok_off, w1, w3, w2, layer):
    b, _, t = idx.shape
    n_tok = b * t
    n_pair = 2 * n_tok
    e_flat = idx.transpose(1, 0, 2).reshape(n_pair)
    onehot = (e_flat[:, None] == jnp.arange(N_EXPERTS)[None, :]).astype(jnp.int32)
    csum = jnp.cumsum(onehot, axis=0)
    counts = csum[-1]
    rank = jnp.sum((csum - onehot) * onehot, axis=1)
    padded = ((counts + FFN_TILE - 1) // FFN_TILE) * FFN_TILE
    ends = jnp.cumsum(padded)
    offs = ends - padded
    pos = offs[e_flat] + rank
    n_rows = n_pair + N_EXPERTS * FFN_TILE
    n_tiles = n_rows // FFN_TILE
    src = jnp.zeros((n_rows,), jnp.int32).at[pos].set(
        jnp.arange(n_pair, dtype=jnp.int32) % n_tok + tok_off, mode="promise_in_bounds", unique_indices=True)
    tile_start = jnp.arange(n_tiles, dtype=jnp.int32) * FFN_TILE
    tile_expert = jnp.minimum(jnp.searchsorted(ends, tile_start, side="right"), N_EXPERTS - 1).astype(jnp.int32)
    n_used = (ends[-1] // FFN_TILE).astype(jnp.int32).reshape(1)
    xs = hb_flat.at[src].get(mode="promise_in_bounds")
    ys = expert_ffn(xs, tile_expert, n_used, w1, w3, w2, layer)
    return ys.at[pos].get(mode="promise_in_bounds", unique_indices=True).reshape(2, b, t, D_MODEL)


def moe_layer(x, mods, router_w, router_b, w1, w3, w2, layer, ln_g, ln_b, *, tile_off):
    hb, idx, wt = moe_route(x, mods, router_w, router_b, tile_off=tile_off)
    b, t, _ = hb.shape
    bh = b // 2
    hb_flat = hb.reshape(b * t, D_MODEL)
    halves = [_expert_outputs(hb_flat, idx[s * bh:(s + 1) * bh], s * bh * t, w1, w3, w2, layer) for s in range(2)]
    return combine_ln(x, halves, wt, mods, ln_g, ln_b, tile_off=tile_off)


def _gate_rows(g):
    b, t = g.shape[:2]
    n = t // CHUNK
    ncp = 8 * ((n + 7) // 8)
    g = g.transpose(2, 3, 0, 4, 1).reshape(2, 2, b, g.shape[-1], n, CHUNK)
    return jnp.pad(g, ((0, 0),) * 4 + ((0, ncp - n), (0, 0)))


def gdn_layer(x, mods, w_in, conv_w, a_log, dt_bias, norm_g, w_out):
    wq = GDN_HEADS * GDN_DK
    n_main = 4 * wq
    z, ab = linear(x, w_in[:, :n_main].astype(BF16), mods=mods, modulate=(0, 1), w_small=w_in[:, n_main:])
    qkv = seq_prep(z, 3 * GDN_HEADS, conv_w=conv_w, norm_blocks=2 * GDN_HEADS, rope_blocks=2 * GDN_HEADS,
                   scale_blocks=GDN_HEADS, scale=GDN_DK ** -0.5)
    b, t, _ = x.shape
    ab = ab.reshape(b, t, 2, 2, GDN_HEADS)
    ab = ab.at[:, :, :, 0].add(dt_bias[None, None])
    rows = _gate_rows(ab)
    log_alpha = -jnp.exp(a_log)[:, None, :, None, None] * jax.nn.softplus(rows[:, 0])
    beta = jax.nn.sigmoid(rows[:, 1])
    valid = (jnp.arange(rows.shape[4]) < t // CHUNK)[:, None]
    log_alpha = jnp.where(valid, log_alpha, 0.0)
    o = gdn_scan(qkv, log_alpha, beta)
    return o, w_out, (GDN_DV, False, _silu), (z, 3, jnp.tile(norm_g, GDN_HEADS))


def mlstm_layer(x, mods, w_in, gate_b, norm_g, w_out):
    wq = MLSTM_HEADS * MLSTM_DQK
    wv = MLSTM_HEADS * MLSTM_DV
    n_main = 2 * wq + 2 * wv
    z, gt = linear(x, w_in[:, :n_main].astype(BF16), mods=mods, modulate=(0, 1), w_small=w_in[:, n_main:])
    qk = seq_prep(z, 2 * MLSTM_HEADS, norm_blocks=0, rope_blocks=2 * MLSTM_HEADS, scale_blocks=MLSTM_HEADS,
                  scale=MLSTM_DQK ** -0.5)
    b, t, _ = x.shape
    gt = gt.reshape(b, t, 2, 2, MLSTM_HEADS) + gate_b[None, None]
    rows = _gate_rows(gt)
    valid = (jnp.arange(rows.shape[4]) < t // CHUNK)[:, None]
    i_pre = rows[:, 0]
    log_f = jnp.where(valid, jax.nn.log_sigmoid(rows[:, 1]), 0.0)
    h = mlstm_scan(qk, z, i_pre, log_f)
    return h, w_out, (MLSTM_DV, True, _sigmoid), (z, 2, norm_g)


RWKV_IN_COLS = 3584


def _rwkv_in_kernel(x_ref, modl_ref, modc_ref, w_ref, o_ref, hd_s):
    t = x_ref.shape[1]

    @pl.when(pl.program_id(1) == 0)
    def _():
        row = _iota2(t, 1, 0)
        is_ctx = row < CTX_LEN
        sel = lambda i: jnp.where(is_ctx, modc_ref[0, :, i * D_MODEL:(i + 1) * D_MODEL],
                                  modl_ref[0, :, i * D_MODEL:(i + 1) * D_MODEL])
        h = x_ref[0] * (1.0 + sel(1)) + sel(0)
        seg_first = (row == 0) | (row == CTX_LEN)
        seg_last = (row == CTX_LEN - 1) | (row == t - 1)
        h_prev = jnp.where(seg_first, 0.0, pltpu.roll(h, 1, axis=0))
        h_next = jnp.where(seg_last, 0.0, pltpu.roll(h, t - 1, axis=0))
        hd_s[:, :D_MODEL] = h.astype(BF16)
        hd_s[:, D_MODEL:] = (0.5 * (h_prev + h_next) - h).astype(BF16)

    for i in range(t // TOKEN_TILE):
        rows = slice(i * TOKEN_TILE, (i + 1) * TOKEN_TILE)
        o_ref[0, rows, :] = jnp.dot(hd_s[rows, :], w_ref[...], preferred_element_type=F32)


def rwkv_in_proj(x, mods, w_big):
    b, t, _ = x.shape
    modl, modc = mods
    n = w_big.shape[1]
    return pl.pallas_call(
        _rwkv_in_kernel,
        grid=(b, n // N_CHUNK_COLS),
        in_specs=[pl.BlockSpec((1, t, D_MODEL), lambda bi, ji: (bi, 0, 0)),
                  pl.BlockSpec((1, 1, 6 * D_MODEL), lambda bi, ji: (bi, 0, 0)),
                  pl.BlockSpec((1, 1, 6 * D_MODEL), lambda bi, ji: (0, 0, 0)),
                  pl.BlockSpec((2 * D_MODEL, N_CHUNK_COLS), lambda bi, ji: (0, ji))],
        out_specs=pl.BlockSpec((1, t, N_CHUNK_COLS), lambda bi, ji: (bi, 0, ji)),
        out_shape=jax.ShapeDtypeStruct((b, t, n), F32),
        scratch_shapes=[pltpu.VMEM((t, 2 * D_MODEL), BF16)],
        compiler_params=_cparams(("parallel", "arbitrary")),
        name="rwkv_in_proj",
    )(x, modl, modc, w_big)


def _seg64_sums(x):
    left = _iota2(1, 128, 1) < RWKV_HEAD
    parts = []
    for blk_i in range(x.shape[1] // 128):
        blk = x[:, blk_i * 128:(blk_i + 1) * 128]
        s_l = jnp.sum(jnp.where(left, blk, 0.0), axis=1, keepdims=True)
        s_r = jnp.sum(jnp.where(left, 0.0, blk), axis=1, keepdims=True)
        parts.append(jnp.where(left, s_l, s_r))
    return jnp.concatenate(parts, axis=1)


def _softplus(x):
    return jnp.maximum(x, 0.0) + jnp.log(1.0 + jnp.exp(-jnp.abs(x)))


def _rwkv_mid_kernel(z_ref, w2_ref, a2_ref, g2_ref, vec_ref, r_ref, v_ref, kk_ref, lw0_ref, lw1_ref,
                     kd0_ref, kd1_ref, bv0_ref, bv1_ref, g_ref):
    d = D_MODEL
    z = z_ref[0]
    r, k, v = z[:, :d], z[:, d:2 * d], z[:, 2 * d:3 * d]
    zg, zw, za = z[:, 3 * d:3 * d + 128], z[:, 3 * d + 128:3 * d + 256], z[:, 3 * d + 256:3 * d + 384]
    vec = vec_ref[...]
    k_k, k_a = vec[4:5], vec[5:6]
    r_ref[0] = r
    v_ref[0] = v
    g_ref[0] = _mm(_sigmoid(zg), g2_ref[...])
    kx = k * k_k
    kk = kx * lax.rsqrt(_seg64_sums(kx * kx) + 1e-6)
    kk_ref[0] = kk
    tw = jnp.tanh(zw)
    for dr, (lw_ref, kd_ref, bv_ref) in enumerate(((lw0_ref, kd0_ref, bv0_ref), (lw1_ref, kd1_ref, bv1_ref))):
        w_raw = -_softplus(-(vec[dr:dr + 1] + _mm(tw, w2_ref[dr]))) - 0.5
        a = _sigmoid(vec[2 + dr:3 + dr] + _mm(za, a2_ref[dr]))
        lw_ref[0] = -jnp.exp(w_raw)
        kd_ref[0] = k * (1.0 + (a - 1.0) * k_a)
        bv_ref[0] = kk * a


def rwkv_mid(z1, w2p, a2p, g2, vec):
    b, t, n = z1.shape
    tok = pl.BlockSpec((1, TOKEN_TILE, D_MODEL), lambda bi, ti: (bi, ti, 0))
    full = lambda a: pl.BlockSpec(a.shape, lambda bi, ti: (0,) * a.ndim)
    return pl.pallas_call(
        _rwkv_mid_kernel,
        grid=(b, t // TOKEN_TILE),
        in_specs=[pl.BlockSpec((1, TOKEN_TILE, n), lambda bi, ti: (bi, ti, 0)),
                  full(w2p), full(a2p), full(g2), full(vec)],
        out_specs=[tok] * 10,
        out_shape=[jax.ShapeDtypeStruct((b, t, D_MODEL), F32)] * 10,
        compiler_params=_cparams(("parallel", "parallel")),
        name="rwkv_mid",
    )(z1, w2p, a2p, g2, vec)


def _rwkv_post(y, r, v, kd0, kd1, g, params):
    inv = 1.0 / RWKV_HEAD
    yc = y - _seg64_sums(y) * inv
    yn = yc * lax.rsqrt(_seg64_sums(yc * yc) * inv + RWKV_GN_EPS)
    bonus = _seg64_sums(r * (kd0 + kd1) * params[2:3]) * v
    return (yn * params[0:1] + params[1:2] + bonus) * g


def rwkv_layer(x, mods, mu, w_rkv, w0, w1, w2, a0, a1, a2, g1, g2, k_k, k_a, r_k, lnx_g, lnx_b, w_out):
    d = D_MODEL
    cols = [(w_rkv[0], 0), (w_rkv[1], 2), (w_rkv[2], 3), (g1, 5), (w1[0], 1), (w1[1], 1), (a1[0], 4), (a1[1], 4)]
    top = jnp.concatenate([w for w, _ in cols], axis=1)
    bot = jnp.concatenate([mu[j][:, None] * w for w, j in cols], axis=1)
    w_big = jnp.pad(jnp.concatenate([top, bot], axis=0), ((0, 0), (0, RWKV_IN_COLS - top.shape[1]))).astype(BF16)
    z1 = rwkv_in_proj(x, mods, w_big)
    pad_dir = lambda w: jnp.stack([jnp.pad(w[0], ((0, w.shape[1]), (0, 0))), jnp.pad(w[1], ((w.shape[1], 0), (0, 0)))])
    vec = jnp.concatenate([w0, a0, k_k[None], k_a[None], jnp.zeros((2, d), F32)], axis=0)
    r, v, kk, lw0, lw1, kd0, kd1, bv0, bv1, g = rwkv_mid(z1, pad_dir(w2), pad_dir(a2), g2, vec)
    y = rwkv_scan(r, v, kk, (lw0, lw1), (kd0, kd1), (bv0, bv1))
    params = jnp.concatenate([lnx_g[None], lnx_b[None], r_k.reshape(1, d), jnp.zeros((5, d), F32)], axis=0)
    return y, w_out, "rwkv", (r, v, kd0, kd1, g, params)


def kernel(x, c, ctx, c_ctx, ada_w, ada_b, ln_g, ln_b, router_w, router_b, moe_w1, moe_w3, moe_w2, gdn_w_in, gdn_conv, gdn_a_log, gdn_dt_bias, gdn_norm_g, gdn_w_out, mlstm_w_in, mlstm_gate_b, mlstm_norm_g, mlstm_w_out, rwkv_mu, rwkv_w_rkv, rwkv_w0, rwkv_w1, rwkv_w2, rwkv_a0, rwkv_a1, rwkv_a2, rwkv_g1, rwkv_g2, rwkv_k_k, rwkv_k_a, rwkv_r_k, rwkv_lnx_g, rwkv_lnx_b, rwkv_w_out, na_w_in, na_rpb, na_w_out):
    b = x.shape[0]
    ctx_tiles = CTX_LEN // TOKEN_TILE
    mod_all = modulation_all(c, c_ctx, ada_w, ada_b)
    xs = jnp.concatenate([ctx, x], axis=1)
    assert DEPTH == 4
    for i in range(DEPTH):
        mods = (mod_all[i, :b, None, :], mod_all[i, b:b + 1, None, :])
        if i % 4 == 0:
            y, w_out, post, post_args = gdn_layer(xs, mods, gdn_w_in, gdn_conv, gdn_a_log, gdn_dt_bias,
                                                  gdn_norm_g, gdn_w_out)
        elif i % 4 == 1:
            y, w_out, post, post_args = mlstm_layer(xs, mods, mlstm_w_in, mlstm_gate_b, mlstm_norm_g, mlstm_w_out)
        elif i % 4 == 2:
            y, w_out, post, post_args = rwkv_layer(xs, mods, rwkv_mu, rwkv_w_rkv, rwkv_w0, rwkv_w1, rwkv_w2, rwkv_a0, rwkv_a1,
                                  rwkv_a2, rwkv_g1, rwkv_g2, rwkv_k_k, rwkv_k_a, rwkv_r_k, rwkv_lnx_g,
                                  rwkv_lnx_b, rwkv_w_out)
        else:
            z = linear(xs, na_w_in.astype(BF16), mods=mods, modulate=(0, 1))
            y, w_out, post, post_args = na_attention(z, _na_bias_table(na_rpb)), na_w_out, None, None
        off = ctx_tiles if (i % 4 == 3) else 0
        xs1 = out_proj_ln(y, w_out.astype(BF16), xs, mods, ln_g[i, 0], ln_b[i, 0], gate_idx=2, tile_off=off,
                          post=post, post_args=post_args)
        xs = moe_layer(xs1, mods, router_w, router_b, moe_w1, moe_w3, moe_w2, i, ln_g[i, 1], ln_b[i, 1],
                       tile_off=off)
    return xs
```

```python
import functools

import numpy as np
import jax
import jax.numpy as jnp
from jax import lax
from jax.experimental import pallas as pl
from jax.experimental.pallas import tpu as pltpu

F32 = jnp.float32
BF16 = jnp.bfloat16

D_MODEL = 1024
DEPTH = 4
GRID_W = 64
CTX_LEN = 256
ALPHA = (2 * DEPTH) ** 0.25
LN_EPS = 1e-5
ROPE_BASE = 10000.0

GDN_HEADS = 8
GDN_DK = 128
GDN_DV = 128
MLSTM_HEADS = 4
MLSTM_DQK = 128
MLSTM_DV = 256
RWKV_HEAD = 64
RWKV_HEADS = 16
RWKV_GN_EPS = 64e-5
NA_HEADS = 16
NA_DH = 64
NA_WIN_ROWS = 8
NA_WIN_COLS = 16
N_EXPERTS = 16
N_GROUPS = 4
D_EXPERT = 512

CHUNK = 64
GDN_UNROLL = 9
RWKV_UNROLL = 4
NA_ROWS_PER_STEP = 4
MLSTM_HEADS_PER_STEP = 2
TOKEN_TILE = 256
N_CHUNK_COLS = 512
FFN_TILE = 512
NEG = -1e30
VMEM_LIMIT = 56 * 1024 * 1024

_HI = lax.Precision.HIGHEST


def _cparams(sem):
    return pltpu.CompilerParams(dimension_semantics=sem, vmem_limit_bytes=VMEM_LIMIT)


def _mm(a, b):
    return jnp.dot(a.astype(BF16), b.astype(BF16), preferred_element_type=F32)


def _mm_nt(a, b):
    return lax.dot_general(a.astype(BF16), b.astype(BF16), (((1,), (1,)), ((), ())),
                           preferred_element_type=F32)


def _mm_tn(a, b):
    return lax.dot_general(a.astype(BF16), b.astype(BF16), (((0,), (0,)), ((), ())),
                           preferred_element_type=F32)


def _mmf(a, b):
    return jnp.dot(a, b, preferred_element_type=F32, precision=_HI)


def _silu(x):
    return x * (1.0 / (1.0 + jnp.exp(-x)))


def _sigmoid(x):
    return 1.0 / (1.0 + jnp.exp(-x))


def _mod_kernel(s_ref, w_ref, b_ref, o_ref):
    o_ref[0] = _mmf(_silu(s_ref[...]), w_ref[0]) + b_ref[0]


def modulation_all(c, c_ctx, ada_w, ada_b):
    b = c.shape[0]
    rows = 8 * ((b + 1 + 7) // 8)
    s = jnp.zeros((rows, D_MODEL), F32).at[:b].set(c).at[b].set(c_ctx)
    tn = 1536
    n = ada_w.shape[-1]
    return pl.pallas_call(
        _mod_kernel,
        grid=(DEPTH, n // tn),
        in_specs=[pl.BlockSpec((rows, D_MODEL), lambda i, j: (0, 0)),
                  pl.BlockSpec((1, D_MODEL, tn), lambda i, j: (i, 0, j)),
                  pl.BlockSpec((1, 1, tn), lambda i, j: (i, 0, j))],
        out_specs=pl.BlockSpec((1, rows, tn), lambda i, j: (i, 0, j)),
        out_shape=jax.ShapeDtypeStruct((DEPTH, rows, n), F32),
        compiler_params=_cparams(("arbitrary", "arbitrary")),
        name="adaln_modulation",
    )(s, ada_w, ada_b.reshape(DEPTH, 1, n))


def _mod_rows(modl_ref, modc_ref, is_ctx, idx):
    sl = slice(idx * D_MODEL, (idx + 1) * D_MODEL)
    return jnp.where(is_ctx, modc_ref[0, :, sl], modl_ref[0, :, sl])


def _linear_kernel(*refs, glob_off, modulate, act, n_main, has_small):
    it = iter(refs)
    x_ref = next(it)
    if modulate is not None:
        modl_ref, modc_ref = next(it), next(it)
    w_ref = next(it)
    ws_ref = next(it) if has_small else None
    o_ref = next(it)
    os_ref = next(it) if has_small else None

    h = x_ref[0]
    if modulate is not None:
        is_ctx = (pl.program_id(1) + glob_off) * TOKEN_TILE < CTX_LEN
        sh = _mod_rows(modl_ref, modc_ref, is_ctx, modulate[0])
        sc = _mod_rows(modl_ref, modc_ref, is_ctx, modulate[1])
        h = h * (1.0 + sc) + sh
    if act == "tanh":
        h = jnp.tanh(h)
    elif act == "sigmoid":
        h = _sigmoid(h)
    hb = h.astype(BF16)
    step = min(N_CHUNK_COLS, n_main)
    for j in range(n_main // step):
        o_ref[0, :, j * step:(j + 1) * step] = jnp.dot(
            hb, w_ref[:, j * step:(j + 1) * step], preferred_element_type=F32)
    if has_small:
        os_ref[0] = _mmf(h, ws_ref[...])


def linear(x, w_bf16, *, mods=None, modulate=None, act=None, w_small=None):
    b, t_out, k = x.shape
    n_main = w_bf16.shape[1]
    has_small = w_small is not None
    in_specs = [pl.BlockSpec((1, TOKEN_TILE, k), lambda bi, ti: (bi, ti, 0))]
    args = [x]
    if modulate is not None:
        modl, modc = mods
        in_specs += [pl.BlockSpec((1, 1, 6 * D_MODEL), lambda bi, ti: (bi, 0, 0)),
                     pl.BlockSpec((1, 1, 6 * D_MODEL), lambda bi, ti: (0, 0, 0))]
        args += [modl, modc]
    in_specs.append(pl.BlockSpec((k, n_main), lambda bi, ti: (0, 0)))
    args.append(w_bf16)
    out_specs = [pl.BlockSpec((1, TOKEN_TILE, n_main), lambda bi, ti: (bi, ti, 0))]
    out_shape = [jax.ShapeDtypeStruct((b, t_out, n_main), F32)]
    if has_small:
        ns = w_small.shape[1]
        in_specs.append(pl.BlockSpec((k, ns), lambda bi, ti: (0, 0)))
        args.append(w_small)
        out_specs.append(pl.BlockSpec((1, TOKEN_TILE, ns), lambda bi, ti: (bi, ti, 0)))
        out_shape.append(jax.ShapeDtypeStruct((b, t_out, ns), F32))
    res = pl.pallas_call(
        functools.partial(_linear_kernel, glob_off=0, modulate=modulate, act=act,
                          n_main=n_main, has_small=has_small),
        grid=(b, t_out // TOKEN_TILE),
        in_specs=in_specs, out_specs=out_specs, out_shape=out_shape,
        compiler_params=_cparams(("parallel", "parallel")),
        name="linear",
    )(*args)
    return res if has_small else res[0]


def _layer_norm_rows(r, g, b):
    mu = jnp.mean(r, axis=-1, keepdims=True)
    rc = r - mu
    var = jnp.mean(rc * rc, axis=-1, keepdims=True)
    return rc * lax.rsqrt(var + LN_EPS) * g + b


def _head_post(y, gate, norm_g, post):
    head_w, centre, act = post
    parts = []
    for h in range(y.shape[1] // head_w):
        seg = y[:, h * head_w:(h + 1) * head_w]
        if centre:
            seg = seg - jnp.mean(seg, axis=1, keepdims=True)
        parts.append(seg * lax.rsqrt(jnp.mean(seg * seg, axis=1, keepdims=True) + 1e-6))
    return jnp.concatenate(parts, axis=1) * norm_g * act(gate)


def _out_ln_kernel(*refs, tile_off, gate_idx, post):
    if post is None:
        y_ref, w_ref, x_ref, modl_ref, modc_ref, g_ref, b_ref, o_ref = refs
        y = y_ref[0]
    elif post == "rwkv":
        y_ref, r_ref, v_ref, kd0_ref, kd1_ref, gg_ref, pr_ref = refs[:7]
        w_ref, x_ref, modl_ref, modc_ref, g_ref, b_ref, o_ref = refs[7:]
        y = _rwkv_post(y_ref[0], r_ref[0], v_ref[0], kd0_ref[0], kd1_ref[0], gg_ref[0], pr_ref[...])
    else:
        y_ref, gate_ref, ng_ref, w_ref, x_ref, modl_ref, modc_ref, g_ref, b_ref, o_ref = refs
        y = _head_post(y_ref[0], gate_ref[0], ng_ref[...], post)
    is_ctx = (pl.program_id(1) + tile_off) * TOKEN_TILE < CTX_LEN
    gate = _mod_rows(modl_ref, modc_ref, is_ctx, gate_idx)
    f = jnp.dot(y.astype(BF16), w_ref[...], preferred_element_type=F32)
    r = ALPHA * x_ref[0] + gate * f
    o_ref[0] = _layer_norm_rows(r, g_ref[...], b_ref[...])


def out_proj_ln(y, w_bf16, x, mods, ln_g, ln_b, *, gate_idx, tile_off=0, post=None, post_args=None):
    b, t_y, k = y.shape
    modl, modc = mods
    tok = pl.BlockSpec((1, TOKEN_TILE, k), lambda bi, ti: (bi, ti, 0))
    in_specs, args = [tok], [y]
    if post == "rwkv":
        *tiles, params = post_args
        in_specs += [tok] * len(tiles) + [pl.BlockSpec(params.shape, lambda bi, ti: (0, 0))]
        args += [*tiles, params]
    elif post is not None:
        z, gate_block, norm_g = post_args
        in_specs += [pl.BlockSpec((1, TOKEN_TILE, k), lambda bi, ti: (bi, ti, gate_block)),
                     pl.BlockSpec((1, k), lambda bi, ti: (0, 0))]
        args += [z, norm_g.reshape(1, k)]
    in_specs += [pl.BlockSpec((k, D_MODEL), lambda bi, ti: (0, 0)),
                 pl.BlockSpec((1, TOKEN_TILE, D_MODEL), lambda bi, ti: (bi, ti + tile_off, 0)),
                 pl.BlockSpec((1, 1, 6 * D_MODEL), lambda bi, ti: (bi, 0, 0)),
                 pl.BlockSpec((1, 1, 6 * D_MODEL), lambda bi, ti: (0, 0, 0)),
                 pl.BlockSpec((1, D_MODEL), lambda bi, ti: (0, 0)),
                 pl.BlockSpec((1, D_MODEL), lambda bi, ti: (0, 0))]
    args += [w_bf16, x, modl, modc, ln_g.reshape(1, -1), ln_b.reshape(1, -1)]
    return pl.pallas_call(
        functools.partial(_out_ln_kernel, tile_off=tile_off, gate_idx=gate_idx, post=post),
        grid=(b, t_y // TOKEN_TILE),
        in_specs=in_specs,
        out_specs=pl.BlockSpec((1, TOKEN_TILE, D_MODEL), lambda bi, ti: (bi, ti, 0)),
        out_shape=jax.ShapeDtypeStruct((b, t_y, D_MODEL), F32),
        compiler_params=_cparams(("parallel", "parallel")),
        name="out_proj_ln",
    )(*args)


def _seq_prep_kernel(z_ref, cos_ref, sin_ref, cw_ref, o_ref, *, conv, norm_blocks, rope_blocks, scale_blocks, scale):
    x = z_ref[0]
    t = x.shape[0]
    j = pl.program_id(1)
    if conv:
        row = _iota2(t, 1, 0)
        seg_first = (row == 0) | (row == CTX_LEN)
        seg_last = (row == CTX_LEN - 1) | (row == t - 1)
        x_prev = jnp.where(seg_first, 0.0, pltpu.roll(x, 1, axis=0))
        x_next = jnp.where(seg_last, 0.0, pltpu.roll(x, t - 1, axis=0))
        w = cw_ref[...]
        x = _silu(x_prev * w[0:1] + x * w[1:2] + x_next * w[2:3])
    if norm_blocks:
        normed = x * lax.rsqrt(jnp.sum(x * x, axis=1, keepdims=True) + 1e-6)
        x = jnp.where(j < norm_blocks, normed, x)
    lane = _iota2(1, x.shape[1], 1)
    partner = jnp.where((lane % 64) < 32, pltpu.roll(x, 96, axis=1), pltpu.roll(x, 32, axis=1))
    roped = x * cos_ref[...] + partner * sin_ref[...]
    x = jnp.where(j < rope_blocks, roped, x)
    o_ref[0] = jnp.where(j < scale_blocks, x * scale, x)


def _rope_lane_tables(t):
    pos = jnp.arange(t - CTX_LEN)
    inv_freq = ROPE_BASE ** (-jnp.arange(32, dtype=F32) / 32)
    ang_r = (pos // GRID_W).astype(F32)[:, None] * inv_freq[None, :]
    ang_c = (pos % GRID_W).astype(F32)[:, None] * inv_freq[None, :]
    cos = jnp.concatenate([jnp.cos(ang_r), jnp.cos(ang_r), jnp.cos(ang_c), jnp.cos(ang_c)], axis=1)
    sin = jnp.concatenate([-jnp.sin(ang_r), jnp.sin(ang_r), -jnp.sin(ang_c), jnp.sin(ang_c)], axis=1)
    pad = lambda a, v: jnp.concatenate([jnp.full((CTX_LEN, 128), v, F32), a], axis=0)
    return pad(cos, 1.0), pad(sin, 0.0)


def seq_prep(z, n_blocks, *, conv_w=None, norm_blocks, rope_blocks, scale_blocks, scale):
    b, t, _ = z.shape
    cos, sin = _rope_lane_tables(t)
    conv = conv_w is not None
    cw = conv_w if conv else jnp.zeros((3, n_blocks * 128), F32)
    return pl.pallas_call(
        functools.partial(_seq_prep_kernel, conv=conv, norm_blocks=norm_blocks, rope_blocks=rope_blocks,
                          scale_blocks=scale_blocks, scale=scale),
        grid=(b, n_blocks),
        in_specs=[pl.BlockSpec((1, t, 128), lambda bi, ji: (bi, 0, ji)),
                  pl.BlockSpec((t, 128), lambda bi, ji: (0, 0)),
                  pl.BlockSpec((t, 128), lambda bi, ji: (0, 0)),
                  pl.BlockSpec((3, 128), lambda bi, ji: (0, ji))],
        out_specs=pl.BlockSpec((1, t, 128), lambda bi, ji: (bi, 0, ji)),
        out_shape=jax.ShapeDtypeStruct((b, t, n_blocks * 128), F32),
        compiler_params=_cparams(("parallel", "parallel")),
        name="seq_prep",
    )(z, cos, sin, cw)


def _iota2(n, m, axis):
    return lax.broadcasted_iota(jnp.int32, (n, m), axis)


def _row_to_col(row, eye):
    return jnp.sum(jnp.where(eye, row, 0.0), axis=1, keepdims=True)


def _tri_solve_steps(n_mat, rhs_list, eye_f, blockdiag):
    nd = jnp.where(blockdiag, n_mat, 0.0)
    ne = n_mat - nd
    p = eye_f + nd
    n2 = _mm(nd, nd)
    yield
    p = p + _mm(p, n2)
    n4 = _mm(n2, n2)
    yield
    p = p + _mm(p, n4)
    n8 = _mm(n4, n4)
    yield
    dinv = p + _mm(p, n8)
    yield
    m = _mm(dinv, ne)
    xs = [_mm(dinv, r) for r in rhs_list]
    yield
    m2 = _mm(m, m)
    xs = [x + _mm(m, x) for x in xs]
    yield
    xs = [x + _mm(m2, x) for x in xs]
    yield
    return xs


def _run_interleaved(gens):
    gens = list(gens)
    while gens:
        alive = []
        for g in gens:
            try:
                next(g)
                alive.append(g)
            except StopIteration:
                pass
        gens = alive


def _split2(x):
    hi = x.astype(BF16)
    return hi, (x - hi.astype(F32)).astype(BF16)


def _cumsum_rows(x, cum):
    hi, lo = _split2(x)
    cb = cum.astype(BF16)
    return jnp.dot(hi, cb, preferred_element_type=F32) + jnp.dot(lo, cb, preferred_element_type=F32)


def _cumsum_cols(cum, x):
    hi, lo = _split2(x)
    cb = cum.astype(BF16)
    return jnp.dot(cb, hi, preferred_element_type=F32) + jnp.dot(cb, lo, preferred_element_type=F32)


def _chunk_order(direction, n_ctx, n_tot):
    def order(j):
        if direction == 0:
            return j
        return jnp.where(j < n_ctx, n_ctx - 1 - j, n_tot - 1 - (j - n_ctx))
    return order


def _masks(direction):
    ii = _iota2(CHUNK, CHUNK, 0)
    jj = _iota2(CHUNK, CHUNK, 1)
    if direction == 0:
        incl, strict = jj <= ii, jj < ii
    else:
        incl, strict = jj >= ii, jj > ii
    return ii, jj, incl, strict


def _gdn_kernel(q_ref, k_ref, v_ref, la_ref, be_ref, o_ref, g_s, lhs_s, add_s, s_s, ob_s, *, n_ctx, n_tot):
    ii, jj, _, _ = _masks(0)
    eye = ii == jj
    eye_f = eye.astype(F32)
    blockdiag = (ii // 16) == (jj // 16)
    dirs = []
    for d in (0, 1):
        _, _, incl, strict = _masks(d)
        cum = (ii <= jj).astype(F32) if d == 0 else (ii >= jj).astype(F32)
        g_s[d] = _cumsum_rows(la_ref[d, 0, 0], cum)
        dirs.append((incl, strict, CHUNK - 1 if d == 0 else 0))

    def prep(j, carry):
        loaded = []
        for uu in range(GDN_UNROLL):
            n = j * GDN_UNROLL + uu
            rows = pl.ds(pl.multiple_of(n * CHUNK, CHUNK), CHUNK)
            loaded.append((n, q_ref[0, rows, :], k_ref[0, rows, :], v_ref[0, rows, :],
                           [g_s[d, pl.ds(n, 1), :] for d in (0, 1)],
                           [be_ref[d, 0, 0, pl.ds(n, 1), :] for d in (0, 1)]))
        stores = []
        grams = [(_mm_nt(kc, kc), _mm_nt(qc, kc)) for _, qc, kc, _, _, _ in loaded]

        def chain(n, qc, kc, vc, g_row, be_row, kk, qk, d, incl, strict, last):
            g_col = _row_to_col(g_row, eye)
            be_col = _row_to_col(be_row, eye)
            g_last = g_row[:, last:last + 1]
            gamma = jnp.exp(jnp.where(incl, g_col - g_row, NEG))
            n_mat = jnp.where(strict, -(be_col * kk * gamma), 0.0)
            eg = jnp.exp(g_col)
            rhs = jnp.concatenate([be_col * vc, (be_col * eg) * kc], axis=1)
            (uw,) = yield from _tri_solve_steps(n_mat, [rhs], eye_f, blockdiag)
            p_mat = jnp.where(incl, qk * gamma, 0.0)
            p_uw = _mm(p_mat, uw)
            k_dec = kc * jnp.exp(g_last - g_col)
            k_uw = _mm_tn(k_dec, uw)
            q_t = qc * eg - p_uw[:, GDN_DV:]
            stores.append((d, n, jnp.concatenate([q_t, -k_uw[:, GDN_DV:]], axis=0).astype(BF16),
                           jnp.concatenate([p_uw[:, :GDN_DV], k_uw[:, :GDN_DV]], axis=0)))

        _run_interleaved(
            chain(n, qc, kc, vc, g_rows[d], be_rows[d], kk, qk, d, *dirs[d])
            for (n, qc, kc, vc, g_rows, be_rows), (kk, qk) in zip(loaded, grams) for d in (0, 1))
        for d, n, lhs, add in stores:
            lhs_s[d, n] = lhs
            add_s[d, n] = add
        return carry

    lax.fori_loop(0, n_tot // GDN_UNROLL, prep, 0)

    s_s[...] = jnp.zeros_like(s_s)
    orders = [_chunk_order(d, n_ctx, n_tot) for d in (0, 1)]
    o_refs = (o_ref.at[0], ob_s)

    def step(j, carry):
        loaded = []
        for d, (_, _, last) in enumerate(dirs):
            n = orders[d](j)
            loaded.append((n, s_s[d], lhs_s[d, n], add_s[d, n], g_s[d, pl.ds(n, 1), last:last + 1]))
        results = []
        for n, s, lhs, add, g_last in loaded:
            z = jnp.dot(lhs, s.astype(BF16), preferred_element_type=F32) + add
            results.append((n, jnp.exp(g_last) * s + z[CHUNK:], z[:CHUNK]))
        for d, (n, s_new, o) in enumerate(results):
            s_s[d] = s_new
            o_refs[d][pl.ds(pl.multiple_of(n * CHUNK, CHUNK), CHUNK), :] = o
        return carry

    lax.fori_loop(0, n_tot, step, 0)
    o_ref[0] += ob_s[...]


def gdn_scan(qkv, log_alpha, beta):
    b, t, _ = qkv.shape
    n_tot = t // CHUNK
    ncp = log_alpha.shape[3]
    blk = pl.BlockSpec((1, t, GDN_DK), lambda bi, hi: (bi, 0, hi))
    k_blk = pl.BlockSpec((1, t, GDN_DK), lambda bi, hi: (bi, 0, GDN_HEADS + hi))
    v_blk = pl.BlockSpec((1, t, GDN_DV), lambda bi, hi: (bi, 0, 2 * GDN_HEADS + hi))
    gate_blk = pl.BlockSpec((2, 1, 1, ncp, CHUNK), lambda bi, hi: (0, bi, hi, 0, 0))
    return pl.pallas_call(
        functools.partial(_gdn_kernel, n_ctx=CTX_LEN // CHUNK, n_tot=n_tot),
        grid=(b, GDN_HEADS),
        in_specs=[blk, k_blk, v_blk, gate_blk, gate_blk],
        out_specs=blk,
        out_shape=jax.ShapeDtypeStruct((b, t, GDN_HEADS * GDN_DV), F32),
        scratch_shapes=[pltpu.VMEM((2, ncp, CHUNK), F32),
                        pltpu.VMEM((2, n_tot, CHUNK + GDN_DK, GDN_DV), BF16),
                        pltpu.VMEM((2, n_tot, CHUNK + GDN_DK, GDN_DV), F32),
                        pltpu.VMEM((2, GDN_DK, GDN_DV), F32),
                        pltpu.VMEM((t, GDN_DV), F32)],
        compiler_params=_cparams(("parallel", "parallel")),
        name="gdn_scan",
    )(qkv, qkv, qkv, log_alpha, beta)


def _mlstm_kernel(q_ref, k_ref, v_ref, ip_ref, lf_ref, o_ref, b_s, c_s, n_s, m_s, ob_s, *, n_ctx, n_tot):
    ii, jj, _, _ = _masks(0)
    eye = ii == jj
    hps = MLSTM_HEADS_PER_STEP
    dirs = []
    for d in (0, 1):
        _, _, incl, _ = _masks(d)
        cum = (ii <= jj).astype(F32) if d == 0 else (ii >= jj).astype(F32)
        for hh in range(hps):
            b_s[d, hh] = _cumsum_rows(lf_ref[d, 0, hh], cum)
        dirs.append((incl, CHUNK - 1 if d == 0 else 0))
    c_s[...] = jnp.zeros_like(c_s)
    n_s[...] = jnp.zeros_like(n_s)
    m_s[...] = jnp.zeros_like(m_s)
    orders = [_chunk_order(d, n_ctx, n_tot) for d in (0, 1)]
    o_refs = (o_ref.at[0], ob_s)

    def step(j, carry):
        loaded = {}
        for d in (0, 1):
            n = orders[d](j)
            rows = pl.ds(pl.multiple_of(n * CHUNK, CHUNK), CHUNK)
            for hh in range(hps):
                qsl = slice(hh * MLSTM_DQK, (hh + 1) * MLSTM_DQK)
                vsl = slice(hh * MLSTM_DV, (hh + 1) * MLSTM_DV)
                loaded[(d, hh)] = (q_ref[0, rows, qsl], k_ref[0, rows, qsl], v_ref[0, rows, vsl],
                                   b_s[d, hh, pl.ds(n, 1), :], ip_ref[d, 0, hh, pl.ds(n, 1), :],
                                   c_s[d, hh], n_s[d, hh], m_s[d, hh], rows, vsl)
        results = {}

        def chain(key, qc, kc, vc, b_row, ip_row, c_st, n_st, m_st, incl, last):
            b_col = _row_to_col(b_row, eye)
            b_last = b_row[:, last:last + 1]
            qk = _mm_nt(qc, kc)
            qc_st = _mm(qc, c_st)
            qn = jnp.sum(qc * n_st, axis=1, keepdims=True)
            log_end = b_last - b_row + ip_row
            m_end = jnp.max(log_end, axis=1, keepdims=True)
            yield
            log_d = jnp.where(incl, b_col - b_row + ip_row, NEG)
            m_intra = jnp.max(log_d, axis=1, keepdims=True)
            m_new = jnp.maximum(b_last + m_st, m_end)
            decay = jnp.exp(b_last + m_st - m_new)
            kw_col = _row_to_col(jnp.exp(log_end - m_new), eye)
            yield
            m_row = jnp.maximum(b_col + m_st, m_intra)
            w_state = jnp.exp(b_col + m_st - m_row)
            k_w = kc * kw_col
            c_new = decay * c_st + _mm_tn(k_w, vc)
            n_new = decay * n_st + jnp.sum(k_w, axis=0, keepdims=True)
            w_intra = jnp.exp(log_d - m_row) * qk
            num = w_state * qc_st + _mm(w_intra, vc)
            den = w_state * qn + jnp.sum(w_intra, axis=1, keepdims=True)
            yield
            results[key] = (num / jnp.maximum(jnp.abs(den), jnp.exp(-m_row)), c_new, n_new, m_new)

        _run_interleaved(chain(key, *vals[:8], *dirs[key[0]]) for key, vals in loaded.items())
        for (d, hh), (h, c_new, n_new, m_new) in results.items():
            rows, vsl = loaded[(d, hh)][8:]
            c_s[d, hh] = c_new
            n_s[d, hh] = n_new
            m_s[d, hh] = m_new
            o_refs[d][rows, vsl] = h
        return carry

    lax.fori_loop(0, n_tot, step, 0)
    o_ref[0] += ob_s[...]


def mlstm_scan(qk, z, i_pre, log_f):
    b, t, _ = qk.shape
    n_tot = t // CHUNK
    ncp = i_pre.shape[3]
    hps = MLSTM_HEADS_PER_STEP
    n_steps = MLSTM_HEADS // hps
    v_off = 2 * MLSTM_HEADS * MLSTM_DQK // (hps * MLSTM_DV)
    q_blk = pl.BlockSpec((1, t, hps * MLSTM_DQK), lambda bi, hi: (bi, 0, hi))
    k_blk = pl.BlockSpec((1, t, hps * MLSTM_DQK), lambda bi, hi: (bi, 0, n_steps + hi))
    v_blk = pl.BlockSpec((1, t, hps * MLSTM_DV), lambda bi, hi: (bi, 0, v_off + hi))
    o_blk = pl.BlockSpec((1, t, hps * MLSTM_DV), lambda bi, hi: (bi, 0, hi))
    gate_blk = pl.BlockSpec((2, 1, hps, ncp, CHUNK), lambda bi, hi: (0, bi, hi, 0, 0))
    return pl.pallas_call(
        functools.partial(_mlstm_kernel, n_ctx=CTX_LEN // CHUNK, n_tot=n_tot),
        grid=(b, n_steps),
        in_specs=[q_blk, k_blk, v_blk, gate_blk, gate_blk],
        out_specs=o_blk,
        out_shape=jax.ShapeDtypeStruct((b, t, MLSTM_HEADS * MLSTM_DV), F32),
        scratch_shapes=[pltpu.VMEM((2, hps, ncp, CHUNK), F32),
                        pltpu.VMEM((2, hps, MLSTM_DQK, MLSTM_DV), F32),
                        pltpu.VMEM((2, hps, 1, MLSTM_DQK), F32),
                        pltpu.VMEM((2, hps, 1, 1), F32),
                        pltpu.VMEM((t, hps * MLSTM_DV), F32)],
        compiler_params=_cparams(("parallel", "parallel")),
        name="mlstm_scan",
    )(qk, qk, z, i_pre, log_f)


def _rwkv_kernel(r_ref, v_ref, kk_ref, lw0_ref, lw1_ref, kd0_ref, kd1_ref, bv0_ref, bv1_ref, o_ref,
                 lhs_s, add_s, gl_s, s_s, ob_s, *, n_ctx, n_tot):
    dir_refs = ((lw0_ref, kd0_ref, bv0_ref), (lw1_ref, kd1_ref, bv1_ref))
    ii, jj, _, _ = _masks(0)
    eye = ii == jj
    eye_f = eye.astype(F32)
    blockdiag = (ii // 16) == (jj // 16)
    hd = RWKV_HEAD
    dirs = []
    for d in (0, 1):
        _, _, incl, strict = _masks(d)
        dirs.append((incl, strict, CHUNK - 1 if d == 0 else 0))
    zero = jnp.zeros((hd, hd), F32)

    def prep(j, carry):
        loaded = []
        for uu in range(RWKV_UNROLL):
            n = j * RWKV_UNROLL + uu
            rows = pl.ds(pl.multiple_of(n * CHUNK, CHUNK), CHUNK)
            loaded.append((n, r_ref[0, rows, :], v_ref[0, rows, :], kk_ref[0, rows, :],
                           [tuple(ref[0, rows, :] for ref in dir_refs[d]) for d in (0, 1)]))
        parts = {}

        def chain(key, r, v, kk, lw, kd, bv, gcs, e_end, incl, strict):
            e_neg = jnp.exp(-gcs)
            a_h = -kk * jnp.exp(gcs - lw)
            r_h = r * jnp.exp(gcs)
            b_h = bv * e_neg
            k_h = kd * e_neg
            a_ab = jnp.where(strict, _mm_nt(a_h, b_h), 0.0)
            a_ak = jnp.where(strict, _mm_nt(a_h, k_h), 0.0)
            a_rb = jnp.where(incl, _mm_nt(r_h, b_h), 0.0)
            a_rk = jnp.where(incl, _mm_nt(r_h, k_h), 0.0)
            yield
            av = _mm(a_ak, v)
            ta, tav = yield from _tri_solve_steps(a_ab, [a_h, av], eye_f, blockdiag)
            b_g = bv * e_end
            parts[key] = (r_h + _mm(a_rb, ta), _mm_tn(b_g, ta),
                          _mm(a_rb, tav) + _mm(a_rk, v), _mm_tn(b_g, tav) + _mm_tn(kd * e_end, v))

        gens, gls = [], {}
        for uu, (_, r2, v2, kk2, per_dir) in enumerate(loaded):
            for d, (incl, strict, last) in enumerate(dirs):
                lw2, kd2, bv2 = per_dir[d]
                gcs2 = _cumsum_cols(incl.astype(F32), lw2)
                gl_row2 = gcs2[last:last + 1, :]
                e_end2 = jnp.exp(gl_row2 - gcs2)
                gls[(uu, d)] = jnp.exp(gl_row2)
                for hh in range(2):
                    sl = slice(hh * hd, (hh + 1) * hd)
                    gens.append(chain((uu, d, hh), *(a[:, sl] for a in (r2, v2, kk2, lw2, kd2, bv2, gcs2, e_end2)),
                                      incl, strict))
        _run_interleaved(gens)
        for uu, (n, _, _, _, _) in enumerate(loaded):
            for d in (0, 1):
                (rt0, mx0, yc0, kv0), (rt1, mx1, yc1, kv1) = parts[(uu, d, 0)], parts[(uu, d, 1)]
                stack = lambda t0, t1, b0, b1: jnp.concatenate(
                    [jnp.concatenate([t0, t1], axis=1), jnp.concatenate([b0, zero], axis=1),
                     jnp.concatenate([zero, b1], axis=1)], axis=0)
                lhs_s[d, n] = stack(rt0, rt1, mx0, mx1).astype(BF16)
                add_s[d, n] = stack(yc0, yc1, kv0, kv1)
                gl_s[d, pl.ds(n, 1), :] = gls[(uu, d)]
        return carry

    lax.fori_loop(0, n_tot // RWKV_UNROLL, prep, 0)

    s_s[...] = jnp.zeros_like(s_s)
    orders = [_chunk_order(d, n_ctx, n_tot) for d in (0, 1)]
    eye2 = _iota2(2 * hd, 2 * hd, 0) == _iota2(2 * hd, 2 * hd, 1)
    o_refs = (o_ref.at[0], ob_s)

    def step(j, carry):
        loaded = []
        for d in (0, 1):
            n = orders[d](j)
            loaded.append((n, s_s[d], lhs_s[d, n], add_s[d, n], gl_s[d, pl.ds(n, 1), :]))
        results = []
        for n, s, lhs, add, gl in loaded:
            z = jnp.dot(lhs, s.astype(BF16), preferred_element_type=F32) + add
            results.append((n, _row_to_col(gl, eye2) * s + z[CHUNK:], z[:CHUNK]))
        for d, (n, s_new, y) in enumerate(results):
            s_s[d] = s_new
            o_refs[d][pl.ds(pl.multiple_of(n * CHUNK, CHUNK), CHUNK), :] = y
        return carry

    lax.fori_loop(0, n_tot, step, 0)
    o_ref[0] += ob_s[...]


def rwkv_scan(r, v, kk, log_w, k_dir, b_dir):
    b, t, _ = r.shape
    n_tot = t // CHUNK
    blk = pl.BlockSpec((1, t, 128), lambda bi, hi: (bi, 0, hi))
    return pl.pallas_call(
        functools.partial(_rwkv_kernel, n_ctx=CTX_LEN // CHUNK, n_tot=n_tot),
        grid=(b, RWKV_HEADS // 2),
        in_specs=[blk] * 9,
        out_specs=blk,
        out_shape=jax.ShapeDtypeStruct((b, t, D_MODEL), F32),
        scratch_shapes=[pltpu.VMEM((2, n_tot, CHUNK + 2 * RWKV_HEAD, 2 * RWKV_HEAD), BF16),
                        pltpu.VMEM((2, n_tot, CHUNK + 2 * RWKV_HEAD, 2 * RWKV_HEAD), F32),
                        pltpu.VMEM((2, 8 * ((n_tot + 7) // 8), 2 * RWKV_HEAD), F32),
                        pltpu.VMEM((2, 2 * RWKV_HEAD, 2 * RWKV_HEAD), F32),
                        pltpu.VMEM((t, 2 * RWKV_HEAD), F32)],
        compiler_params=_cparams(("parallel", "parallel")),
        name="rwkv_scan",
    )(r, v, kk, *log_w, *k_dir, *b_dir)


def _na_kernel(q_ref, k_ref, v_ref, bias_ref, o_ref, *, rows):
    scale = NA_DH ** -0.5
    slab = NA_WIN_ROWS * GRID_W
    lane = _iota2(1, 2 * NA_DH, 1)
    head_masks = (lane < NA_DH, lane >= NA_DH)
    kc2 = k_ref[0, 0:CTX_LEN, :].astype(BF16)
    vc2 = v_ref[0, 0:CTX_LEN, :].astype(BF16)

    def body(j, carry):
        loaded = []
        for uu in range(NA_ROWS_PER_STEP):
            r = j * NA_ROWS_PER_STEP + uu
            r0 = jnp.clip(r - NA_WIN_ROWS // 2, 0, rows - NA_WIN_ROWS)
            dr0 = r0 - r + NA_WIN_ROWS - 1
            krows = pl.ds(pl.multiple_of(CTX_LEN + r0 * GRID_W, GRID_W), slab)
            loaded.append((r, q_ref[0, pl.ds(pl.multiple_of(CTX_LEN + r * GRID_W, GRID_W), GRID_W), :],
                           k_ref[0, krows, :].astype(BF16), v_ref[0, krows, :].astype(BF16),
                           [bias_ref[hh, pl.ds(dr0, 1)][0] for hh in range(2)]))
        outs = {}

        def chain(key, q2, ks2, vs2, bias, mask):
            qh = jnp.where(mask, q2, 0.0)
            s_lat = _mm_nt(qh, ks2) * scale + bias
            s_ctx = _mm_nt(qh, kc2) * scale
            yield
            m = jnp.maximum(jnp.max(s_lat, axis=1, keepdims=True), jnp.max(s_ctx, axis=1, keepdims=True))
            yield
            p_lat = jnp.exp(s_lat - m)
            p_ctx = jnp.exp(s_ctx - m)
            l = jnp.sum(p_lat, axis=1, keepdims=True) + jnp.sum(p_ctx, axis=1, keepdims=True)
            outs[key] = (_mm(p_lat, vs2) + _mm(p_ctx, vc2)) / l
            yield

        _run_interleaved(chain((uu, hh), q2, ks2, vs2, biases[hh], head_masks[hh])
                         for uu, (_, q2, ks2, vs2, biases) in enumerate(loaded) for hh in range(2))
        for uu, (r, _, _, _, _) in enumerate(loaded):
            o_ref[0, pl.ds(pl.multiple_of(r * GRID_W, GRID_W), GRID_W), :] = jnp.where(
                head_masks[0], outs[(uu, 0)], outs[(uu, 1)])
        return carry

    lax.fori_loop(0, rows // NA_ROWS_PER_STEP, body, 0)


def _na_bias_table(rpb):
    cols = np.arange(GRID_W)
    win_c0 = np.clip(cols - NA_WIN_COLS // 2, 0, GRID_W - NA_WIN_COLS)
    kc = np.arange(GRID_W)
    in_win = (kc[None, :] >= win_c0[:, None]) & (kc[None, :] < win_c0[:, None] + NA_WIN_COLS)
    dc = np.clip(kc[None, :] - cols[:, None] + NA_WIN_COLS - 1, 0, 2 * NA_WIN_COLS - 2)
    dr = np.arange(NA_WIN_ROWS)[:, None] + np.arange(NA_WIN_ROWS)[None, :]
    tab = rpb.astype(F32)[:, dr][:, :, :, dc]
    tab = jnp.where(in_win[None, None, None], tab, NEG)
    tab = tab.transpose(0, 1, 3, 2, 4)
    return tab.reshape(NA_HEADS, NA_WIN_ROWS, GRID_W, NA_WIN_ROWS * GRID_W)


def na_attention(z, bias_tab):
    b, t, _ = z.shape
    t_lat = t - CTX_LEN
    n_pairs = NA_HEADS // 2
    return pl.pallas_call(
        functools.partial(_na_kernel, rows=t_lat // GRID_W),
        grid=(n_pairs, b),
        in_specs=[pl.BlockSpec((1, t, 128), lambda hi, bi: (bi, 0, hi)),
                  pl.BlockSpec((1, t, 128), lambda hi, bi: (bi, 0, n_pairs + hi)),
                  pl.BlockSpec((1, t, 128), lambda hi, bi: (bi, 0, 2 * n_pairs + hi)),
                  pl.BlockSpec((2, NA_WIN_ROWS, GRID_W, NA_WIN_ROWS * GRID_W), lambda hi, bi: (hi, 0, 0, 0))],
        out_specs=pl.BlockSpec((1, t_lat, 128), lambda hi, bi: (bi, 0, hi)),
        out_shape=jax.ShapeDtypeStruct((b, t_lat, D_MODEL), F32),
        compiler_params=_cparams(("parallel", "parallel")),
        name="na_attention",
    )(z, z, z, bias_tab)


def _router_kernel(x_ref, modl_ref, modc_ref, rw_ref, rb_ref, hb_ref, idx_ref, wt_ref, *, tile_off):
    is_ctx = (pl.program_id(1) + tile_off) * TOKEN_TILE < CTX_LEN
    sh = _mod_rows(modl_ref, modc_ref, is_ctx, 3)
    sc = _mod_rows(modl_ref, modc_ref, is_ctx, 4)
    h = x_ref[0] * (1.0 + sc) + sh
    hb_ref[0] = h.astype(BF16)
    logits = lax.dot_general(rw_ref[...], h, (((1,), (1,)), ((), ())), preferred_element_type=F32,
                             precision=_HI)
    mx = jnp.max(logits, axis=0, keepdims=True)
    ex = jnp.exp(logits - mx)
    probs = ex / jnp.sum(ex, axis=0, keepdims=True)
    sel = probs + rb_ref[...]
    tm = sel.shape[1]
    e_id = _iota2(N_EXPERTS, tm, 0)
    per_group = N_EXPERTS // N_GROUPS
    g_id = e_id // per_group

    def top1(vals):
        m1 = jnp.max(vals, axis=0, keepdims=True)
        i1 = jnp.min(jnp.where(vals == m1, e_id, N_EXPERTS), axis=0, keepdims=True)
        return m1, i1

    best_score, best = None, None
    for g in range(N_GROUPS):
        vals = jnp.where(g_id == g, sel, NEG)
        m1, i1 = top1(vals)
        m2, _ = top1(jnp.where(e_id == i1, NEG, vals))
        score = m1 + m2
        if g == 0:
            best_score, best = score, jnp.zeros_like(i1)
        else:
            better = score > best_score
            best = jnp.where(better, g, best)
            best_score = jnp.where(better, score, best_score)
    vals = jnp.where(g_id == best, sel, NEG)
    _, i1 = top1(vals)
    _, i2 = top1(jnp.where(e_id == i1, NEG, vals))
    p1 = jnp.sum(jnp.where(e_id == i1, probs, 0.0), axis=0, keepdims=True)
    p2 = jnp.sum(jnp.where(e_id == i2, probs, 0.0), axis=0, keepdims=True)
    tot = p1 + p2
    idx_ref[0] = jnp.where(_iota2(2, tm, 0) == 0, i1, i2)
    eye = _iota2(tm, tm, 0) == _iota2(tm, tm, 1)
    wt_ref[0] = jnp.where(_iota2(tm, 2, 1) == 0, _row_to_col(p1 / tot, eye), _row_to_col(p2 / tot, eye))


def moe_route(x, mods, router_w, router_b, *, tile_off):
    b, t_out, _ = x.shape
    modl, modc = mods
    return pl.pallas_call(
        functools.partial(_router_kernel, tile_off=tile_off),
        grid=(b, t_out // TOKEN_TILE),
        in_specs=[pl.BlockSpec((1, TOKEN_TILE, D_MODEL), lambda bi, ti: (bi, ti, 0)),
                  pl.BlockSpec((1, 1, 6 * D_MODEL), lambda bi, ti: (bi, 0, 0)),
                  pl.BlockSpec((1, 1, 6 * D_MODEL), lambda bi, ti: (0, 0, 0)),
                  pl.BlockSpec((N_EXPERTS, D_MODEL), lambda bi, ti: (0, 0)),
                  pl.BlockSpec((N_EXPERTS, 1), lambda bi, ti: (0, 0))],
        out_specs=[pl.BlockSpec((1, TOKEN_TILE, D_MODEL), lambda bi, ti: (bi, ti, 0)),
                   pl.BlockSpec((1, 2, TOKEN_TILE), lambda bi, ti: (bi, 0, ti)),
                   pl.BlockSpec((1, TOKEN_TILE, 2), lambda bi, ti: (bi, ti, 0))],
        out_shape=[jax.ShapeDtypeStruct((b, t_out, D_MODEL), BF16),
                   jax.ShapeDtypeStruct((b, 2, t_out), jnp.int32),
                   jax.ShapeDtypeStruct((b, t_out, 2), F32)],
        compiler_params=_cparams(("parallel", "parallel")),
        name="moe_route",
    )(x, modl, modc, router_w.T, router_b.reshape(-1, 1))


def _ffn_kernel(te_ref, nt_ref, x_ref, w1_ref, w3_ref, w2_ref, o_ref):
    @pl.when(pl.program_id(0) < nt_ref[0])
    def _():
        xb = x_ref[...]
        h1 = jnp.dot(xb, w1_ref[0, 0].astype(BF16), preferred_element_type=F32)
        h3 = jnp.dot(xb, w3_ref[0, 0].astype(BF16), preferred_element_type=F32)
        hid = (_silu(h1) * h3).astype(BF16)
        o_ref[...] = jnp.dot(hid, w2_ref[0, 0].astype(BF16), preferred_element_type=F32).astype(o_ref.dtype)

    @pl.when(pl.program_id(0) >= nt_ref[0])
    def _():
        o_ref[...] = jnp.zeros_like(o_ref)


def expert_ffn(xs, tile_expert, n_tiles_used, w1, w3, w2, layer):
    p = xs.shape[0]
    grid_spec = pltpu.PrefetchScalarGridSpec(
        num_scalar_prefetch=2,
        grid=(p // FFN_TILE,),
        in_specs=[pl.BlockSpec((FFN_TILE, D_MODEL), lambda i, te, nt: (i, 0)),
                  pl.BlockSpec((1, 1, D_MODEL, D_EXPERT), lambda i, te, nt: (layer, te[i], 0, 0)),
                  pl.BlockSpec((1, 1, D_MODEL, D_EXPERT), lambda i, te, nt: (layer, te[i], 0, 0)),
                  pl.BlockSpec((1, 1, D_EXPERT, D_MODEL), lambda i, te, nt: (layer, te[i], 0, 0))],
        out_specs=pl.BlockSpec((FFN_TILE, D_MODEL), lambda i, te, nt: (i, 0)),
    )
    return pl.pallas_call(
        _ffn_kernel,
        grid_spec=grid_spec,
        out_shape=jax.ShapeDtypeStruct((p, D_MODEL), BF16),
        compiler_params=_cparams(("arbitrary",)),
        name="expert_ffn",
    )(tile_expert, n_tiles_used, xs, w1, w3, w2)


def _combine_ln_kernel(x_ref, y0_ref, y1_ref, wt_ref, modl_ref, modc_ref, g_ref, b_ref, o_ref, *, tile_off):
    is_ctx = (pl.program_id(1) + tile_off) * TOKEN_TILE < CTX_LEN
    gate = _mod_rows(modl_ref, modc_ref, is_ctx, 5)
    wt = wt_ref[0]
    f = wt[:, 0:1] * y0_ref[0, 0].astype(F32) + wt[:, 1:2] * y1_ref[0, 0].astype(F32)
    r = ALPHA * x_ref[0] + gate * f
    o_ref[0] = _layer_norm_rows(r, g_ref[...], b_ref[...])


def combine_ln(x, y01, wt, mods, ln_g, ln_b, *, tile_off):
    _, b, t_out, _ = y01.shape
    modl, modc = mods
    tok = lambda bi, ti: (bi, ti, 0)
    return pl.pallas_call(
        functools.partial(_combine_ln_kernel, tile_off=tile_off),
        grid=(b, t_out // TOKEN_TILE),
        in_specs=[pl.BlockSpec((1, TOKEN_TILE, D_MODEL), tok),
                  pl.BlockSpec((1, 1, TOKEN_TILE, D_MODEL), lambda bi, ti: (0, bi, ti, 0)),
                  pl.BlockSpec((1, 1, TOKEN_TILE, D_MODEL), lambda bi, ti: (1, bi, ti, 0)),
                  pl.BlockSpec((1, TOKEN_TILE, 2), tok),
                  pl.BlockSpec((1, 1, 6 * D_MODEL), lambda bi, ti: (bi, 0, 0)),
                  pl.BlockSpec((1, 1, 6 * D_MODEL), lambda bi, ti: (0, 0, 0)),
                  pl.BlockSpec((1, D_MODEL), lambda bi, ti: (0, 0)),
                  pl.BlockSpec((1, D_MODEL), lambda bi, ti: (0, 0))],
        out_specs=pl.BlockSpec((1, TOKEN_TILE, D_MODEL), tok),
        out_shape=jax.ShapeDtypeStruct((b, t_out, D_MODEL), F32),
        compiler_params=_cparams(("parallel", "parallel")),
        name="moe_combine_ln",
    )(x, y01, y01, wt, modl, modc, ln_g.reshape(1, -1), ln_b.reshape(1, -1))


def moe_layer(x, mods, router_w, router_b, w1, w3, w2, layer, ln_g, ln_b, *, tile_off):
    hb, idx, wt = moe_route(x, mods, router_w, router_b, tile_off=tile_off)
    b, t, _ = hb.shape
    n_tok = b * t
    n_pair = 2 * n_tok
    e_flat = idx.transpose(1, 0, 2).reshape(n_pair)
    onehot = (e_flat[:, None] == jnp.arange(N_EXPERTS)[None, :]).astype(jnp.int32)
    csum = jnp.cumsum(onehot, axis=0)
    counts = csum[-1]
    rank = jnp.sum((csum - onehot) * onehot, axis=1)
    padded = ((counts + FFN_TILE - 1) // FFN_TILE) * FFN_TILE
    ends = jnp.cumsum(padded)
    offs = ends - padded
    pos = offs[e_flat] + rank
    n_rows = n_pair + N_EXPERTS * FFN_TILE
    n_tiles = n_rows // FFN_TILE
    src = jnp.zeros((n_rows,), jnp.int32).at[pos].set(
        jnp.arange(n_pair, dtype=jnp.int32) % n_tok, mode="promise_in_bounds", unique_indices=True)
    tile_start = jnp.arange(n_tiles, dtype=jnp.int32) * FFN_TILE
    tile_expert = jnp.minimum(jnp.searchsorted(ends, tile_start, side="right"), N_EXPERTS - 1).astype(jnp.int32)
    n_used = (ends[-1] // FFN_TILE).astype(jnp.int32).reshape(1)
    xs = hb.reshape(n_tok, D_MODEL).at[src].get(mode="promise_in_bounds")
    ys = expert_ffn(xs, tile_expert, n_used, w1, w3, w2, layer)
    y01 = ys.at[pos].get(mode="promise_in_bounds", unique_indices=True).reshape(2, b, t, D_MODEL)
    return combine_ln(x, y01, wt, mods, ln_g, ln_b, tile_off=tile_off)


def _gate_rows(g):
    b, t = g.shape[:2]
    n = t // CHUNK
    ncp = 8 * ((n + 7) // 8)
    g = g.transpose(2, 3, 0, 4, 1).reshape(2, 2, b, g.shape[-1], n, CHUNK)
    return jnp.pad(g, ((0, 0),) * 4 + ((0, ncp - n), (0, 0)))


def gdn_layer(x, mods, w_in, conv_w, a_log, dt_bias, norm_g, w_out):
    wq = GDN_HEADS * GDN_DK
    n_main = 4 * wq
    z, ab = linear(x, w_in[:, :n_main].astype(BF16), mods=mods, modulate=(0, 1), w_small=w_in[:, n_main:])
    qkv = seq_prep(z, 3 * GDN_HEADS, conv_w=conv_w, norm_blocks=2 * GDN_HEADS, rope_blocks=2 * GDN_HEADS,
                   scale_blocks=GDN_HEADS, scale=GDN_DK ** -0.5)
    b, t, _ = x.shape
    ab = ab.reshape(b, t, 2, 2, GDN_HEADS)
    ab = ab.at[:, :, :, 0].add(dt_bias[None, None])
    rows = _gate_rows(ab)
    log_alpha = -jnp.exp(a_log)[:, None, :, None, None] * jax.nn.softplus(rows[:, 0])
    beta = jax.nn.sigmoid(rows[:, 1])
    valid = (jnp.arange(rows.shape[4]) < t // CHUNK)[:, None]
    log_alpha = jnp.where(valid, log_alpha, 0.0)
    o = gdn_scan(qkv, log_alpha, beta)
    return o, w_out, (GDN_DV, False, _silu), (z, 3, jnp.tile(norm_g, GDN_HEADS))


def mlstm_layer(x, mods, w_in, gate_b, norm_g, w_out):
    wq = MLSTM_HEADS * MLSTM_DQK
    wv = MLSTM_HEADS * MLSTM_DV
    n_main = 2 * wq + 2 * wv
    z, gt = linear(x, w_in[:, :n_main].astype(BF16), mods=mods, modulate=(0, 1), w_small=w_in[:, n_main:])
    qk = seq_prep(z, 2 * MLSTM_HEADS, norm_blocks=0, rope_blocks=2 * MLSTM_HEADS, scale_blocks=MLSTM_HEADS,
                  scale=MLSTM_DQK ** -0.5)
    b, t, _ = x.shape
    gt = gt.reshape(b, t, 2, 2, MLSTM_HEADS) + gate_b[None, None]
    rows = _gate_rows(gt)
    valid = (jnp.arange(rows.shape[4]) < t // CHUNK)[:, None]
    i_pre = rows[:, 0]
    log_f = jnp.where(valid, jax.nn.log_sigmoid(rows[:, 1]), 0.0)
    h = mlstm_scan(qk, z, i_pre, log_f)
    return h, w_out, (MLSTM_DV, True, _sigmoid), (z, 2, norm_g)


RWKV_IN_COLS = 3584


def _rwkv_in_kernel(x_ref, modl_ref, modc_ref, w_ref, o_ref, hd_s):
    t = x_ref.shape[1]

    @pl.when(pl.program_id(1) == 0)
    def _():
        row = _iota2(t, 1, 0)
        is_ctx = row < CTX_LEN
        sel = lambda i: jnp.where(is_ctx, modc_ref[0, :, i * D_MODEL:(i + 1) * D_MODEL],
                                  modl_ref[0, :, i * D_MODEL:(i + 1) * D_MODEL])
        h = x_ref[0] * (1.0 + sel(1)) + sel(0)
        seg_first = (row == 0) | (row == CTX_LEN)
        seg_last = (row == CTX_LEN - 1) | (row == t - 1)
        h_prev = jnp.where(seg_first, 0.0, pltpu.roll(h, 1, axis=0))
        h_next = jnp.where(seg_last, 0.0, pltpu.roll(h, t - 1, axis=0))
        hd_s[:, :D_MODEL] = h.astype(BF16)
        hd_s[:, D_MODEL:] = (0.5 * (h_prev + h_next) - h).astype(BF16)

    step = 3 * TOKEN_TILE
    for i in range(t // step):
        rows = slice(i * step, (i + 1) * step)
        o_ref[0, rows, :] = jnp.dot(hd_s[rows, :], w_ref[...], preferred_element_type=F32)


def rwkv_in_proj(x, mods, w_big):
    b, t, _ = x.shape
    modl, modc = mods
    n = w_big.shape[1]
    return pl.pallas_call(
        _rwkv_in_kernel,
        grid=(b, n // N_CHUNK_COLS),
        in_specs=[pl.BlockSpec((1, t, D_MODEL), lambda bi, ji: (bi, 0, 0)),
                  pl.BlockSpec((1, 1, 6 * D_MODEL), lambda bi, ji: (bi, 0, 0)),
                  pl.BlockSpec((1, 1, 6 * D_MODEL), lambda bi, ji: (0, 0, 0)),
                  pl.BlockSpec((2 * D_MODEL, N_CHUNK_COLS), lambda bi, ji: (0, ji))],
        out_specs=pl.BlockSpec((1, t, N_CHUNK_COLS), lambda bi, ji: (bi, 0, ji)),
        out_shape=jax.ShapeDtypeStruct((b, t, n), F32),
        scratch_shapes=[pltpu.VMEM((t, 2 * D_MODEL), BF16)],
        compiler_params=_cparams(("parallel", "arbitrary")),
        name="rwkv_in_proj",
    )(x, modl, modc, w_big)


def _seg64_sums(x):
    left = _iota2(1, 128, 1) < RWKV_HEAD
    parts = []
    for blk_i in range(x.shape[1] // 128):
        blk = x[:, blk_i * 128:(blk_i + 1) * 128]
        s_l = jnp.sum(jnp.where(left, blk, 0.0), axis=1, keepdims=True)
        s_r = jnp.sum(jnp.where(left, 0.0, blk), axis=1, keepdims=True)
        parts.append(jnp.where(left, s_l, s_r))
    return jnp.concatenate(parts, axis=1)


def _softplus(x):
    return jnp.maximum(x, 0.0) + jnp.log(1.0 + jnp.exp(-jnp.abs(x)))


def _rwkv_mid_kernel(z_ref, w2_ref, a2_ref, g2_ref, vec_ref, r_ref, v_ref, kk_ref, lw0_ref, lw1_ref,
                     kd0_ref, kd1_ref, bv0_ref, bv1_ref, g_ref):
    d = D_MODEL
    z = z_ref[0]
    r, k, v = z[:, :d], z[:, d:2 * d], z[:, 2 * d:3 * d]
    zg, zw, za = z[:, 3 * d:3 * d + 128], z[:, 3 * d + 128:3 * d + 256], z[:, 3 * d + 256:3 * d + 384]
    vec = vec_ref[...]
    k_k, k_a = vec[4:5], vec[5:6]
    r_ref[0] = r
    v_ref[0] = v
    g_ref[0] = _mm(_sigmoid(zg), g2_ref[...])
    kx = k * k_k
    kk = kx * lax.rsqrt(_seg64_sums(kx * kx) + 1e-6)
    kk_ref[0] = kk
    tw = jnp.tanh(zw)
    for dr, (lw_ref, kd_ref, bv_ref) in enumerate(((lw0_ref, kd0_ref, bv0_ref), (lw1_ref, kd1_ref, bv1_ref))):
        w_raw = -_softplus(-(vec[dr:dr + 1] + _mm(tw, w2_ref[dr]))) - 0.5
        a = _sigmoid(vec[2 + dr:3 + dr] + _mm(za, a2_ref[dr]))
        lw_ref[0] = -jnp.exp(w_raw)
        kd_ref[0] = k * (1.0 + (a - 1.0) * k_a)
        bv_ref[0] = kk * a


def rwkv_mid(z1, w2p, a2p, g2, vec):
    b, t, n = z1.shape
    tok = pl.BlockSpec((1, TOKEN_TILE, D_MODEL), lambda bi, ti: (bi, ti, 0))
    full = lambda a: pl.BlockSpec(a.shape, lambda bi, ti: (0,) * a.ndim)
    return pl.pallas_call(
        _rwkv_mid_kernel,
        grid=(b, t // TOKEN_TILE),
        in_specs=[pl.BlockSpec((1, TOKEN_TILE, n), lambda bi, ti: (bi, ti, 0)),
                  full(w2p), full(a2p), full(g2), full(vec)],
        out_specs=[tok] * 10,
        out_shape=[jax.ShapeDtypeStruct((b, t, D_MODEL), F32)] * 10,
        compiler_params=_cparams(("parallel", "parallel")),
        name="rwkv_mid",
    )(z1, w2p, a2p, g2, vec)


def _rwkv_post(y, r, v, kd0, kd1, g, params):
    inv = 1.0 / RWKV_HEAD
    yc = y - _seg64_sums(y) * inv
    yn = yc * lax.rsqrt(_seg64_sums(yc * yc) * inv + RWKV_GN_EPS)
    bonus = _seg64_sums(r * (kd0 + kd1) * params[2:3]) * v
    return (yn * params[0:1] + params[1:2] + bonus) * g


def rwkv_layer(x, mods, mu, w_rkv, w0, w1, w2, a0, a1, a2, g1, g2, k_k, k_a, r_k, lnx_g, lnx_b, w_out):
    d = D_MODEL
    cols = [(w_rkv[0], 0), (w_rkv[1], 2), (w_rkv[2], 3), (g1, 5), (w1[0], 1), (w1[1], 1), (a1[0], 4), (a1[1], 4)]
    top = jnp.concatenate([w for w, _ in cols], axis=1)
    bot = jnp.concatenate([mu[j][:, None] * w for w, j in cols], axis=1)
    w_big = jnp.pad(jnp.concatenate([top, bot], axis=0), ((0, 0), (0, RWKV_IN_COLS - top.shape[1]))).astype(BF16)
    z1 = rwkv_in_proj(x, mods, w_big)
    pad_dir = lambda w: jnp.stack([jnp.pad(w[0], ((0, w.shape[1]), (0, 0))), jnp.pad(w[1], ((w.shape[1], 0), (0, 0)))])
    vec = jnp.concatenate([w0, a0, k_k[None], k_a[None], jnp.zeros((2, d), F32)], axis=0)
    r, v, kk, lw0, lw1, kd0, kd1, bv0, bv1, g = rwkv_mid(z1, pad_dir(w2), pad_dir(a2), g2, vec)
    y = rwkv_scan(r, v, kk, (lw0, lw1), (kd0, kd1), (bv0, bv1))
    params = jnp.concatenate([lnx_g[None], lnx_b[None], r_k.reshape(1, d), jnp.zeros((5, d), F32)], axis=0)
    return y, w_out, "rwkv", (r, v, kd0, kd1, g, params)


def kernel(x, c, ctx, c_ctx, ada_w, ada_b, ln_g, ln_b, router_w, router_b, moe_w1, moe_w3, moe_w2, gdn_w_in, gdn_conv, gdn_a_log, gdn_dt_bias, gdn_norm_g, gdn_w_out, mlstm_w_in, mlstm_gate_b, mlstm_norm_g, mlstm_w_out, rwkv_mu, rwkv_w_rkv, rwkv_w0, rwkv_w1, rwkv_w2, rwkv_a0, rwkv_a1, rwkv_a2, rwkv_g1, rwkv_g2, rwkv_k_k, rwkv_k_a, rwkv_r_k, rwkv_lnx_g, rwkv_lnx_b, rwkv_w_out, na_w_in, na_rpb, na_w_out):
    b = x.shape[0]
    ctx_tiles = CTX_LEN // TOKEN_TILE
    mod_all = modulation_all(c, c_ctx, ada_w, ada_b)
    xs = jnp.concatenate([ctx, x], axis=1)
    assert DEPTH == 4
    for i in range(DEPTH):
        mods = (mod_all[i, :b, None, :], mod_all[i, b:b + 1, None, :])
        if i % 4 == 0:
            y, w_out, post, post_args = gdn_layer(xs, mods, gdn_w_in, gdn_conv, gdn_a_log, gdn_dt_bias,
                                                  gdn_norm_g, gdn_w_out)
        elif i % 4 == 1:
            y, w_out, post, post_args = mlstm_layer(xs, mods, mlstm_w_in, mlstm_gate_b, mlstm_norm_g, mlstm_w_out)
        elif i % 4 == 2:
            y, w_out, post, post_args = rwkv_layer(xs, mods, rwkv_mu, rwkv_w_rkv, rwkv_w0, rwkv_w1, rwkv_w2, rwkv_a0, rwkv_a1,
                                  rwkv_a2, rwkv_g1, rwkv_g2, rwkv_k_k, rwkv_k_a, rwkv_r_k, rwkv_lnx_g,
                                  rwkv_lnx_b, rwkv_w_out)
        else:
            z = linear(xs, na_w_in.astype(BF16), mods=mods, modulate=(0, 1))
            y, w_out, post, post_args = na_attention(z, _na_bias_table(na_rpb)), na_w_out, None, None
        off = ctx_tiles if (i % 4 == 3) else 0
        xs1 = out_proj_ln(y, w_out.astype(BF16), xs, mods, ln_g[i, 0], ln_b[i, 0], gate_idx=2, tile_off=off,
                          post=post, post_args=post_args)
        xs = moe_layer(xs1, mods, router_w, router_b, moe_w1, moe_w3, moe_w2, i, ln_g[i, 1], ln_b[i, 1],
                       tile_off=off)
    return xs
```

```python
import functools

import numpy as np
import jax
import jax.numpy as jnp
from jax import lax
from jax.experimental import pallas as pl
from jax.experimental.pallas import tpu as pltpu

F32 = jnp.float32
BF16 = jnp.bfloat16

D_MODEL = 1024
DEPTH = 4
GRID_W = 64
CTX_LEN = 256
ALPHA = (2 * DEPTH) ** 0.25
LN_EPS = 1e-5
ROPE_BASE = 10000.0

GDN_HEADS = 8
GDN_DK = 128
GDN_DV = 128
MLSTM_HEADS = 4
MLSTM_DQK = 128
MLSTM_DV = 256
RWKV_HEAD = 64
RWKV_HEADS = 16
RWKV_GN_EPS = 64e-5
NA_HEADS = 16
NA_DH = 64
NA_WIN_ROWS = 8
NA_WIN_COLS = 16
N_EXPERTS = 16
N_GROUPS = 4
D_EXPERT = 512

CHUNK = 64
GDN_UNROLL = 12
RWKV_UNROLL = 6
NA_ROWS_PER_STEP = 4
MLSTM_HEADS_PER_STEP = 2
TOKEN_TILE = 256
N_CHUNK_COLS = 512
FFN_TILE = 512
NEG = -1e30
VMEM_LIMIT = 56 * 1024 * 1024

_HI = lax.Precision.HIGHEST


def _cparams(sem):
    return pltpu.CompilerParams(dimension_semantics=sem, vmem_limit_bytes=VMEM_LIMIT)


def _mm(a, b):
    return jnp.dot(a.astype(BF16), b.astype(BF16), preferred_element_type=F32)


def _mm_nt(a, b):
    return lax.dot_general(a.astype(BF16), b.astype(BF16), (((1,), (1,)), ((), ())),
                           preferred_element_type=F32)


def _mm_tn(a, b):
    return lax.dot_general(a.astype(BF16), b.astype(BF16), (((0,), (0,)), ((), ())),
                           preferred_element_type=F32)


def _mmf(a, b):
    return jnp.dot(a, b, preferred_element_type=F32, precision=_HI)


def _silu(x):
    return x * (1.0 / (1.0 + jnp.exp(-x)))


def _sigmoid(x):
    return 1.0 / (1.0 + jnp.exp(-x))


def _mod_kernel(s_ref, w_ref, b_ref, o_ref):
    o_ref[0] = _mmf(_silu(s_ref[...]), w_ref[0]) + b_ref[0]


def modulation_all(c, c_ctx, ada_w, ada_b):
    b = c.shape[0]
    rows = 8 * ((b + 1 + 7) // 8)
    s = jnp.zeros((rows, D_MODEL), F32).at[:b].set(c).at[b].set(c_ctx)
    tn = 1536
    n = ada_w.shape[-1]
    return pl.pallas_call(
        _mod_kernel,
        grid=(DEPTH, n // tn),
        in_specs=[pl.BlockSpec((rows, D_MODEL), lambda i, j: (0, 0)),
                  pl.BlockSpec((1, D_MODEL, tn), lambda i, j: (i, 0, j)),
                  pl.BlockSpec((1, 1, tn), lambda i, j: (i, 0, j))],
        out_specs=pl.BlockSpec((1, rows, tn), lambda i, j: (i, 0, j)),
        out_shape=jax.ShapeDtypeStruct((DEPTH, rows, n), F32),
        compiler_params=_cparams(("arbitrary", "arbitrary")),
        name="adaln_modulation",
    )(s, ada_w, ada_b.reshape(DEPTH, 1, n))


def _mod_rows(modl_ref, modc_ref, is_ctx, idx):
    sl = slice(idx * D_MODEL, (idx + 1) * D_MODEL)
    return jnp.where(is_ctx, modc_ref[0, :, sl], modl_ref[0, :, sl])


def _linear_kernel(*refs, glob_off, modulate, act, n_main, has_small):
    it = iter(refs)
    x_ref = next(it)
    if modulate is not None:
        modl_ref, modc_ref = next(it), next(it)
    w_ref = next(it)
    ws_ref = next(it) if has_small else None
    o_ref = next(it)
    os_ref = next(it) if has_small else None

    h = x_ref[0]
    if modulate is not None:
        is_ctx = (pl.program_id(1) + glob_off) * TOKEN_TILE < CTX_LEN
        sh = _mod_rows(modl_ref, modc_ref, is_ctx, modulate[0])
        sc = _mod_rows(modl_ref, modc_ref, is_ctx, modulate[1])
        h = h * (1.0 + sc) + sh
    if act == "tanh":
        h = jnp.tanh(h)
    elif act == "sigmoid":
        h = _sigmoid(h)
    hb = h.astype(BF16)
    step = min(N_CHUNK_COLS, n_main)
    for j in range(n_main // step):
        o_ref[0, :, j * step:(j + 1) * step] = jnp.dot(
            hb, w_ref[:, j * step:(j + 1) * step], preferred_element_type=F32)
    if has_small:
        os_ref[0] = _mmf(h, ws_ref[...])


def linear(x, w_bf16, *, mods=None, modulate=None, act=None, w_small=None):
    b, t_out, k = x.shape
    n_main = w_bf16.shape[1]
    has_small = w_small is not None
    in_specs = [pl.BlockSpec((1, TOKEN_TILE, k), lambda bi, ti: (bi, ti, 0))]
    args = [x]
    if modulate is not None:
        modl, modc = mods
        in_specs += [pl.BlockSpec((1, 1, 6 * D_MODEL), lambda bi, ti: (bi, 0, 0)),
                     pl.BlockSpec((1, 1, 6 * D_MODEL), lambda bi, ti: (0, 0, 0))]
        args += [modl, modc]
    in_specs.append(pl.BlockSpec((k, n_main), lambda bi, ti: (0, 0)))
    args.append(w_bf16)
    out_specs = [pl.BlockSpec((1, TOKEN_TILE, n_main), lambda bi, ti: (bi, ti, 0))]
    out_shape = [jax.ShapeDtypeStruct((b, t_out, n_main), F32)]
    if has_small:
        ns = w_small.shape[1]
        in_specs.append(pl.BlockSpec((k, ns), lambda bi, ti: (0, 0)))
        args.append(w_small)
        out_specs.append(pl.BlockSpec((1, TOKEN_TILE, ns), lambda bi, ti: (bi, ti, 0)))
        out_shape.append(jax.ShapeDtypeStruct((b, t_out, ns), F32))
    res = pl.pallas_call(
        functools.partial(_linear_kernel, glob_off=0, modulate=modulate, act=act,
                          n_main=n_main, has_small=has_small),
        grid=(b, t_out // TOKEN_TILE),
        in_specs=in_specs, out_specs=out_specs, out_shape=out_shape,
        compiler_params=_cparams(("parallel", "parallel")),
        name="linear",
    )(*args)
    return res if has_small else res[0]


def _layer_norm_rows(r, g, b):
    mu = jnp.mean(r, axis=-1, keepdims=True)
    rc = r - mu
    var = jnp.mean(rc * rc, axis=-1, keepdims=True)
    return rc * lax.rsqrt(var + LN_EPS) * g + b


def _head_post(y, gate, norm_g, post):
    head_w, centre, act = post
    parts = []
    for h in range(y.shape[1] // head_w):
        seg = y[:, h * head_w:(h + 1) * head_w]
        if centre:
            seg = seg - jnp.mean(seg, axis=1, keepdims=True)
        parts.append(seg * lax.rsqrt(jnp.mean(seg * seg, axis=1, keepdims=True) + 1e-6))
    return jnp.concatenate(parts, axis=1) * norm_g * act(gate)


def _out_ln_kernel(*refs, tile_off, gate_idx, post):
    if post is None:
        y_ref, w_ref, x_ref, modl_ref, modc_ref, g_ref, b_ref, o_ref = refs
        y = y_ref[0]
    elif post == "rwkv":
        y_ref, r_ref, v_ref, kd0_ref, kd1_ref, gg_ref, pr_ref = refs[:7]
        w_ref, x_ref, modl_ref, modc_ref, g_ref, b_ref, o_ref = refs[7:]
        y = _rwkv_post(y_ref[0], r_ref[0], v_ref[0], kd0_ref[0], kd1_ref[0], gg_ref[0], pr_ref[...])
    else:
        y_ref, gate_ref, ng_ref, w_ref, x_ref, modl_ref, modc_ref, g_ref, b_ref, o_ref = refs
        y = _head_post(y_ref[0], gate_ref[0], ng_ref[...], post)
    is_ctx = (pl.program_id(1) + tile_off) * TOKEN_TILE < CTX_LEN
    gate = _mod_rows(modl_ref, modc_ref, is_ctx, gate_idx)
    f = jnp.dot(y.astype(BF16), w_ref[...], preferred_element_type=F32)
    r = ALPHA * x_ref[0] + gate * f
    o_ref[0] = _layer_norm_rows(r, g_ref[...], b_ref[...])


def out_proj_ln(y, w_bf16, x, mods, ln_g, ln_b, *, gate_idx, tile_off=0, post=None, post_args=None):
    b, t_y, k = y.shape
    modl, modc = mods
    tok = pl.BlockSpec((1, TOKEN_TILE, k), lambda bi, ti: (bi, ti, 0))
    in_specs, args = [tok], [y]
    if post == "rwkv":
        *tiles, params = post_args
        in_specs += [tok] * len(tiles) + [pl.BlockSpec(params.shape, lambda bi, ti: (0, 0))]
        args += [*tiles, params]
    elif post is not None:
        z, gate_block, norm_g = post_args
        in_specs += [pl.BlockSpec((1, TOKEN_TILE, k), lambda bi, ti: (bi, ti, gate_block)),
                     pl.BlockSpec((1, k), lambda bi, ti: (0, 0))]
        args += [z, norm_g.reshape(1, k)]
    in_specs += [pl.BlockSpec((k, D_MODEL), lambda bi, ti: (0, 0)),
                 pl.BlockSpec((1, TOKEN_TILE, D_MODEL), lambda bi, ti: (bi, ti + tile_off, 0)),
                 pl.BlockSpec((1, 1, 6 * D_MODEL), lambda bi, ti: (bi, 0, 0)),
                 pl.BlockSpec((1, 1, 6 * D_MODEL), lambda bi, ti: (0, 0, 0)),
                 pl.BlockSpec((1, D_MODEL), lambda bi, ti: (0, 0)),
                 pl.BlockSpec((1, D_MODEL), lambda bi, ti: (0, 0))]
    args += [w_bf16, x, modl, modc, ln_g.reshape(1, -1), ln_b.reshape(1, -1)]
    return pl.pallas_call(
        functools.partial(_out_ln_kernel, tile_off=tile_off, gate_idx=gate_idx, post=post),
        grid=(b, t_y // TOKEN_TILE),
        in_specs=in_specs,
        out_specs=pl.BlockSpec((1, TOKEN_TILE, D_MODEL), lambda bi, ti: (bi, ti, 0)),
        out_shape=jax.ShapeDtypeStruct((b, t_y, D_MODEL), F32),
        compiler_params=_cparams(("parallel", "parallel")),
        name="out_proj_ln",
    )(*args)


def _seq_prep_kernel(z_ref, cos_ref, sin_ref, cw_ref, o_ref, *, conv, norm_blocks, rope_blocks, scale_blocks, scale):
    x = z_ref[0]
    t = x.shape[0]
    j = pl.program_id(1)
    if conv:
        row = _iota2(t, 1, 0)
        seg_first = (row == 0) | (row == CTX_LEN)
        seg_last = (row == CTX_LEN - 1) | (row == t - 1)
        x_prev = jnp.where(seg_first, 0.0, pltpu.roll(x, 1, axis=0))
        x_next = jnp.where(seg_last, 0.0, pltpu.roll(x, t - 1, axis=0))
        w = cw_ref[...]
        x = _silu(x_prev * w[0:1] + x * w[1:2] + x_next * w[2:3])
    if norm_blocks:
        normed = x * lax.rsqrt(jnp.sum(x * x, axis=1, keepdims=True) + 1e-6)
        x = jnp.where(j < norm_blocks, normed, x)
    lane = _iota2(1, x.shape[1], 1)
    partner = jnp.where((lane % 64) < 32, pltpu.roll(x, 96, axis=1), pltpu.roll(x, 32, axis=1))
    roped = x * cos_ref[...] + partner * sin_ref[...]
    x = jnp.where(j < rope_blocks, roped, x)
    o_ref[0] = jnp.where(j < scale_blocks, x * scale, x)


def _rope_lane_tables(t):
    pos = jnp.arange(t - CTX_LEN)
    inv_freq = ROPE_BASE ** (-jnp.arange(32, dtype=F32) / 32)
    ang_r = (pos // GRID_W).astype(F32)[:, None] * inv_freq[None, :]
    ang_c = (pos % GRID_W).astype(F32)[:, None] * inv_freq[None, :]
    cos = jnp.concatenate([jnp.cos(ang_r), jnp.cos(ang_r), jnp.cos(ang_c), jnp.cos(ang_c)], axis=1)
    sin = jnp.concatenate([-jnp.sin(ang_r), jnp.sin(ang_r), -jnp.sin(ang_c), jnp.sin(ang_c)], axis=1)
    pad = lambda a, v: jnp.concatenate([jnp.full((CTX_LEN, 128), v, F32), a], axis=0)
    return pad(cos, 1.0), pad(sin, 0.0)


def seq_prep(z, n_blocks, *, conv_w=None, norm_blocks, rope_blocks, scale_blocks, scale):
    b, t, _ = z.shape
    cos, sin = _rope_lane_tables(t)
    conv = conv_w is not None
    cw = conv_w if conv else jnp.zeros((3, n_blocks * 128), F32)
    return pl.pallas_call(
        functools.partial(_seq_prep_kernel, conv=conv, norm_blocks=norm_blocks, rope_blocks=rope_blocks,
                          scale_blocks=scale_blocks, scale=scale),
        grid=(b, n_blocks),
        in_specs=[pl.BlockSpec((1, t, 128), lambda bi, ji: (bi, 0, ji)),
                  pl.BlockSpec((t, 128), lambda bi, ji: (0, 0)),
                  pl.BlockSpec((t, 128), lambda bi, ji: (0, 0)),
                  pl.BlockSpec((3, 128), lambda bi, ji: (0, ji))],
        out_specs=pl.BlockSpec((1, t, 128), lambda bi, ji: (bi, 0, ji)),
        out_shape=jax.ShapeDtypeStruct((b, t, n_blocks * 128), F32),
        compiler_params=_cparams(("parallel", "parallel")),
        name="seq_prep",
    )(z, cos, sin, cw)


def _iota2(n, m, axis):
    return lax.broadcasted_iota(jnp.int32, (n, m), axis)


def _row_to_col(row, eye):
    return jnp.sum(jnp.where(eye, row, 0.0), axis=1, keepdims=True)


def _tri_solve_steps(n_mat, rhs_list, eye_f, blockdiag):
    nd = jnp.where(blockdiag, n_mat, 0.0)
    ne = n_mat - nd
    p = eye_f + nd
    n2 = _mm(nd, nd)
    yield
    p = p + _mm(p, n2)
    n4 = _mm(n2, n2)
    yield
    p = p + _mm(p, n4)
    n8 = _mm(n4, n4)
    yield
    dinv = p + _mm(p, n8)
    yield
    m = _mm(dinv, ne)
    xs = [_mm(dinv, r) for r in rhs_list]
    yield
    m2 = _mm(m, m)
    xs = [x + _mm(m, x) for x in xs]
    yield
    xs = [x + _mm(m2, x) for x in xs]
    yield
    return xs


def _run_interleaved(gens):
    gens = list(gens)
    while gens:
        alive = []
        for g in gens:
            try:
                next(g)
                alive.append(g)
            except StopIteration:
                pass
        gens = alive


def _split2(x):
    hi = x.astype(BF16)
    return hi, (x - hi.astype(F32)).astype(BF16)


def _cumsum_rows(x, cum):
    hi, lo = _split2(x)
    cb = cum.astype(BF16)
    return jnp.dot(hi, cb, preferred_element_type=F32) + jnp.dot(lo, cb, preferred_element_type=F32)


def _cumsum_cols(cum, x):
    hi, lo = _split2(x)
    cb = cum.astype(BF16)
    return jnp.dot(cb, hi, preferred_element_type=F32) + jnp.dot(cb, lo, preferred_element_type=F32)


def _chunk_order(direction, n_ctx, n_tot):
    def order(j):
        if direction == 0:
            return j
        return jnp.where(j < n_ctx, n_ctx - 1 - j, n_tot - 1 - (j - n_ctx))
    return order


def _masks(direction):
    ii = _iota2(CHUNK, CHUNK, 0)
    jj = _iota2(CHUNK, CHUNK, 1)
    if direction == 0:
        incl, strict = jj <= ii, jj < ii
    else:
        incl, strict = jj >= ii, jj > ii
    return ii, jj, incl, strict


def _gdn_kernel(q_ref, k_ref, v_ref, la_ref, be_ref, o_ref, g_s, lhs_s, add_s, s_s, ob_s, *, n_ctx, n_tot):
    ii, jj, _, _ = _masks(0)
    eye = ii == jj
    eye_f = eye.astype(F32)
    blockdiag = (ii // 16) == (jj // 16)
    dirs = []
    for d in (0, 1):
        _, _, incl, strict = _masks(d)
        cum = (ii <= jj).astype(F32) if d == 0 else (ii >= jj).astype(F32)
        g_s[d] = _cumsum_rows(la_ref[d, 0, 0], cum)
        dirs.append((incl, strict, CHUNK - 1 if d == 0 else 0))

    def prep(j, carry):
        loaded = []
        for uu in range(GDN_UNROLL):
            n = j * GDN_UNROLL + uu
            rows = pl.ds(pl.multiple_of(n * CHUNK, CHUNK), CHUNK)
            loaded.append((n, q_ref[0, rows, :], k_ref[0, rows, :], v_ref[0, rows, :],
                           [g_s[d, pl.ds(n, 1), :] for d in (0, 1)],
                           [be_ref[d, 0, 0, pl.ds(n, 1), :] for d in (0, 1)]))
        stores = []
        grams = [(_mm_nt(kc, kc), _mm_nt(qc, kc)) for _, qc, kc, _, _, _ in loaded]

        def chain(n, qc, kc, vc, g_row, be_row, kk, qk, d, incl, strict, last):
            g_col = _row_to_col(g_row, eye)
            be_col = _row_to_col(be_row, eye)
            g_last = g_row[:, last:last + 1]
            gamma = jnp.exp(jnp.where(incl, g_col - g_row, NEG))
            n_mat = jnp.where(strict, -(be_col * kk * gamma), 0.0)
            eg = jnp.exp(g_col)
            rhs = jnp.concatenate([be_col * vc, (be_col * eg) * kc], axis=1)
            (uw,) = yield from _tri_solve_steps(n_mat, [rhs], eye_f, blockdiag)
            p_mat = jnp.where(incl, qk * gamma, 0.0)
            p_uw = _mm(p_mat, uw)
            k_dec = kc * jnp.exp(g_last - g_col)
            k_uw = _mm_tn(k_dec, uw)
            q_t = qc * eg - p_uw[:, GDN_DV:]
            stores.append((d, n, jnp.concatenate([q_t, -k_uw[:, GDN_DV:]], axis=0).astype(BF16),
                           jnp.concatenate([p_uw[:, :GDN_DV], k_uw[:, :GDN_DV]], axis=0)))

        _run_interleaved(
            chain(n, qc, kc, vc, g_rows[d], be_rows[d], kk, qk, d, *dirs[d])
            for (n, qc, kc, vc, g_rows, be_rows), (kk, qk) in zip(loaded, grams) for d in (0, 1))
        for d, n, lhs, add in stores:
            lhs_s[d, n] = lhs
            add_s[d, n] = add
        return carry

    lax.fori_loop(0, n_tot // GDN_UNROLL, prep, 0)

    s_s[...] = jnp.zeros_like(s_s)
    orders = [_chunk_order(d, n_ctx, n_tot) for d in (0, 1)]
    o_refs = (o_ref.at[0], ob_s)

    def step(j, carry):
        loaded = []
        for d, (_, _, last) in enumerate(dirs):
            n = orders[d](j)
            loaded.append((n, s_s[d], lhs_s[d, n], add_s[d, n], g_s[d, pl.ds(n, 1), last:last + 1]))
        results = []
        for n, s, lhs, add, g_last in loaded:
            z = jnp.dot(lhs, s.astype(BF16), preferred_element_type=F32) + add
            results.append((n, jnp.exp(g_last) * s + z[CHUNK:], z[:CHUNK]))
        for d, (n, s_new, o) in enumerate(results):
            s_s[d] = s_new
            o_refs[d][pl.ds(pl.multiple_of(n * CHUNK, CHUNK), CHUNK), :] = o
        return carry

    lax.fori_loop(0, n_tot, step, 0)
    o_ref[0] += ob_s[...]


def gdn_scan(qkv, log_alpha, beta):
    b, t, _ = qkv.shape
    n_tot = t // CHUNK
    ncp = log_alpha.shape[3]
    blk = pl.BlockSpec((1, t, GDN_DK), lambda bi, hi: (bi, 0, hi))
    k_blk = pl.BlockSpec((1, t, GDN_DK), lambda bi, hi: (bi, 0, GDN_HEADS + hi))
    v_blk = pl.BlockSpec((1, t, GDN_DV), lambda bi, hi: (bi, 0, 2 * GDN_HEADS + hi))
    gate_blk = pl.BlockSpec((2, 1, 1, ncp, CHUNK), lambda bi, hi: (0, bi, hi, 0, 0))
    return pl.pallas_call(
        functools.partial(_gdn_kernel, n_ctx=CTX_LEN // CHUNK, n_tot=n_tot),
        grid=(b, GDN_HEADS),
        in_specs=[blk, k_blk, v_blk, gate_blk, gate_blk],
        out_specs=blk,
        out_shape=jax.ShapeDtypeStruct((b, t, GDN_HEADS * GDN_DV), F32),
        scratch_shapes=[pltpu.VMEM((2, ncp, CHUNK), F32),
                        pltpu.VMEM((2, n_tot, CHUNK + GDN_DK, GDN_DV), BF16),
                        pltpu.VMEM((2, n_tot, CHUNK + GDN_DK, GDN_DV), F32),
                        pltpu.VMEM((2, GDN_DK, GDN_DV), F32),
                        pltpu.VMEM((t, GDN_DV), F32)],
        compiler_params=_cparams(("parallel", "parallel")),
        name="gdn_scan",
    )(qkv, qkv, qkv, log_alpha, beta)


def _mlstm_kernel(q_ref, k_ref, v_ref, ip_ref, lf_ref, o_ref, b_s, c_s, n_s, m_s, ob_s, *, n_ctx, n_tot):
    ii, jj, _, _ = _masks(0)
    eye = ii == jj
    hps = MLSTM_HEADS_PER_STEP
    dirs = []
    for d in (0, 1):
        _, _, incl, _ = _masks(d)
        cum = (ii <= jj).astype(F32) if d == 0 else (ii >= jj).astype(F32)
        for hh in range(hps):
            b_s[d, hh] = _cumsum_rows(lf_ref[d, 0, hh], cum)
        dirs.append((incl, CHUNK - 1 if d == 0 else 0))
    c_s[...] = jnp.zeros_like(c_s)
    n_s[...] = jnp.zeros_like(n_s)
    m_s[...] = jnp.zeros_like(m_s)
    orders = [_chunk_order(d, n_ctx, n_tot) for d in (0, 1)]
    o_refs = (o_ref.at[0], ob_s)

    def step(j, carry):
        loaded = {}
        for d in (0, 1):
            n = orders[d](j)
            rows = pl.ds(pl.multiple_of(n * CHUNK, CHUNK), CHUNK)
            for hh in range(hps):
                qsl = slice(hh * MLSTM_DQK, (hh + 1) * MLSTM_DQK)
                vsl = slice(hh * MLSTM_DV, (hh + 1) * MLSTM_DV)
                loaded[(d, hh)] = (q_ref[0, rows, qsl], k_ref[0, rows, qsl], v_ref[0, rows, vsl],
                                   b_s[d, hh, pl.ds(n, 1), :], ip_ref[d, 0, hh, pl.ds(n, 1), :],
                                   c_s[d, hh], n_s[d, hh], m_s[d, hh], rows, vsl)
        results = {}

        def chain(key, qc, kc, vc, b_row, ip_row, c_st, n_st, m_st, incl, last):
            b_col = _row_to_col(b_row, eye)
            b_last = b_row[:, last:last + 1]
            qk = _mm_nt(qc, kc)
            qc_st = _mm(qc, c_st)
            qn = jnp.sum(qc * n_st, axis=1, keepdims=True)
            log_end = b_last - b_row + ip_row
            m_end = jnp.max(log_end, axis=1, keepdims=True)
            yield
            log_d = jnp.where(incl, b_col - b_row + ip_row, NEG)
            m_intra = jnp.max(log_d, axis=1, keepdims=True)
            m_new = jnp.maximum(b_last + m_st, m_end)
            decay = jnp.exp(b_last + m_st - m_new)
            kw_col = _row_to_col(jnp.exp(log_end - m_new), eye)
            yield
            m_row = jnp.maximum(b_col + m_st, m_intra)
            w_state = jnp.exp(b_col + m_st - m_row)
            k_w = kc * kw_col
            c_new = decay * c_st + _mm_tn(k_w, vc)
            n_new = decay * n_st + jnp.sum(k_w, axis=0, keepdims=True)
            w_intra = jnp.exp(log_d - m_row) * qk
            num = w_state * qc_st + _mm(w_intra, vc)
            den = w_state * qn + jnp.sum(w_intra, axis=1, keepdims=True)
            yield
            results[key] = (num / jnp.maximum(jnp.abs(den), jnp.exp(-m_row)), c_new, n_new, m_new)

        _run_interleaved(chain(key, *vals[:8], *dirs[key[0]]) for key, vals in loaded.items())
        for (d, hh), (h, c_new, n_new, m_new) in results.items():
            rows, vsl = loaded[(d, hh)][8:]
            c_s[d, hh] = c_new
            n_s[d, hh] = n_new
            m_s[d, hh] = m_new
            o_refs[d][rows, vsl] = h
        return carry

    lax.fori_loop(0, n_tot, step, 0)
    o_ref[0] += ob_s[...]


def mlstm_scan(qk, z, i_pre, log_f):
    b, t, _ = qk.shape
    n_tot = t // CHUNK
    ncp = i_pre.shape[3]
    hps = MLSTM_HEADS_PER_STEP
    n_steps = MLSTM_HEADS // hps
    v_off = 2 * MLSTM_HEADS * MLSTM_DQK // (hps * MLSTM_DV)
    q_blk = pl.BlockSpec((1, t, hps * MLSTM_DQK), lambda bi, hi: (bi, 0, hi))
    k_blk = pl.BlockSpec((1, t, hps * MLSTM_DQK), lambda bi, hi: (bi, 0, n_steps + hi))
    v_blk = pl.BlockSpec((1, t, hps * MLSTM_DV), lambda bi, hi: (bi, 0, v_off + hi))
    o_blk = pl.BlockSpec((1, t, hps * MLSTM_DV), lambda bi, hi: (bi, 0, hi))
    gate_blk = pl.BlockSpec((2, 1, hps, ncp, CHUNK), lambda bi, hi: (0, bi, hi, 0, 0))
    return pl.pallas_call(
        functools.partial(_mlstm_kernel, n_ctx=CTX_LEN // CHUNK, n_tot=n_tot),
        grid=(b, n_steps),
        in_specs=[q_blk, k_blk, v_blk, gate_blk, gate_blk],
        out_specs=o_blk,
        out_shape=jax.ShapeDtypeStruct((b, t, MLSTM_HEADS * MLSTM_DV), F32),
        scratch_shapes=[pltpu.VMEM((2, hps, ncp, CHUNK), F32),
                        pltpu.VMEM((2, hps, MLSTM_DQK, MLSTM_DV), F32),
                        pltpu.VMEM((2, hps, 1, MLSTM_DQK), F32),
                        pltpu.VMEM((2, hps, 1, 1), F32),
                        pltpu.VMEM((t, hps * MLSTM_DV), F32)],
        compiler_params=_cparams(("parallel", "parallel")),
        name="mlstm_scan",
    )(qk, qk, z, i_pre, log_f)


def _rwkv_kernel(r_ref, v_ref, kk_ref, lw0_ref, lw1_ref, kd0_ref, kd1_ref, bv0_ref, bv1_ref, o_ref,
                 lhs_s, add_s, gl_s, s_s, ob_s, *, n_ctx, n_tot):
    dir_refs = ((lw0_ref, kd0_ref, bv0_ref), (lw1_ref, kd1_ref, bv1_ref))
    ii, jj, _, _ = _masks(0)
    eye = ii == jj
    eye_f = eye.astype(F32)
    blockdiag = (ii // 16) == (jj // 16)
    hd = RWKV_HEAD
    dirs = []
    for d in (0, 1):
        _, _, incl, strict = _masks(d)
        dirs.append((incl, strict, CHUNK - 1 if d == 0 else 0))
    zero = jnp.zeros((hd, hd), F32)

    def prep(j, carry):
        loaded = []
        for uu in range(RWKV_UNROLL):
            n = j * RWKV_UNROLL + uu
            rows = pl.ds(pl.multiple_of(n * CHUNK, CHUNK), CHUNK)
            loaded.append((n, r_ref[0, rows, :], v_ref[0, rows, :], kk_ref[0, rows, :],
                           [tuple(ref[0, rows, :] for ref in dir_refs[d]) for d in (0, 1)]))
        parts = {}

        def chain(key, r, v, kk, lw, kd, bv, gcs, e_end, incl, strict):
            e_neg = jnp.exp(-gcs)
            a_h = -kk * jnp.exp(gcs - lw)
            r_h = r * jnp.exp(gcs)
            b_h = bv * e_neg
            k_h = kd * e_neg
            a_ab = jnp.where(strict, _mm_nt(a_h, b_h), 0.0)
            a_ak = jnp.where(strict, _mm_nt(a_h, k_h), 0.0)
            a_rb = jnp.where(incl, _mm_nt(r_h, b_h), 0.0)
            a_rk = jnp.where(incl, _mm_nt(r_h, k_h), 0.0)
            yield
            av = _mm(a_ak, v)
            ta, tav = yield from _tri_solve_steps(a_ab, [a_h, av], eye_f, blockdiag)
            b_g = bv * e_end
            parts[key] = (r_h + _mm(a_rb, ta), _mm_tn(b_g, ta),
                          _mm(a_rb, tav) + _mm(a_rk, v), _mm_tn(b_g, tav) + _mm_tn(kd * e_end, v))

        gens, gls = [], {}
        for uu, (_, r2, v2, kk2, per_dir) in enumerate(loaded):
            for d, (incl, strict, last) in enumerate(dirs):
                lw2, kd2, bv2 = per_dir[d]
                gcs2 = _cumsum_cols(incl.astype(F32), lw2)
                gl_row2 = gcs2[last:last + 1, :]
                e_end2 = jnp.exp(gl_row2 - gcs2)
                gls[(uu, d)] = jnp.exp(gl_row2)
                for hh in range(2):
                    sl = slice(hh * hd, (hh + 1) * hd)
                    gens.append(chain((uu, d, hh), *(a[:, sl] for a in (r2, v2, kk2, lw2, kd2, bv2, gcs2, e_end2)),
                                      incl, strict))
        _run_interleaved(gens)
        for uu, (n, _, _, _, _) in enumerate(loaded):
            for d in (0, 1):
                (rt0, mx0, yc0, kv0), (rt1, mx1, yc1, kv1) = parts[(uu, d, 0)], parts[(uu, d, 1)]
                stack = lambda t0, t1, b0, b1: jnp.concatenate(
                    [jnp.concatenate([t0, t1], axis=1), jnp.concatenate([b0, zero], axis=1),
                     jnp.concatenate([zero, b1], axis=1)], axis=0)
                lhs_s[d, n] = stack(rt0, rt1, mx0, mx1).astype(BF16)
                add_s[d, n] = stack(yc0, yc1, kv0, kv1)
                gl_s[d, pl.ds(n, 1), :] = gls[(uu, d)]
        return carry

    lax.fori_loop(0, n_tot // RWKV_UNROLL, prep, 0)

    s_s[...] = jnp.zeros_like(s_s)
    orders = [_chunk_order(d, n_ctx, n_tot) for d in (0, 1)]
    eye2 = _iota2(2 * hd, 2 * hd, 0) == _iota2(2 * hd, 2 * hd, 1)
    o_refs = (o_ref.at[0], ob_s)

    def step(j, carry):
        loaded = []
        for d in (0, 1):
            n = orders[d](j)
            loaded.append((n, s_s[d], lhs_s[d, n], add_s[d, n], gl_s[d, pl.ds(n, 1), :]))
        results = []
        for n, s, lhs, add, gl in loaded:
            z = jnp.dot(lhs, s.astype(BF16), preferred_element_type=F32) + add
            results.append((n, _row_to_col(gl, eye2) * s + z[CHUNK:], z[:CHUNK]))
        for d, (n, s_new, y) in enumerate(results):
            s_s[d] = s_new
            o_refs[d][pl.ds(pl.multiple_of(n * CHUNK, CHUNK), CHUNK), :] = y
        return carry

    lax.fori_loop(0, n_tot, step, 0)
    o_ref[0] += ob_s[...]


def rwkv_scan(r, v, kk, log_w, k_dir, b_dir):
    b, t, _ = r.shape
    n_tot = t // CHUNK
    blk = pl.BlockSpec((1, t, 128), lambda bi, hi: (bi, 0, hi))
    return pl.pallas_call(
        functools.partial(_rwkv_kernel, n_ctx=CTX_LEN // CHUNK, n_tot=n_tot),
        grid=(b, RWKV_HEADS // 2),
        in_specs=[blk] * 9,
        out_specs=blk,
        out_shape=jax.ShapeDtypeStruct((b, t, D_MODEL), F32),
        scratch_shapes=[pltpu.VMEM((2, n_tot, CHUNK + 2 * RWKV_HEAD, 2 * RWKV_HEAD), BF16),
                        pltpu.VMEM((2, n_tot, CHUNK + 2 * RWKV_HEAD, 2 * RWKV_HEAD), F32),
                        pltpu.VMEM((2, 8 * ((n_tot + 7) // 8), 2 * RWKV_HEAD), F32),
                        pltpu.VMEM((2, 2 * RWKV_HEAD, 2 * RWKV_HEAD), F32),
                        pltpu.VMEM((t, 2 * RWKV_HEAD), F32)],
        compiler_params=_cparams(("parallel", "parallel")),
        name="rwkv_scan",
    )(r, v, kk, *log_w, *k_dir, *b_dir)


def _na_kernel(q_ref, k_ref, v_ref, bias_ref, o_ref, *, rows):
    scale = NA_DH ** -0.5
    slab = NA_WIN_ROWS * GRID_W
    lane = _iota2(1, 2 * NA_DH, 1)
    head_masks = (lane < NA_DH, lane >= NA_DH)
    kc2 = k_ref[0, 0:CTX_LEN, :].astype(BF16)
    vc2 = v_ref[0, 0:CTX_LEN, :].astype(BF16)

    def body(j, carry):
        loaded = []
        for uu in range(NA_ROWS_PER_STEP):
            r = j * NA_ROWS_PER_STEP + uu
            r0 = jnp.clip(r - NA_WIN_ROWS // 2, 0, rows - NA_WIN_ROWS)
            dr0 = r0 - r + NA_WIN_ROWS - 1
            krows = pl.ds(pl.multiple_of(CTX_LEN + r0 * GRID_W, GRID_W), slab)
            loaded.append((r, q_ref[0, pl.ds(pl.multiple_of(CTX_LEN + r * GRID_W, GRID_W), GRID_W), :],
                           k_ref[0, krows, :].astype(BF16), v_ref[0, krows, :].astype(BF16),
                           [bias_ref[hh, pl.ds(dr0, 1)][0] for hh in range(2)]))
        outs = {}

        def chain(key, q2, ks2, vs2, bias, mask):
            qh = jnp.where(mask, q2, 0.0)
            s_lat = _mm_nt(qh, ks2) * scale + bias
            s_ctx = _mm_nt(qh, kc2) * scale
            yield
            m = jnp.maximum(jnp.max(s_lat, axis=1, keepdims=True), jnp.max(s_ctx, axis=1, keepdims=True))
            yield
            p_lat = jnp.exp(s_lat - m)
            p_ctx = jnp.exp(s_ctx - m)
            l = jnp.sum(p_lat, axis=1, keepdims=True) + jnp.sum(p_ctx, axis=1, keepdims=True)
            outs[key] = (_mm(p_lat, vs2) + _mm(p_ctx, vc2)) / l
            yield

        _run_interleaved(chain((uu, hh), q2, ks2, vs2, biases[hh], head_masks[hh])
                         for uu, (_, q2, ks2, vs2, biases) in enumerate(loaded) for hh in range(2))
        for uu, (r, _, _, _, _) in enumerate(loaded):
            o_ref[0, pl.ds(pl.multiple_of(r * GRID_W, GRID_W), GRID_W), :] = jnp.where(
                head_masks[0], outs[(uu, 0)], outs[(uu, 1)])
        return carry

    lax.fori_loop(0, rows // NA_ROWS_PER_STEP, body, 0)


def _na_bias_table(rpb):
    cols = np.arange(GRID_W)
    win_c0 = np.clip(cols - NA_WIN_COLS // 2, 0, GRID_W - NA_WIN_COLS)
    kc = np.arange(GRID_W)
    in_win = (kc[None, :] >= win_c0[:, None]) & (kc[None, :] < win_c0[:, None] + NA_WIN_COLS)
    dc = np.clip(kc[None, :] - cols[:, None] + NA_WIN_COLS - 1, 0, 2 * NA_WIN_COLS - 2)
    dr = np.arange(NA_WIN_ROWS)[:, None] + np.arange(NA_WIN_ROWS)[None, :]
    tab = rpb.astype(F32)[:, dr][:, :, :, dc]
    tab = jnp.where(in_win[None, None, None], tab, NEG)
    tab = tab.transpose(0, 1, 3, 2, 4)
    return tab.reshape(NA_HEADS, NA_WIN_ROWS, GRID_W, NA_WIN_ROWS * GRID_W)


def na_attention(z, bias_tab):
    b, t, _ = z.shape
    t_lat = t - CTX_LEN
    n_pairs = NA_HEADS // 2
    return pl.pallas_call(
        functools.partial(_na_kernel, rows=t_lat // GRID_W),
        grid=(n_pairs, b),
        in_specs=[pl.BlockSpec((1, t, 128), lambda hi, bi: (bi, 0, hi)),
                  pl.BlockSpec((1, t, 128), lambda hi, bi: (bi, 0, n_pairs + hi)),
                  pl.BlockSpec((1, t, 128), lambda hi, bi: (bi, 0, 2 * n_pairs + hi)),
                  pl.BlockSpec((2, NA_WIN_ROWS, GRID_W, NA_WIN_ROWS * GRID_W), lambda hi, bi: (hi, 0, 0, 0))],
        out_specs=pl.BlockSpec((1, t_lat, 128), lambda hi, bi: (bi, 0, hi)),
        out_shape=jax.ShapeDtypeStruct((b, t_lat, D_MODEL), F32),
        compiler_params=_cparams(("parallel", "parallel")),
        name="na_attention",
    )(z, z, z, bias_tab)


def _router_kernel(x_ref, modl_ref, modc_ref, rw_ref, rb_ref, hb_ref, idx_ref, wt_ref, *, tile_off):
    is_ctx = (pl.program_id(1) + tile_off) * TOKEN_TILE < CTX_LEN
    sh = _mod_rows(modl_ref, modc_ref, is_ctx, 3)
    sc = _mod_rows(modl_ref, modc_ref, is_ctx, 4)
    h = x_ref[0] * (1.0 + sc) + sh
    hb_ref[0] = h.astype(BF16)
    logits = lax.dot_general(rw_ref[...], h, (((1,), (1,)), ((), ())), preferred_element_type=F32,
                             precision=_HI)
    mx = jnp.max(logits, axis=0, keepdims=True)
    ex = jnp.exp(logits - mx)
    probs = ex / jnp.sum(ex, axis=0, keepdims=True)
    sel = probs + rb_ref[...]
    tm = sel.shape[1]
    e_id = _iota2(N_EXPERTS, tm, 0)
    per_group = N_EXPERTS // N_GROUPS
    g_id = e_id // per_group

    def top1(vals):
        m1 = jnp.max(vals, axis=0, keepdims=True)
        i1 = jnp.min(jnp.where(vals == m1, e_id, N_EXPERTS), axis=0, keepdims=True)
        return m1, i1

    best_score, best = None, None
    for g in range(N_GROUPS):
        vals = jnp.where(g_id == g, sel, NEG)
        m1, i1 = top1(vals)
        m2, _ = top1(jnp.where(e_id == i1, NEG, vals))
        score = m1 + m2
        if g == 0:
            best_score, best = score, jnp.zeros_like(i1)
        else:
            better = score > best_score
            best = jnp.where(better, g, best)
            best_score = jnp.where(better, score, best_score)
    vals = jnp.where(g_id == best, sel, NEG)
    _, i1 = top1(vals)
    _, i2 = top1(jnp.where(e_id == i1, NEG, vals))
    p1 = jnp.sum(jnp.where(e_id == i1, probs, 0.0), axis=0, keepdims=True)
    p2 = jnp.sum(jnp.where(e_id == i2, probs, 0.0), axis=0, keepdims=True)
    tot = p1 + p2
    idx_ref[0] = jnp.where(_iota2(2, tm, 0) == 0, i1, i2)
    eye = _iota2(tm, tm, 0) == _iota2(tm, tm, 1)
    wt_ref[0] = jnp.where(_iota2(tm, 2, 1) == 0, _row_to_col(p1 / tot, eye), _row_to_col(p2 / tot, eye))


def moe_route(x, mods, router_w, router_b, *, tile_off):
    b, t_out, _ = x.shape
    modl, modc = mods
    return pl.pallas_call(
        functools.partial(_router_kernel, tile_off=tile_off),
        grid=(b, t_out // TOKEN_TILE),
        in_specs=[pl.BlockSpec((1, TOKEN_TILE, D_MODEL), lambda bi, ti: (bi, ti, 0)),
                  pl.BlockSpec((1, 1, 6 * D_MODEL), lambda bi, ti: (bi, 0, 0)),
                  pl.BlockSpec((1, 1, 6 * D_MODEL), lambda bi, ti: (0, 0, 0)),
                  pl.BlockSpec((N_EXPERTS, D_MODEL), lambda bi, ti: (0, 0)),
                  pl.BlockSpec((N_EXPERTS, 1), lambda bi, ti: (0, 0))],
        out_specs=[pl.BlockSpec((1, TOKEN_TILE, D_MODEL), lambda bi, ti: (bi, ti, 0)),
                   pl.BlockSpec((1, 2, TOKEN_TILE), lambda bi, ti: (bi, 0, ti)),
                   pl.BlockSpec((1, TOKEN_TILE, 2), lambda bi, ti: (bi, ti, 0))],
        out_shape=[jax.ShapeDtypeStruct((b, t_out, D_MODEL), BF16),
                   jax.ShapeDtypeStruct((b, 2, t_out), jnp.int32),
                   jax.ShapeDtypeStruct((b, t_out, 2), F32)],
        compiler_params=_cparams(("parallel", "parallel")),
        name="moe_route",
    )(x, modl, modc, router_w.T, router_b.reshape(-1, 1))


def _ffn_kernel(te_ref, nt_ref, x_ref, w1_ref, w3_ref, w2_ref, o_ref):
    @pl.when(pl.program_id(0) < nt_ref[0])
    def _():
        xb = x_ref[...]
        h1 = jnp.dot(xb, w1_ref[0, 0].astype(BF16), preferred_element_type=F32)
        h3 = jnp.dot(xb, w3_ref[0, 0].astype(BF16), preferred_element_type=F32)
        hid = (_silu(h1) * h3).astype(BF16)
        o_ref[...] = jnp.dot(hid, w2_ref[0, 0].astype(BF16), preferred_element_type=F32).astype(o_ref.dtype)

    @pl.when(pl.program_id(0) >= nt_ref[0])
    def _():
        o_ref[...] = jnp.zeros_like(o_ref)


def expert_ffn(xs, tile_expert, n_tiles_used, w1, w3, w2, layer):
    p = xs.shape[0]
    grid_spec = pltpu.PrefetchScalarGridSpec(
        num_scalar_prefetch=2,
        grid=(p // FFN_TILE,),
        in_specs=[pl.BlockSpec((FFN_TILE, D_MODEL), lambda i, te, nt: (i, 0)),
                  pl.BlockSpec((1, 1, D_MODEL, D_EXPERT), lambda i, te, nt: (layer, te[i], 0, 0)),
                  pl.BlockSpec((1, 1, D_MODEL, D_EXPERT), lambda i, te, nt: (layer, te[i], 0, 0)),
                  pl.BlockSpec((1, 1, D_EXPERT, D_MODEL), lambda i, te, nt: (layer, te[i], 0, 0))],
        out_specs=pl.BlockSpec((FFN_TILE, D_MODEL), lambda i, te, nt: (i, 0)),
    )
    return pl.pallas_call(
        _ffn_kernel,
        grid_spec=grid_spec,
        out_shape=jax.ShapeDtypeStruct((p, D_MODEL), BF16),
        compiler_params=_cparams(("arbitrary",)),
        name="expert_ffn",
    )(tile_expert, n_tiles_used, xs, w1, w3, w2)


def _combine_ln_kernel(x_ref, y0_ref, y1_ref, wt_ref, modl_ref, modc_ref, g_ref, b_ref, o_ref, *, tile_off):
    is_ctx = (pl.program_id(1) + tile_off) * TOKEN_TILE < CTX_LEN
    gate = _mod_rows(modl_ref, modc_ref, is_ctx, 5)
    wt = wt_ref[0]
    f = wt[:, 0:1] * y0_ref[0, 0].astype(F32) + wt[:, 1:2] * y1_ref[0, 0].astype(F32)
    r = ALPHA * x_ref[0] + gate * f
    o_ref[0] = _layer_norm_rows(r, g_ref[...], b_ref[...])


def combine_ln(x, y01, wt, mods, ln_g, ln_b, *, tile_off):
    _, b, t_out, _ = y01.shape
    modl, modc = mods
    tok = lambda bi, ti: (bi, ti, 0)
    return pl.pallas_call(
        functools.partial(_combine_ln_kernel, tile_off=tile_off),
        grid=(b, t_out // TOKEN_TILE),
        in_specs=[pl.BlockSpec((1, TOKEN_TILE, D_MODEL), tok),
                  pl.BlockSpec((1, 1, TOKEN_TILE, D_MODEL), lambda bi, ti: (0, bi, ti, 0)),
                  pl.BlockSpec((1, 1, TOKEN_TILE, D_MODEL), lambda bi, ti: (1, bi, ti, 0)),
                  pl.BlockSpec((1, TOKEN_TILE, 2), tok),
                  pl.BlockSpec((1, 1, 6 * D_MODEL), lambda bi, ti: (bi, 0, 0)),
                  pl.BlockSpec((1, 1, 6 * D_MODEL), lambda bi, ti: (0, 0, 0)),
                  pl.BlockSpec((1, D_MODEL), lambda bi, ti: (0, 0)),
                  pl.BlockSpec((1, D_MODEL), lambda bi, ti: (0, 0))],
        out_specs=pl.BlockSpec((1, TOKEN_TILE, D_MODEL), tok),
        out_shape=jax.ShapeDtypeStruct((b, t_out, D_MODEL), F32),
        compiler_params=_cparams(("parallel", "parallel")),
        name="moe_combine_ln",
    )(x, y01, y01, wt, modl, modc, ln_g.reshape(1, -1), ln_b.reshape(1, -1))


def moe_layer(x, mods, router_w, router_b, w1, w3, w2, layer, ln_g, ln_b, *, tile_off):
    hb, idx, wt = moe_route(x, mods, router_w, router_b, tile_off=tile_off)
    b, t, _ = hb.shape
    n_tok = b * t
    n_pair = 2 * n_tok
    e_flat = idx.transpose(1, 0, 2).reshape(n_pair)
    onehot = (e_flat[:, None] == jnp.arange(N_EXPERTS)[None, :]).astype(jnp.int32)
    csum = jnp.cumsum(onehot, axis=0)
    counts = csum[-1]
    rank = jnp.sum((csum - onehot) * onehot, axis=1)
    padded = ((counts + FFN_TILE - 1) // FFN_TILE) * FFN_TILE
    ends = jnp.cumsum(padded)
    offs = ends - padded
    pos = offs[e_flat] + rank
    n_rows = n_pair + N_EXPERTS * FFN_TILE
    n_tiles = n_rows // FFN_TILE
    src = jnp.zeros((n_rows,), jnp.int32).at[pos].set(
        jnp.arange(n_pair, dtype=jnp.int32) % n_tok, mode="promise_in_bounds", unique_indices=True)
    tile_start = jnp.arange(n_tiles, dtype=jnp.int32) * FFN_TILE
    tile_expert = jnp.minimum(jnp.searchsorted(ends, tile_start, side="right"), N_EXPERTS - 1).astype(jnp.int32)
    n_used = (ends[-1] // FFN_TILE).astype(jnp.int32).reshape(1)
    xs = hb.reshape(n_tok, D_MODEL).at[src].get(mode="promise_in_bounds")
    ys = expert_ffn(xs, tile_expert, n_used, w1, w3, w2, layer)
    y01 = ys.at[pos].get(mode="promise_in_bounds", unique_indices=True).reshape(2, b, t, D_MODEL)
    return combine_ln(x, y01, wt, mods, ln_g, ln_b, tile_off=tile_off)


def _gate_rows(g):
    b, t = g.shape[:2]
    n = t // CHUNK
    ncp = 8 * ((n + 7) // 8)
    g = g.transpose(2, 3, 0, 4, 1).reshape(2, 2, b, g.shape[-1], n, CHUNK)
    return jnp.pad(g, ((0, 0),) * 4 + ((0, ncp - n), (0, 0)))


def gdn_layer(x, mods, w_in, conv_w, a_log, dt_bias, norm_g, w_out):
    wq = GDN_HEADS * GDN_DK
    n_main = 4 * wq
    z, ab = linear(x, w_in[:, :n_main].astype(BF16), mods=mods, modulate=(0, 1), w_small=w_in[:, n_main:])
    qkv = seq_prep(z, 3 * GDN_HEADS, conv_w=conv_w, norm_blocks=2 * GDN_HEADS, rope_blocks=2 * GDN_HEADS,
                   scale_blocks=GDN_HEADS, scale=GDN_DK ** -0.5)
    b, t, _ = x.shape
    ab = ab.reshape(b, t, 2, 2, GDN_HEADS)
    ab = ab.at[:, :, :, 0].add(dt_bias[None, None])
    rows = _gate_rows(ab)
    log_alpha = -jnp.exp(a_log)[:, None, :, None, None] * jax.nn.softplus(rows[:, 0])
    beta = jax.nn.sigmoid(rows[:, 1])
    valid = (jnp.arange(rows.shape[4]) < t // CHUNK)[:, None]
    log_alpha = jnp.where(valid, log_alpha, 0.0)
    o = gdn_scan(qkv, log_alpha, beta)
    return o, w_out, (GDN_DV, False, _silu), (z, 3, jnp.tile(norm_g, GDN_HEADS))


def mlstm_layer(x, mods, w_in, gate_b, norm_g, w_out):
    wq = MLSTM_HEADS * MLSTM_DQK
    wv = MLSTM_HEADS * MLSTM_DV
    n_main = 2 * wq + 2 * wv
    z, gt = linear(x, w_in[:, :n_main].astype(BF16), mods=mods, modulate=(0, 1), w_small=w_in[:, n_main:])
    qk = seq_prep(z, 2 * MLSTM_HEADS, norm_blocks=0, rope_blocks=2 * MLSTM_HEADS, scale_blocks=MLSTM_HEADS,
                  scale=MLSTM_DQK ** -0.5)
    b, t, _ = x.shape
    gt = gt.reshape(b, t, 2, 2, MLSTM_HEADS) + gate_b[None, None]
    rows = _gate_rows(gt)
    valid = (jnp.arange(rows.shape[4]) < t // CHUNK)[:, None]
    i_pre = rows[:, 0]
    log_f = jnp.where(valid, jax.nn.log_sigmoid(rows[:, 1]), 0.0)
    h = mlstm_scan(qk, z, i_pre, log_f)
    return h, w_out, (MLSTM_DV, True, _sigmoid), (z, 2, norm_g)


RWKV_IN_COLS = 3584


def _rwkv_in_kernel(x_ref, modl_ref, modc_ref, w_ref, o_ref, hd_s):
    t = x_ref.shape[1]

    @pl.when(pl.program_id(1) == 0)
    def _():
        row = _iota2(t, 1, 0)
        is_ctx = row < CTX_LEN
        sel = lambda i: jnp.where(is_ctx, modc_ref[0, :, i * D_MODEL:(i + 1) * D_MODEL],
                                  modl_ref[0, :, i * D_MODEL:(i + 1) * D_MODEL])
        h = x_ref[0] * (1.0 + sel(1)) + sel(0)
        seg_first = (row == 0) | (row == CTX_LEN)
        seg_last = (row == CTX_LEN - 1) | (row == t - 1)
        h_prev = jnp.where(seg_first, 0.0, pltpu.roll(h, 1, axis=0))
        h_next = jnp.where(seg_last, 0.0, pltpu.roll(h, t - 1, axis=0))
        hd_s[:, :D_MODEL] = h.astype(BF16)
        hd_s[:, D_MODEL:] = (0.5 * (h_prev + h_next) - h).astype(BF16)

    step = 3 * TOKEN_TILE
    for i in range(t // step):
        rows = slice(i * step, (i + 1) * step)
        o_ref[0, rows, :] = jnp.dot(hd_s[rows, :], w_ref[...], preferred_element_type=F32)


def rwkv_in_proj(x, mods, w_big):
    b, t, _ = x.shape
    modl, modc = mods
    n = w_big.shape[1]
    return pl.pallas_call(
        _rwkv_in_kernel,
        grid=(b, n // N_CHUNK_COLS),
        in_specs=[pl.BlockSpec((1, t, D_MODEL), lambda bi, ji: (bi, 0, 0)),
                  pl.BlockSpec((1, 1, 6 * D_MODEL), lambda bi, ji: (bi, 0, 0)),
                  pl.BlockSpec((1, 1, 6 * D_MODEL), lambda bi, ji: (0, 0, 0)),
                  pl.BlockSpec((2 * D_MODEL, N_CHUNK_COLS), lambda bi, ji: (0, ji))],
        out_specs=pl.BlockSpec((1, t, N_CHUNK_COLS), lambda bi, ji: (bi, 0, ji)),
        out_shape=jax.ShapeDtypeStruct((b, t, n), F32),
        scratch_shapes=[pltpu.VMEM((t, 2 * D_MODEL), BF16)],
        compiler_params=_cparams(("parallel", "arbitrary")),
        name="rwkv_in_proj",
    )(x, modl, modc, w_big)


def _seg64_sums(x):
    left = _iota2(1, 128, 1) < RWKV_HEAD
    parts = []
    for blk_i in range(x.shape[1] // 128):
        blk = x[:, blk_i * 128:(blk_i + 1) * 128]
        s_l = jnp.sum(jnp.where(left, blk, 0.0), axis=1, keepdims=True)
        s_r = jnp.sum(jnp.where(left, 0.0, blk), axis=1, keepdims=True)
        parts.append(jnp.where(left, s_l, s_r))
    return jnp.concatenate(parts, axis=1)


def _softplus(x):
    return jnp.maximum(x, 0.0) + jnp.log(1.0 + jnp.exp(-jnp.abs(x)))


def _rwkv_mid_kernel(z_ref, w2_ref, a2_ref, g2_ref, vec_ref, r_ref, v_ref, kk_ref, lw0_ref, lw1_ref,
                     kd0_ref, kd1_ref, bv0_ref, bv1_ref, g_ref):
    d = D_MODEL
    z = z_ref[0]
    r, k, v = z[:, :d], z[:, d:2 * d], z[:, 2 * d:3 * d]
    zg, zw, za = z[:, 3 * d:3 * d + 128], z[:, 3 * d + 128:3 * d + 256], z[:, 3 * d + 256:3 * d + 384]
    vec = vec_ref[...]
    k_k, k_a = vec[4:5], vec[5:6]
    r_ref[0] = r
    v_ref[0] = v
    g_ref[0] = _mm(_sigmoid(zg), g2_ref[...])
    kx = k * k_k
    kk = kx * lax.rsqrt(_seg64_sums(kx * kx) + 1e-6)
    kk_ref[0] = kk
    tw = jnp.tanh(zw)
    for dr, (lw_ref, kd_ref, bv_ref) in enumerate(((lw0_ref, kd0_ref, bv0_ref), (lw1_ref, kd1_ref, bv1_ref))):
        w_raw = -_softplus(-(vec[dr:dr + 1] + _mm(tw, w2_ref[dr]))) - 0.5
        a = _sigmoid(vec[2 + dr:3 + dr] + _mm(za, a2_ref[dr]))
        lw_ref[0] = -jnp.exp(w_raw)
        kd_ref[0] = k * (1.0 + (a - 1.0) * k_a)
        bv_ref[0] = kk * a


def rwkv_mid(z1, w2p, a2p, g2, vec):
    b, t, n = z1.shape
    tok = pl.BlockSpec((1, TOKEN_TILE, D_MODEL), lambda bi, ti: (bi, ti, 0))
    full = lambda a: pl.BlockSpec(a.shape, lambda bi, ti: (0,) * a.ndim)
    return pl.pallas_call(
        _rwkv_mid_kernel,
        grid=(b, t // TOKEN_TILE),
        in_specs=[pl.BlockSpec((1, TOKEN_TILE, n), lambda bi, ti: (bi, ti, 0)),
                  full(w2p), full(a2p), full(g2), full(vec)],
        out_specs=[tok] * 10,
        out_shape=[jax.ShapeDtypeStruct((b, t, D_MODEL), F32)] * 10,
        compiler_params=_cparams(("parallel", "parallel")),
        name="rwkv_mid",
    )(z1, w2p, a2p, g2, vec)


def _rwkv_post(y, r, v, kd0, kd1, g, params):
    inv = 1.0 / RWKV_HEAD
    yc = y - _seg64_sums(y) * inv
    yn = yc * lax.rsqrt(_seg64_sums(yc * yc) * inv + RWKV_GN_EPS)
    bonus = _seg64_sums(r * (kd0 + kd1) * params[2:3]) * v
    return (yn * params[0:1] + params[1:2] + bonus) * g


def rwkv_layer(x, mods, mu, w_rkv, w0, w1, w2, a0, a1, a2, g1, g2, k_k, k_a, r_k, lnx_g, lnx_b, w_out):
    d = D_MODEL
    cols = [(w_rkv[0], 0), (w_rkv[1], 2), (w_rkv[2], 3), (g1, 5), (w1[0], 1), (w1[1], 1), (a1[0], 4), (a1[1], 4)]
    top = jnp.concatenate([w for w, _ in cols], axis=1)
    bot = jnp.concatenate([mu[j][:, None] * w for w, j in cols], axis=1)
    w_big = jnp.pad(jnp.concatenate([top, bot], axis=0), ((0, 0), (0, RWKV_IN_COLS - top.shape[1]))).astype(BF16)
    z1 = rwkv_in_proj(x, mods, w_big)
    pad_dir = lambda w: jnp.stack([jnp.pad(w[0], ((0, w.shape[1]), (0, 0))), jnp.pad(w[1], ((w.shape[1], 0), (0, 0)))])
    vec = jnp.concatenate([w0, a0, k_k[None], k_a[None], jnp.zeros((2, d), F32)], axis=0)
    r, v, kk, lw0, lw1, kd0, kd1, bv0, bv1, g = rwkv_mid(z1, pad_dir(w2), pad_dir(a2), g2, vec)
    y = rwkv_scan(r, v, kk, (lw0, lw1), (kd0, kd1), (bv0, bv1))
    params = jnp.concatenate([lnx_g[None], lnx_b[None], r_k.reshape(1, d), jnp.zeros((5, d), F32)], axis=0)
    return y, w_out, "rwkv", (r, v, kd0, kd1, g, params)


def kernel(x, c, ctx, c_ctx, ada_w, ada_b, ln_g, ln_b, router_w, router_b, moe_w1, moe_w3, moe_w2, gdn_w_in, gdn_conv, gdn_a_log, gdn_dt_bias, gdn_norm_g, gdn_w_out, mlstm_w_in, mlstm_gate_b, mlstm_norm_g, mlstm_w_out, rwkv_mu, rwkv_w_rkv, rwkv_w0, rwkv_w1, rwkv_w2, rwkv_a0, rwkv_a1, rwkv_a2, rwkv_g1, rwkv_g2, rwkv_k_k, rwkv_k_a, rwkv_r_k, rwkv_lnx_g, rwkv_lnx_b, rwkv_w_out, na_w_in, na_rpb, na_w_out):
    b = x.shape[0]
    ctx_tiles = CTX_LEN // TOKEN_TILE
    mod_all = modulation_all(c, c_ctx, ada_w, ada_b)
    xs = jnp.concatenate([ctx, x], axis=1)
    assert DEPTH == 4
    for i in range(DEPTH):
        mods = (mod_all[i, :b, None, :], mod_all[i, b:b + 1, None, :])
        if i % 4 == 0:
            y, w_out, post, post_args = gdn_layer(xs, mods, gdn_w_in, gdn_conv, gdn_a_log, gdn_dt_bias,
                                                  gdn_norm_g, gdn_w_out)
        elif i % 4 == 1:
            y, w_out, post, post_args = mlstm_layer(xs, mods, mlstm_w_in, mlstm_gate_b, mlstm_norm_g, mlstm_w_out)
        elif i % 4 == 2:
            y, w_out, post, post_args = rwkv_layer(xs, mods, rwkv_mu, rwkv_w_rkv, rwkv_w0, rwkv_w1, rwkv_w2, rwkv_a0, rwkv_a1,
                                  rwkv_a2, rwkv_g1, rwkv_g2, rwkv_k_k, rwkv_k_a, rwkv_r_k, rwkv_lnx_g,
                                  rwkv_lnx_b, rwkv_w_out)
        else:
            z = linear(xs, na_w_in.astype(BF16), mods=mods, modulate=(0, 1))
            y, w_out, post, post_args = na_attention(z, _na_bias_table(na_rpb)), na_w_out, None, None
        off = ctx_tiles if (i % 4 == 3) else 0
        xs1 = out_proj_ln(y, w_out.astype(BF16), xs, mods, ln_g[i, 0], ln_b[i, 0], gate_idx=2, tile_off=off,
                          post=post, post_args=post_args)
        xs = moe_layer(xs1, mods, router_w, router_b, moe_w1, moe_w3, moe_w2, i, ln_g[i, 1], ln_b[i, 1],
                       tile_off=off)
    return xs
```
